```python
import math
import jax
import jax.numpy as jnp
from jax import lax
import numpy as np

D_MODEL = 1024
BATCH = 8
SEQ = 2048
DEPTH = 2
DEC_BATCH = 128
DEC_SEQ = 8
PAST_LEN = 2048
PAGE_SIZE = 128

D_MIX = D_MODEL
HEAD_DIM = 64
D_ATT = (3 * D_MIX) // 8
D_REC = (3 * D_MIX) // 8
D_CONV = D_MIX - D_ATT - D_REC
N_HEADS_ATT = D_ATT // HEAD_DIM
N_HEADS_REC = D_REC // HEAD_DIM
DILATED_BRANCHES = ((128, 1), (512, 4), (2048, 16))
WIN_MAX = max(w for w, _ in DILATED_BRANCHES)
ROPE_THETA = 10000.0
CONV_WIDTH = 3
D_FF = -(-8 * D_MODEL // (3 * 256)) * 256
D_IN_PROJ = 3 * D_ATT + 4 * D_REC + 3 * D_CONV
Q_BLOCK = 128
REC_CHUNK = 64
NORM_EPS = 1e-6

kernel_name = 'hymba_dilated_hgrn2_shortconv_step'


def _rms(x):
    xf = x.astype(jnp.float32)
    return xf * lax.rsqrt(jnp.mean(xf * xf, axis=-1, keepdims=True) + NORM_EPS)


def _modulate(x, shift, scale):
    h = _rms(x) * (1.0 + scale[:, None, :].astype(jnp.float32)) + shift[:, None, :].astype(jnp.float32)
    return h.astype(x.dtype)


def _rope(x, pos):
    half = HEAD_DIM // 2
    freqs = ROPE_THETA ** (-jnp.arange(half, dtype=jnp.float32) / half)
    ang = pos.astype(jnp.float32)[:, None] * freqs[None, :]
    cos = jnp.cos(ang)[None, :, None, :]
    sin = jnp.sin(ang)[None, :, None, :]
    xf = x.astype(jnp.float32)
    x1, x2 = xf[..., :half], xf[..., half:]
    return jnp.concatenate([x1 * cos - x2 * sin, x2 * cos + x1 * sin], axis=-1).astype(x.dtype)


def _dilated_attention(q, k_all, v_all, q_idx):
    B, T, H, hd = q.shape
    qb = math.gcd(T, Q_BLOCK)
    nb = T // qb
    q_blocks = q.reshape(B, nb, qb, H, hd).transpose(1, 0, 2, 3, 4)
    idx_blocks = q_idx.reshape(nb, qb)
    scale = hd ** -0.5

    def block(args):
        qblk, qi = args
        outs, lses = [], []
        for win, dil in DILATED_BRANCHES:
            j = jnp.arange(win // dil + 1, dtype=jnp.int32)
            idx = qi[:, None] - dil * j[None, :]
            valid = idx >= 0
            idx = jnp.maximum(idx, 0)
            kg = k_all[:, idx]
            vg = v_all[:, idx]
            s = jnp.einsum('bqhd,bqjhd->bhqj', qblk, kg).astype(jnp.float32) * scale
            s = jnp.where(valid[None, None], s, -jnp.inf)
            m = jnp.max(s, axis=-1, keepdims=True)
            p = jnp.exp(s - m)
            den = jnp.sum(p, axis=-1, keepdims=True)
            o = jnp.einsum('bhqj,bqjhd->bqhd', p, vg.astype(jnp.float32)) / den.transpose(0, 2, 1, 3)
            outs.append(o)
            lses.append((m + jnp.log(den))[..., 0])
        alpha = jax.nn.softmax(jnp.stack(lses), axis=0)
        o = jnp.einsum('nbhq,nbqhd->bqhd', alpha, jnp.stack(outs))
        return o.astype(q.dtype)

    out = lax.map(block, (q_blocks, idx_blocks))
    return out.transpose(1, 0, 2, 3, 4).reshape(B, T, H, hd)


def _hgrn2(q, log_f, k, v, s0):
    B, T, H, K = q.shape
    C = math.gcd(T, REC_CHUNK)
    nc = T // C

    def chunks(a):
        return a.astype(jnp.float32).reshape(B, nc, C, H, a.shape[-1]).transpose(1, 0, 3, 2, 4)

    tri = jnp.tril(jnp.ones((C, C), dtype=bool))

    def step(S, inp):
        qc, lfc, kc, vc = inp
        b = jnp.cumsum(lfc, axis=2)
        diff = b[:, :, :, None, :] - b[:, :, None, :, :]
        decay = jnp.where(tri[:, :, None], jnp.exp(jnp.minimum(diff, 0.0)), 0.0)
        a = jnp.einsum('bhtk,bhsk,bhtsk->bhts', qc, kc, decay)
        o = jnp.einsum('bhts,bhsv->bhtv', a, vc) + jnp.einsum('bhtk,bhkv->bhtv', qc * jnp.exp(b), S)
        b_last = b[:, :, -1:, :]
        S = jnp.exp(b_last[:, :, 0, :])[..., None] * S + jnp.einsum('bhsk,bhsv->bhkv', kc * jnp.exp(b_last - b), vc)
        return S, o

    S_fin, o = lax.scan(step, s0.astype(jnp.float32), (chunks(q), chunks(log_f), chunks(k), chunks(v)))
    o = o.transpose(1, 0, 3, 2, 4).reshape(B, T, H, v.shape[-1])
    return o, S_fin


def _short_conv(u, buf, w, b):
    T = u.shape[1]
    up = jnp.concatenate([buf.astype(u.dtype), u], axis=1)
    y = up[:, 0:T] * w[0]
    for i in range(1, CONV_WIDTH):
        y = y + up[:, i:i + T] * w[i]
    return y + b, up[:, -(CONV_WIDTH - 1):]


def _layer(x, c, pos, k_buf, v_buf, s0, conv0, lb, w_ada, b_ada, w_in, conv_w, conv_b,
           norm_w, w_out, w_ffn_in, w_ffn_out):
    B, T, _ = x.shape
    mod = jax.nn.silu(c) @ w_ada + b_ada
    sh1, sc1, g1, sh2, sc2, g2 = jnp.split(mod, 6, axis=-1)
    h = _modulate(x, sh1, sc1)
    proj = h @ w_in
    cuts = np.cumsum([D_ATT] * 3 + [D_REC] * 4 + [D_CONV] * 2).tolist()
    qa, ka, va, qr, fr, ir, gr, bc, cc, xc = jnp.split(proj, cuts, axis=-1)

    def heads(a):
        return a.reshape(B, T, -1, HEAD_DIM)

    qa = _rope(heads(qa), pos)
    ka = _rope(heads(ka), pos)
    va = heads(va)
    if k_buf is None:
        k_all, v_all, off = ka, va, 0
    else:
        k_all = jnp.concatenate([k_buf.astype(ka.dtype), ka], axis=1)
        v_all = jnp.concatenate([v_buf.astype(va.dtype), va], axis=1)
        off = k_buf.shape[1]
    q_idx = off + jnp.arange(T, dtype=jnp.int32)
    oa = _dilated_attention(qa, k_all, v_all, q_idx).reshape(B, T, D_ATT)
    keep = min(WIN_MAX, T)
    k_new, v_new = ka[:, T - keep:], va[:, T - keep:]

    z = fr.astype(jnp.float32)
    log_f = jnp.logaddexp(jnp.log(lb), jnp.log1p(-lb) + jax.nn.log_sigmoid(z))
    k_r = (1.0 - lb) * jax.nn.sigmoid(-z)
    o_r, s_new = _hgrn2(heads(qr), heads(log_f), heads(k_r), heads(ir), s0)
    o_r = (_rms(o_r).reshape(B, T, D_REC) * norm_w.astype(jnp.float32)
           * jax.nn.silu(gr.astype(jnp.float32))).astype(x.dtype)

    y_c, conv_new = _short_conv(cc * xc, conv0, conv_w, conv_b)
    oc = bc * y_c

    mix = jnp.concatenate([oa, o_r, oc.astype(x.dtype)], axis=-1) @ w_out
    x = x + g1[:, None, :] * mix

    h2 = _modulate(x, sh2, sc2)
    gate, up = jnp.split(h2 @ w_ffn_in, 2, axis=-1)
    x = x + g2[:, None, :] * ((jax.nn.silu(gate) * up) @ w_ffn_out)
    return x, k_new, v_new, s_new.astype(x.dtype), conv_new


def _trunk(x, c, pos, cache_k, cache_v, state_hgrn, state_conv, lbs, params, final_norm_w):
    w_ada, b_ada, w_in, conv_w, conv_b, hgrn_norm_w, w_out, w_ffn_in, w_ffn_out = params
    B = x.shape[0]
    ks, vs, hs, cs = [], [], [], []
    for l in range(DEPTH):
        if cache_k is None:
            k_buf, v_buf = None, None
            s0 = jnp.zeros((B, N_HEADS_REC, HEAD_DIM, HEAD_DIM), jnp.float32)
            conv0 = jnp.zeros((B, CONV_WIDTH - 1, D_CONV), x.dtype)
        else:
            k_buf, v_buf, s0, conv0 = cache_k[l], cache_v[l], state_hgrn[l], state_conv[l]
        x, k_new, v_new, s_new, conv_new = _layer(
            x, c, pos, k_buf, v_buf, s0, conv0, lbs[l], w_ada[l], b_ada[l], w_in[l], conv_w[l],
            conv_b[l], hgrn_norm_w[l], w_out[l], w_ffn_in[l], w_ffn_out[l])
        ks.append(k_new)
        vs.append(v_new)
        hs.append(s_new)
        cs.append(conv_new)
    y = (_rms(x) * final_norm_w.astype(jnp.float32)).astype(x.dtype)
    return y, jnp.stack(ks), jnp.stack(vs), jnp.stack(hs), jnp.stack(cs)


def setup_inputs(seed: int = 0) -> dict:
    key = jax.random.key(seed)
    ks = jax.random.split(key, 20)
    wbuf = min(WIN_MAX, PAST_LEN)
    f32 = jnp.float32
    nrm = lambda k, s: jax.random.normal(k, s, f32)
    return {
        'x_prompt': nrm(ks[0], (BATCH, SEQ, D_MODEL)),
        'x_sample': nrm(ks[1], (DEC_BATCH, DEC_SEQ, D_MODEL)),
        'cache_k': nrm(ks[2], (DEPTH, DEC_BATCH, wbuf, N_HEADS_ATT, HEAD_DIM)),
        'cache_v': nrm(ks[3], (DEPTH, DEC_BATCH, wbuf, N_HEADS_ATT, HEAD_DIM)),
        'state_hgrn': 0.5 * nrm(ks[4], (DEPTH, DEC_BATCH, N_HEADS_REC, HEAD_DIM, HEAD_DIM)),
        'state_conv': nrm(ks[5], (DEPTH, DEC_BATCH, CONV_WIDTH - 1, D_CONV)),
        'c_prompt': nrm(ks[6], (BATCH, D_MODEL)),
        'c_sample': nrm(ks[7], (DEC_BATCH, D_MODEL)),
        'w_ada': 0.5 * D_MODEL ** -0.5 * nrm(ks[8], (DEPTH, D_MODEL, 6 * D_MODEL)),
        'b_ada': 0.01 * nrm(ks[9], (DEPTH, 6 * D_MODEL)),
        'w_in': D_MODEL ** -0.5 * nrm(ks[10], (DEPTH, D_MODEL, D_IN_PROJ)),
        'conv_w': CONV_WIDTH ** -0.5 * nrm(ks[11], (DEPTH, CONV_WIDTH, D_CONV)),
        'conv_b': 0.01 * nrm(ks[12], (DEPTH, D_CONV)),
        'hgrn_lb_logits': 0.5 * nrm(ks[13], (DEPTH, D_REC)),
        'hgrn_norm_w': 1.0 + 0.01 * nrm(ks[14], (DEPTH, D_REC)),
        'w_out': D_MIX ** -0.5 * nrm(ks[15], (DEPTH, D_MIX, D_MODEL)),
        'w_ffn_in': D_MODEL ** -0.5 * nrm(ks[16], (DEPTH, D_MODEL, 2 * D_FF)),
        'w_ffn_out': D_FF ** -0.5 * nrm(ks[17], (DEPTH, D_FF, D_MODEL)),
        'final_norm_w': 1.0 + 0.01 * nrm(ks[18], (D_MODEL,)),
    }


def reference(x_prompt, x_sample, cache_k, cache_v, state_hgrn, state_conv, c_prompt, c_sample,
              w_ada, b_ada, w_in, conv_w, conv_b, hgrn_lb_logits, hgrn_norm_w, w_out,
              w_ffn_in, w_ffn_out, final_norm_w):
    cum = jnp.cumsum(jax.nn.softmax(hgrn_lb_logits.astype(jnp.float32), axis=0), axis=0)
    lbs = cum - cum[0:1]
    params = (w_ada, b_ada, w_in, conv_w, conv_b, hgrn_norm_w, w_out, w_ffn_in, w_ffn_out)
    pos_p = jnp.arange(x_prompt.shape[1], dtype=jnp.int32)
    pos_s = PAST_LEN + jnp.arange(x_sample.shape[1], dtype=jnp.int32)
    y_prompt, k_p, v_p, h_p, cv_p = _trunk(x_prompt, c_prompt, pos_p, None, None, None, None,
                                           lbs, params, final_norm_w)
    y_sample, k_s, v_s, h_s, cv_s = _trunk(x_sample, c_sample, pos_s, cache_k, cache_v, state_hgrn,
                                           state_conv, lbs, params, final_norm_w)
    return (y_prompt, y_sample, k_p, v_p, h_p, cv_p, k_s, v_s, h_s, cv_s)
```

```python
import functools
import math

import jax
import jax.numpy as jnp
from jax import lax
from jax.experimental import pallas as pl
from jax.experimental.pallas import tpu as pltpu

F32 = jnp.float32
BF16 = jnp.bfloat16

HEAD_DIM = 64
LANES = 128
DILATIONS = (1, 4, 16)
WINDOW_STEPS = 128
PAST_LEN = 2048
ROPE_THETA = 10000.0
NORM_EPS = 1e-6
REC_CHUNK = 16
ROW_TILE = 512
VMEM_LIMIT = 48 * 1024 * 1024


def _params(sem):
    return pltpu.CompilerParams(dimension_semantics=sem, vmem_limit_bytes=VMEM_LIMIT)


def _dot(a, b):
    return jnp.dot(a, b, preferred_element_type=F32)


def _dot_nt(a, b):
    return lax.dot_general(a, b, (((1,), (1,)), ((), ())), preferred_element_type=F32)


def _dot_tn(a, b):
    return lax.dot_general(a, b, (((0,), (0,)), ((), ())), preferred_element_type=F32)


def _split3(x):
    hi = x.astype(BF16)
    r = x - hi.astype(F32)
    mid = r.astype(BF16)
    lo = (r - mid.astype(F32)).astype(BF16)
    return hi, mid, lo


def _silu(x):
    return x * jax.nn.sigmoid(x)


def _rms(x):
    return x * lax.rsqrt(jnp.mean(x * x, axis=-1, keepdims=True) + NORM_EPS)


def _token_tile(batch, seq):
    if seq >= ROW_TILE:
        assert seq % ROW_TILE == 0
        return 1, ROW_TILE
    assert ROW_TILE % seq == 0 and seq % 8 == 0
    bb = min(batch, ROW_TILE // seq)
    assert batch % bb == 0
    return bb, seq


def _ada_body(c_ref, w_ref, b_ref, o_ref):
    s = _silu(c_ref[...])
    s_hi, s_mid, _ = _split3(s)
    w_hi, w_mid, _ = _split3(w_ref[0])
    o_ref[0] = _dot(s_hi, w_hi) + _dot(s_hi, w_mid) + _dot(s_mid, w_hi) + b_ref[0]


def _ada(c_all, w_ada, b_ada):
    depth, d, n6 = w_ada.shape
    bc = c_all.shape[0]
    tn = 512
    return pl.pallas_call(
        _ada_body,
        grid=(depth, n6 // tn),
        in_specs=[pl.BlockSpec((bc, d), lambda l, j: (0, 0)),
                  pl.BlockSpec((1, d, tn), lambda l, j: (l, 0, j)),
                  pl.BlockSpec((1, 1, tn), lambda l, j: (l, 0, j))],
        out_specs=pl.BlockSpec((1, bc, tn), lambda l, j: (l, 0, j)),
        out_shape=jax.ShapeDtypeStruct((depth, bc, n6), F32),
        compiler_params=_params(("arbitrary", "arbitrary")),
        name="ada",
    )(c_all, w_ada, b_ada.reshape(depth, 1, n6))


def _rope_tables(pos, n_heads):
    half = HEAD_DIM // 2
    freqs = ROPE_THETA ** (-jnp.arange(half, dtype=F32) / half)
    ang = pos.astype(F32)[:, None] * freqs[None, :]
    cos, sin = jnp.cos(ang), jnp.sin(ang)
    cos_t = jnp.tile(jnp.concatenate([cos, cos], axis=-1), (1, n_heads))
    sin_t = jnp.tile(jnp.concatenate([-sin, sin], axis=-1), (1, n_heads))
    return cos_t, sin_t


def _rope(x, cos, sin_signed):
    outs = []
    for c in range(x.shape[-1] // LANES):
        sl = slice(c * LANES, (c + 1) * LANES)
        xc = x[:, sl]
        lane = lax.broadcasted_iota(jnp.int32, xc.shape, 1)
        ahead = pltpu.roll(xc, LANES - HEAD_DIM // 2, axis=1)
        behind = pltpu.roll(xc, HEAD_DIM // 2, axis=1)
        rot = jnp.where((lane & (HEAD_DIM // 2)) == 0, ahead, behind)
        outs.append(xc * cos[:, sl] + rot * sin_signed[:, sl])
    return jnp.concatenate(outs, axis=-1)


def _in_proj_body(x_ref, sh_ref, sc_ref, w_ref, cos_ref, sin_ref, *out_refs, bb, tt, widths):
    d = x_ref.shape[-1]
    x = x_ref[...].reshape(bb, tt, d)
    h = _rms(x) * (1.0 + sc_ref[...]) + sh_ref[...]
    hb = h.reshape(bb * tt, d).astype(BF16)
    off = 0
    for idx, (o_ref, wd) in enumerate(zip(out_refs, widths)):
        p = _dot(hb, w_ref[:, off:off + wd])
        if idx < 2:
            p = _rope(p, cos_ref[...], sin_ref[...])
        o_ref[...] = p
        off += wd


def _in_proj(x2, shift, scale, w_in_bf, cos_t, sin_t, batch, seq, widths):
    n, d = x2.shape
    bb, tt = _token_tile(batch, seq)
    tm = bb * tt
    n_t = seq // tt
    table_blocks = cos_t.shape[0] // tm
    body = functools.partial(_in_proj_body, bb=bb, tt=tt, widths=widths)
    mod_spec = pl.BlockSpec((bb, 1, d), lambda i: (i // n_t, 0, 0))
    tab_spec = pl.BlockSpec((tm, cos_t.shape[1]), lambda i: (i % table_blocks, 0))
    return pl.pallas_call(
        body,
        grid=(n // tm,),
        in_specs=[pl.BlockSpec((tm, d), lambda i: (i, 0)), mod_spec, mod_spec,
                  pl.BlockSpec(w_in_bf.shape, lambda i: (0, 0)), tab_spec, tab_spec],
        out_specs=[pl.BlockSpec((tm, wd), lambda i: (i, 0)) for wd in widths],
        out_shape=[jax.ShapeDtypeStruct((n, wd), F32) for wd in widths],
        compiler_params=_params(("arbitrary",)),
        name="in_proj",
    )(x2, shift, scale, w_in_bf, cos_t, sin_t)


def _head_pairs(parts):
    pairs = [jnp.concatenate(parts[i:i + 2], axis=-1) for i in range(0, len(parts), 2)]
    return jnp.concatenate(pairs, axis=-1)


def _band_attn_body(*refs, has_prev, n_heads):
    if has_prev:
        q_ref, kc_ref, vc_ref, kp_ref, vp_ref, o_ref, lse_ref = refs
    else:
        q_ref, kc_ref, vc_ref, o_ref, lse_ref = refs
    blk = q_ref.shape[1]
    row = lax.broadcasted_iota(jnp.int32, (blk, blk), 0)
    col = lax.broadcasted_iota(jnp.int32, (blk, blk), 1)
    cur_ok = col <= row
    if has_prev:
        prev_ok = col - row >= jnp.where(pl.program_id(2) > 0, 0, blk)
    q = (q_ref[0] * (HEAD_DIM ** -0.5)).astype(BF16)
    kc = kc_ref[0].astype(BF16)
    vc = vc_ref[0].astype(BF16)
    if has_prev:
        kp = kp_ref[0].astype(BF16)
        vp = vp_ref[0].astype(BF16)
    outs, lses = [], []
    for h in range(n_heads):
        sl = slice(h * HEAD_DIM, (h + 1) * HEAD_DIM)
        s_c = jnp.where(cur_ok, _dot_nt(q[:, sl], kc[:, sl]), -jnp.inf)
        m = jnp.max(s_c, axis=-1, keepdims=True)
        if has_prev:
            s_p = jnp.where(prev_ok, _dot_nt(q[:, sl], kp[:, sl]), -jnp.inf)
            m = jnp.maximum(m, jnp.max(s_p, axis=-1, keepdims=True))
        p_c = jnp.exp(s_c - m)
        den = jnp.sum(p_c, axis=-1, keepdims=True)
        acc = _dot(p_c.astype(BF16), vc[:, sl])
        if has_prev:
            p_p = jnp.exp(s_p - m)
            den = den + jnp.sum(p_p, axis=-1, keepdims=True)
            acc = acc + _dot(p_p.astype(BF16), vp[:, sl])
        outs.append(acc / den)
        lses.append(jnp.broadcast_to(m + jnp.log(den), (blk, HEAD_DIM)))
    o_ref[0] = _head_pairs(outs)
    lse_ref[0] = _head_pairs(lses)


def _band_attention(q2, k2, v2, batch, seq, dil):
    d_att = q2.shape[-1]
    n_heads = d_att // HEAD_DIM
    sub = seq // dil
    blk = WINDOW_STEPS
    assert sub % blk == 0
    n_blk = sub // blk
    has_prev = n_blk > 1
    view = lambda a: a.reshape(batch, sub, dil * d_att)
    cur = pl.BlockSpec((1, blk, d_att), lambda b, r, i: (b, i, r))
    prev = pl.BlockSpec((1, blk, d_att), lambda b, r, i: (b, jnp.maximum(i - 1, 0), r))
    args = [view(q2), view(k2), view(v2)]
    specs = [cur, cur, cur]
    if has_prev:
        args += [view(k2), view(v2)]
        specs += [prev, prev]
    o, lse = pl.pallas_call(
        functools.partial(_band_attn_body, has_prev=has_prev, n_heads=n_heads),
        grid=(batch, dil, n_blk),
        in_specs=specs,
        out_specs=[cur, cur],
        out_shape=[jax.ShapeDtypeStruct((batch, sub, dil * d_att), F32)] * 2,
        compiler_params=_params(("arbitrary",) * 3),
        name=f"band_attn_d{dil}",
    )(*args)
    return o.reshape(batch * seq, d_att), lse.reshape(batch * seq, d_att)


def _branch_count(delta):
    cnt = jnp.zeros(delta.shape, F32)
    for dil in DILATIONS:
        hit = (delta >= 0) & (delta <= WINDOW_STEPS * dil) & ((delta & (dil - 1)) == 0)
        cnt = cnt + jnp.where(hit, 1.0, 0.0)
    return cnt


def _cache_attn_body(q_ref, kn_ref, vn_ref, kc_ref, vc_ref, o_ref):
    t_new = q_ref.shape[1]
    past = kc_ref.shape[2]
    n_pairs = q_ref.shape[2] // LANES
    rows = 2 * t_new
    assert t_new & (t_new - 1) == 0
    t_c = lax.broadcasted_iota(jnp.int32, (rows, past), 0) & (t_new - 1)
    j_c = lax.broadcasted_iota(jnp.int32, (rows, past), 1)
    cnt_c = _branch_count(past + t_c - j_c)
    t_n = lax.broadcasted_iota(jnp.int32, (rows, t_new), 0) & (t_new - 1)
    j_n = lax.broadcasted_iota(jnp.int32, (rows, t_new), 1)
    cnt_n = _branch_count(t_n - j_n)
    lane = lax.broadcasted_iota(jnp.int32, (t_new, LANES), 1)
    first = lane < HEAD_DIM
    outs = []
    for p in range(n_pairs):
        sl = slice(p * LANES, (p + 1) * LANES)
        qp = q_ref[0, :, sl] * (HEAD_DIM ** -0.5)
        q2 = jnp.concatenate([jnp.where(first, qp, 0.0), jnp.where(first, 0.0, qp)], axis=0)
        kc = kc_ref[0, 0, :, sl].astype(BF16)
        vc = vc_ref[0, 0, :, sl].astype(BF16)
        s_c = jnp.where(cnt_c > 0, _dot_nt(q2.astype(BF16), kc), -jnp.inf)
        s_n = jnp.where(cnt_n > 0, _dot_nt(q2, kn_ref[0, :, sl]), -jnp.inf)
        m = jnp.maximum(jnp.max(s_c, axis=-1, keepdims=True), jnp.max(s_n, axis=-1, keepdims=True))
        p_c = cnt_c * jnp.exp(s_c - m)
        p_n = cnt_n * jnp.exp(s_n - m)
        den = jnp.sum(p_c, axis=-1, keepdims=True) + jnp.sum(p_n, axis=-1, keepdims=True)
        acc = (_dot(p_c.astype(BF16), vc) + _dot(p_n, vn_ref[0, :, sl])) / den
        outs.append(jnp.where(first, acc[:t_new], acc[t_new:]))
    o_ref[0] = jnp.concatenate(outs, axis=-1)


def _cache_attention(q2, kn2, vn2, cache_k, cache_v, layer, batch, seq):
    d_att = q2.shape[-1]
    past = cache_k.shape[2]
    ck = cache_k.reshape(cache_k.shape[0], batch, past, d_att)
    cv = cache_v.reshape(cache_v.shape[0], batch, past, d_att)
    new = pl.BlockSpec((1, seq, d_att), lambda b: (b, 0, 0))
    old = pl.BlockSpec((1, 1, past, d_att), lambda b: (layer, b, 0, 0))
    view = lambda a: a.reshape(batch, seq, d_att)
    o = pl.pallas_call(
        _cache_attn_body,
        grid=(batch,),
        in_specs=[new, new, new, old, old],
        out_specs=new,
        out_shape=jax.ShapeDtypeStruct((batch, seq, d_att), F32),
        compiler_params=_params(("arbitrary",)),
        name="cache_attn",
    )(view(q2), view(kn2), view(vn2), ck, cv)
    return o.reshape(batch * seq, d_att)


def _lower_bound(logits, layer):
    e = jnp.exp(logits - jnp.max(logits, axis=0, keepdims=True))
    sm = e / jnp.sum(e, axis=0, keepdims=True)
    lb = jnp.zeros_like(sm[0:1])
    for j in range(1, layer + 1):
        lb = lb + sm[j:j + 1]
    return lb


def _hgrn_body(lbl_ref, nw_ref, q_ref, f_ref, i_ref, g_ref, s0_ref, o_ref, sout_ref, st_ref,
               *, layer, t_valid, n_chunks):
    c = REC_CHUNK
    d_rec = q_ref.shape[-1]
    n_pairs = d_rec // LANES
    rows_in = q_ref.shape[1]

    @pl.when(pl.program_id(1) == 0)
    def _():
        st_ref[...] = s0_ref[0]

    lb = _lower_bound(lbl_ref[...], layer)
    log_lb = jnp.log(lb)
    log_1m_lb = jnp.log1p(-lb)
    nw = nw_ref[...]

    r_i = lax.broadcasted_iota(jnp.int32, (c, c), 0)
    c_i = lax.broadcasted_iota(jnp.int32, (c, c), 1)
    tri = jnp.where(c_i <= r_i, 1.0, 0.0).astype(BF16)
    l_r = lax.broadcasted_iota(jnp.int32, (LANES, LANES), 0) >> (HEAD_DIM.bit_length() - 1)
    l_c = lax.broadcasted_iota(jnp.int32, (LANES, LANES), 1) >> (HEAD_DIM.bit_length() - 1)
    same_head = l_r == l_c
    head_sum = jnp.where(same_head, 1.0, 0.0).astype(BF16)
    s_r = lax.broadcasted_iota(jnp.int32, (c, c * c), 0)
    s_c = lax.broadcasted_iota(jnp.int32, (c, c * c), 1) >> (c.bit_length() - 1)
    pick_t = jnp.where(s_r == s_c, 1.0, 0.0).astype(BF16)
    row = lax.broadcasted_iota(jnp.int32, (c, LANES), 0)

    def load(ref, r0):
        if rows_in >= c:
            return ref[0, pl.ds(r0, c), :]
        pad = jnp.zeros((c - rows_in, d_rec), F32)
        return jnp.concatenate([ref[0], pad], axis=0)

    def chunk(ci, carry):
        r0 = pl.multiple_of(ci * c, c)
        z = load(f_ref, r0)
        log_sig = jnp.minimum(z, 0.0) - jnp.log1p(jnp.exp(-jnp.abs(z)))
        b_term = log_1m_lb + log_sig
        log_f = jnp.maximum(log_lb, b_term) + jnp.log1p(jnp.exp(-jnp.abs(log_lb - b_term)))
        kk = (1.0 - lb) * jax.nn.sigmoid(-z)
        if t_valid < c:
            live = lax.broadcasted_iota(jnp.int32, (c, d_rec), 0) < t_valid
            log_f = jnp.where(live, log_f, 0.0)
            kk = jnp.where(live, kk, 0.0)
        f_hi, f_mid, f_lo = _split3(log_f)
        b_all = _dot(tri, f_hi) + _dot(tri, f_mid) + _dot(tri, f_lo)
        q_all = load(q_ref, r0)
        v_all = load(i_ref, r0)
        g_all = load(g_ref, r0)
        outs = []
        for p in range(n_pairs):
            sl = slice(p * LANES, (p + 1) * LANES)
            b, q, k, v = b_all[:, sl], q_all[:, sl], kk[:, sl], v_all[:, sl]
            st = st_ref[p]
            xs = []
            for t in range(c):
                e = jnp.exp(jnp.minimum(b[t:t + 1, :] - b, 0.0))
                xs.append(jnp.where(row <= t, (q[t:t + 1, :] * e) * k, 0.0))
            x = jnp.concatenate(xs, axis=0).astype(BF16)
            a_rep = _dot(x, head_sum)
            y = (a_rep * jnp.concatenate([v] * c, axis=0)).astype(BF16)
            o = _dot(pick_t, y)
            o = o + _dot_nt((q * jnp.exp(b)).astype(BF16), st.astype(BF16))
            b_last = b[c - 1:c, :]
            k_hat = (k * jnp.exp(b_last - b)).astype(BF16)
            upd = _dot_tn(v.astype(BF16), k_hat)
            st_ref[p] = st * jnp.exp(b_last) + jnp.where(same_head, upd, 0.0)
            sq_hi, sq_mid, _ = _split3(o * o)
            ms = (_dot(sq_hi, head_sum) + _dot(sq_mid, head_sum)) * (1.0 / HEAD_DIM)
            o = o * lax.rsqrt(ms + NORM_EPS)
            outs.append(o * nw[:, sl] * _silu(g_all[:, sl]))
        out = jnp.concatenate(outs, axis=-1)
        if rows_in >= c:
            o_ref[0, pl.ds(r0, c), :] = out
        else:
            o_ref[0] = out[:rows_in]
        return carry

    lax.fori_loop(0, n_chunks, chunk, 0)

    @pl.when(pl.program_id(1) == pl.num_programs(1) - 1)
    def _():
        sout_ref[0] = st_ref[...]


def _pair_state(s):
    b, h, kd, vd = s.shape
    st = jnp.swapaxes(s, 2, 3).reshape(b, h // 2, 2, vd, kd)
    z = jnp.zeros_like(st[:, :, 0])
    top = jnp.concatenate([st[:, :, 0], z], axis=-1)
    bot = jnp.concatenate([z, st[:, :, 1]], axis=-1)
    return jnp.concatenate([top, bot], axis=-2)


def _unpair_state(sp):
    b, hp = sp.shape[:2]
    s0 = sp[:, :, :HEAD_DIM, :HEAD_DIM]
    s1 = sp[:, :, HEAD_DIM:, HEAD_DIM:]
    st = jnp.stack([s0, s1], axis=2).reshape(b, 2 * hp, HEAD_DIM, HEAD_DIM)
    return jnp.swapaxes(st, 2, 3)


def _hgrn(qr, fr, ir, gr, s0, lb_logits, norm_w, layer, batch, seq):
    d_rec = qr.shape[-1]
    n_pairs = d_rec // LANES
    tb = min(seq, ROW_TILE)
    assert seq % tb == 0 and (tb % REC_CHUNK == 0 or tb == seq < REC_CHUNK)
    n_chunks = max(tb // REC_CHUNK, 1)
    view = lambda a: a.reshape(batch, seq, d_rec)
    tok = pl.BlockSpec((1, tb, d_rec), lambda b, t: (b, t, 0))
    state = pl.BlockSpec((1, n_pairs, LANES, LANES), lambda b, t: (b, 0, 0, 0))
    o, s_new = pl.pallas_call(
        functools.partial(_hgrn_body, layer=layer, t_valid=min(seq, REC_CHUNK), n_chunks=n_chunks),
        grid=(batch, seq // tb),
        in_specs=[pl.BlockSpec(lb_logits.shape, lambda b, t: (0, 0)),
                  pl.BlockSpec((1, d_rec), lambda b, t: (0, 0)),
                  tok, tok, tok, tok, state],
        out_specs=[tok, state],
        out_shape=[jax.ShapeDtypeStruct((batch, seq, d_rec), F32),
                   jax.ShapeDtypeStruct((batch, n_pairs, LANES, LANES), F32)],
        scratch_shapes=[pltpu.VMEM((n_pairs, LANES, LANES), F32)],
        compiler_params=_params(("arbitrary", "arbitrary")),
        name="hgrn",
    )(lb_logits, norm_w.reshape(1, d_rec), view(qr), view(fr), view(ir), view(gr), _pair_state(s0))
    return o.reshape(batch * seq, d_rec), _unpair_state(s_new)


CONV_PAD = 8


def _conv_body(bc_ref, cc_ref, xc_ref, buf_ref, w_ref, b_ref, oc_ref, new_ref, up_ref):
    seq = cc_ref.shape[1]
    width = w_ref.shape[0]
    u = cc_ref[0] * xc_ref[0]
    up_ref[pl.ds(CONV_PAD, seq), :] = u
    up_ref[pl.ds(CONV_PAD - (width - 1), width - 1), :] = buf_ref[0]
    y = u * w_ref[width - 1:width, :] + b_ref[...]
    for i in range(width - 1):
        y = y + up_ref[pl.ds(CONV_PAD - (width - 1) + i, seq), :] * w_ref[i:i + 1, :]
    oc_ref[0] = bc_ref[0] * y
    new_ref[0] = up_ref[pl.ds(CONV_PAD + seq - (width - 1), width - 1), :]


def _short_conv(bc, cc, xc, buf, conv_w, conv_b, batch, seq):
    d_conv = bc.shape[-1]
    width = conv_w.shape[0]
    view = lambda a: a.reshape(batch, seq, d_conv)
    tok = pl.BlockSpec((1, seq, d_conv), lambda b: (b, 0, 0))
    tail = pl.BlockSpec((1, width - 1, d_conv), lambda b: (b, 0, 0))
    oc, new = pl.pallas_call(
        _conv_body,
        grid=(batch,),
        in_specs=[tok, tok, tok, tail,
                  pl.BlockSpec((width, d_conv), lambda b: (0, 0)),
                  pl.BlockSpec((1, d_conv), lambda b: (0, 0))],
        out_specs=[tok, tail],
        out_shape=[jax.ShapeDtypeStruct((batch, seq, d_conv), F32),
                   jax.ShapeDtypeStruct((batch, width - 1, d_conv), F32)],
        scratch_shapes=[pltpu.VMEM((CONV_PAD + seq, d_conv), F32)],
        compiler_params=_params(("arbitrary",)),
        name="short_conv",
    )(view(bc), view(cc), view(xc), buf, conv_w, conv_b.reshape(1, d_conv))
    return oc.reshape(batch * seq, d_conv), new


def _out_proj_body(*refs, bb, tt, n_branch):
    att = refs[:2 * n_branch] if n_branch > 1 else refs[:1]
    or_ref, oc_ref, x_ref, g_ref, sh_ref, sc_ref, w_ref, xo_ref, h_ref = refs[len(att):]
    if n_branch > 1:
        outs = [att[2 * i][...] for i in range(n_branch)]
        lses = [att[2 * i + 1][...] for i in range(n_branch)]
        m = functools.reduce(jnp.maximum, lses)
        ws = [jnp.exp(l - m) for l in lses]
        oa = sum(w * o for w, o in zip(ws, outs)) / sum(ws)
    else:
        oa = att[0][...]
    d_att, d_rec = oa.shape[-1], or_ref.shape[-1]
    mix = (_dot(oa.astype(BF16), w_ref[0:d_att, :])
           + _dot(or_ref[...].astype(BF16), w_ref[d_att:d_att + d_rec, :])
           + _dot(oc_ref[...].astype(BF16), w_ref[d_att + d_rec:, :]))
    d = x_ref.shape[-1]
    x = x_ref[...].reshape(bb, tt, d) + g_ref[...] * mix.reshape(bb, tt, d)
    xo_ref[...] = x.reshape(bb * tt, d)
    h = _rms(x) * (1.0 + sc_ref[...]) + sh_ref[...]
    h_ref[...] = h.reshape(bb * tt, d).astype(BF16)


def _out_proj(att, o_r, oc, x2, gate, shift, scale, w_out_bf, batch, seq):
    n, d = x2.shape
    bb, tt = _token_tile(batch, seq)
    tm = bb * tt
    n_t = seq // tt
    n_branch = len(att) // 2 if len(att) > 1 else 1
    tok = lambda a: pl.BlockSpec((tm, a.shape[-1]), lambda i: (i, 0))
    mod_spec = pl.BlockSpec((bb, 1, d), lambda i: (i // n_t, 0, 0))
    ins = list(att) + [o_r, oc, x2]
    return pl.pallas_call(
        functools.partial(_out_proj_body, bb=bb, tt=tt, n_branch=n_branch),
        grid=(n // tm,),
        in_specs=[tok(a) for a in ins] + [mod_spec] * 3 + [pl.BlockSpec(w_out_bf.shape, lambda i: (0, 0))],
        out_specs=[pl.BlockSpec((tm, d), lambda i: (i, 0))] * 2,
        out_shape=[jax.ShapeDtypeStruct((n, d), F32), jax.ShapeDtypeStruct((n, d), BF16)],
        compiler_params=_params(("arbitrary",)),
        name="out_proj",
    )(*ins, gate, shift, scale, w_out_bf)


def _ffn_body(h_ref, x_ref, g_ref, wg_ref, wu_ref, wo_ref, fw_ref, o_ref, acc_ref, *, bb, tt, final):
    j = pl.program_id(1)

    @pl.when(j == 0)
    def _():
        acc_ref[...] = jnp.zeros_like(acc_ref)

    h = h_ref[...]
    a = (_silu(_dot(h, wg_ref[...])) * _dot(h, wu_ref[...])).astype(BF16)
    acc_ref[...] += _dot(a, wo_ref[...])

    @pl.when(j == pl.num_programs(1) - 1)
    def _():
        d = x_ref.shape[-1]
        x = x_ref[...].reshape(bb, tt, d) + g_ref[...] * acc_ref[...].reshape(bb, tt, d)
        if final:
            x = _rms(x) * fw_ref[...]
        o_ref[...] = x.reshape(bb * tt, d)


def _ffn(h2, x2, gate, w_ffn_in_bf, w_ffn_out_bf, final_w, batch, seq, final):
    n, d = x2.shape
    d_ff = w_ffn_out_bf.shape[0]
    bb, tt = _token_tile(batch, seq)
    tm = bb * tt
    n_t = seq // tt
    n_f = 2
    tf = d_ff // n_f
    assert tf % LANES == 0
    return pl.pallas_call(
        functools.partial(_ffn_body, bb=bb, tt=tt, final=final),
        grid=(n // tm, n_f),
        in_specs=[pl.BlockSpec((tm, d), lambda i, j: (i, 0)),
                  pl.BlockSpec((tm, d), lambda i, j: (i, 0)),
                  pl.BlockSpec((bb, 1, d), lambda i, j: (i // n_t, 0, 0)),
                  pl.BlockSpec((d, tf), lambda i, j: (0, j)),
                  pl.BlockSpec((d, tf), lambda i, j: (0, j + n_f)),
                  pl.BlockSpec((tf, d), lambda i, j: (j, 0)),
                  pl.BlockSpec((1, d), lambda i, j: (0, 0))],
        out_specs=pl.BlockSpec((tm, d), lambda i, j: (i, 0)),
        out_shape=jax.ShapeDtypeStruct((n, d), F32),
        scratch_shapes=[pltpu.VMEM((tm, d), F32)],
        compiler_params=_params(("arbitrary", "arbitrary")),
        name="ffn",
    )(h2, x2, gate, w_ffn_in_bf, w_ffn_in_bf, w_ffn_out_bf, final_w.reshape(1, d))


def _trunk(x, mods, pos, cache_k, cache_v, state_hgrn, state_conv, weights):
    (w_in, conv_w, conv_b, lb_logits, hgrn_norm_w, w_out, w_ffn_in, w_ffn_out, final_norm_w) = weights
    batch, seq, d = x.shape
    depth = w_in.shape[0]
    d_conv = conv_w.shape[-1]
    d_att = d_rec = (w_in.shape[-1] - 3 * d_conv) // 7
    n_heads = d_att // HEAD_DIM
    widths = (d_att,) * 3 + (d_rec,) * 4 + (d_conv,) * 3
    bb, tt = _token_tile(batch, seq)
    cos_t, sin_t = _rope_tables(pos, n_heads)
    if bb > 1:
        cos_t, sin_t = jnp.tile(cos_t, (bb, 1)), jnp.tile(sin_t, (bb, 1))
    x2 = x.reshape(batch * seq, d)
    ks, vs, hs, cs = [], [], [], []
    for l in range(depth):
        sh1, sc1, g1, sh2, sc2, g2 = [m.reshape(batch, 1, d) for m in jnp.split(mods[l], 6, axis=-1)]
        qa, ka, va, qr, fr, ir, gr, bc, cc, xc = _in_proj(
            x2, sh1, sc1, w_in[l], cos_t, sin_t, batch, seq, widths)
        if cache_k is None:
            att = []
            for dil in DILATIONS:
                att += list(_band_attention(qa, ka, va, batch, seq, dil))
            s0 = jnp.zeros((batch, n_heads, HEAD_DIM, HEAD_DIM), F32)
            conv0 = jnp.zeros((batch, conv_w.shape[1] - 1, d_conv), F32)
        else:
            att = [_cache_attention(qa, ka, va, cache_k, cache_v, l, batch, seq)]
            s0, conv0 = state_hgrn[l], state_conv[l]
        o_r, s_new = _hgrn(qr, fr, ir, gr, s0, lb_logits, hgrn_norm_w[l], l, batch, seq)
        oc, conv_new = _short_conv(bc, cc, xc, conv0, conv_w[l], conv_b[l], batch, seq)
        x2, h2 = _out_proj(att, o_r, oc, x2, g1, sh2, sc2, w_out[l], batch, seq)
        x2 = _ffn(h2, x2, g2, w_ffn_in[l], w_ffn_out[l], final_norm_w, batch, seq, l == depth - 1)
        keep = min(WINDOW_STEPS * max(DILATIONS), seq)
        ks.append(ka.reshape(batch, seq, n_heads, HEAD_DIM)[:, seq - keep:])
        vs.append(va.reshape(batch, seq, n_heads, HEAD_DIM)[:, seq - keep:])
        hs.append(s_new)
        cs.append(conv_new)
    return x2.reshape(batch, seq, d), jnp.stack(ks), jnp.stack(vs), jnp.stack(hs), jnp.stack(cs)


def kernel(x_prompt, x_sample, cache_k, cache_v, state_hgrn, state_conv, c_prompt, c_sample,
           w_ada, b_ada, w_in, conv_w, conv_b, hgrn_lb_logits, hgrn_norm_w, w_out,
           w_ffn_in, w_ffn_out, final_norm_w):
    n_prompt = c_prompt.shape[0]
    past_len = PAST_LEN
    assert cache_k.shape[2] == min(WINDOW_STEPS * max(DILATIONS), PAST_LEN) == PAST_LEN
    mods = _ada(jnp.concatenate([c_prompt, c_sample], axis=0), w_ada, b_ada)
    weights = (w_in.astype(BF16), conv_w, conv_b, hgrn_lb_logits.astype(F32), hgrn_norm_w,
               w_out.astype(BF16), w_ffn_in.astype(BF16), w_ffn_out.astype(BF16), final_norm_w)
    pos_p = jnp.arange(x_prompt.shape[1], dtype=jnp.int32)
    pos_s = past_len + jnp.arange(x_sample.shape[1], dtype=jnp.int32)
    out_p = _trunk(x_prompt, mods[:, :n_prompt], pos_p, None, None, None, None, weights)
    out_s = _trunk(x_sample, mods[:, n_prompt:], pos_s, cache_k, cache_v, state_hgrn, state_conv, weights)
    return (out_p[0], out_s[0]) + out_p[1:] + out_s[1:]
```

```python
import functools
import math

import jax
import jax.numpy as jnp
from jax import lax
from jax.experimental import pallas as pl
from jax.experimental.pallas import tpu as pltpu

F32 = jnp.float32
BF16 = jnp.bfloat16

HEAD_DIM = 64
LANES = 128
DILATIONS = (1, 4, 16)
WINDOW_STEPS = 128
PAST_LEN = 2048
ROPE_THETA = 10000.0
NORM_EPS = 1e-6
REC_CHUNK = 16
ROW_TILE = 512
VMEM_LIMIT = 48 * 1024 * 1024


def _params(sem):
    return pltpu.CompilerParams(dimension_semantics=sem, vmem_limit_bytes=VMEM_LIMIT)


def _dot(a, b):
    return jnp.dot(a, b, preferred_element_type=F32)


def _dot_nt(a, b):
    return lax.dot_general(a, b, (((1,), (1,)), ((), ())), preferred_element_type=F32)


def _dot_tn(a, b):
    return lax.dot_general(a, b, (((0,), (0,)), ((), ())), preferred_element_type=F32)


def _split3(x):
    hi = x.astype(BF16)
    r = x - hi.astype(F32)
    mid = r.astype(BF16)
    lo = (r - mid.astype(F32)).astype(BF16)
    return hi, mid, lo


def _silu(x):
    return x * jax.nn.sigmoid(x)


def _rms(x):
    return x * lax.rsqrt(jnp.mean(x * x, axis=-1, keepdims=True) + NORM_EPS)


def _token_tile(batch, seq):
    if seq >= ROW_TILE:
        assert seq % ROW_TILE == 0
        return 1, ROW_TILE
    assert ROW_TILE % seq == 0 and seq % 8 == 0
    bb = min(batch, ROW_TILE // seq)
    assert batch % bb == 0
    return bb, seq


def _ada_body(c_ref, w_ref, b_ref, o_ref):
    s = _silu(c_ref[...])
    s_hi, s_mid, _ = _split3(s)
    w_hi, w_mid, _ = _split3(w_ref[0])
    o_ref[0] = _dot(s_hi, w_hi) + _dot(s_hi, w_mid) + _dot(s_mid, w_hi) + b_ref[0]


def _ada(c_all, w_ada, b_ada):
    depth, d, n6 = w_ada.shape
    bc = c_all.shape[0]
    tn = 512
    return pl.pallas_call(
        _ada_body,
        grid=(depth, n6 // tn),
        in_specs=[pl.BlockSpec((bc, d), lambda l, j: (0, 0)),
                  pl.BlockSpec((1, d, tn), lambda l, j: (l, 0, j)),
                  pl.BlockSpec((1, 1, tn), lambda l, j: (l, 0, j))],
        out_specs=pl.BlockSpec((1, bc, tn), lambda l, j: (l, 0, j)),
        out_shape=jax.ShapeDtypeStruct((depth, bc, n6), F32),
        compiler_params=_params(("arbitrary", "arbitrary")),
        name="ada",
    )(c_all, w_ada, b_ada.reshape(depth, 1, n6))


def _rope_tables(pos, n_heads):
    half = HEAD_DIM // 2
    freqs = ROPE_THETA ** (-jnp.arange(half, dtype=F32) / half)
    ang = pos.astype(F32)[:, None] * freqs[None, :]
    cos, sin = jnp.cos(ang), jnp.sin(ang)
    cos_t = jnp.tile(jnp.concatenate([cos, cos], axis=-1), (1, n_heads))
    sin_t = jnp.tile(jnp.concatenate([-sin, sin], axis=-1), (1, n_heads))
    return cos_t, sin_t


def _rope(x, cos, sin_signed):
    outs = []
    for c in range(x.shape[-1] // LANES):
        sl = slice(c * LANES, (c + 1) * LANES)
        xc = x[:, sl]
        lane = lax.broadcasted_iota(jnp.int32, xc.shape, 1)
        ahead = pltpu.roll(xc, LANES - HEAD_DIM // 2, axis=1)
        behind = pltpu.roll(xc, HEAD_DIM // 2, axis=1)
        rot = jnp.where((lane & (HEAD_DIM // 2)) == 0, ahead, behind)
        outs.append(xc * cos[:, sl] + rot * sin_signed[:, sl])
    return jnp.concatenate(outs, axis=-1)


def _in_proj_body(x_ref, sh_ref, sc_ref, w_ref, cos_ref, sin_ref, *out_refs, bb, tt, widths, n_paired):
    d = x_ref.shape[-1]
    x = x_ref[...].reshape(bb, tt, d)
    h = _rms(x) * (1.0 + sc_ref[...]) + sh_ref[...]
    hb = h.reshape(bb * tt, d).astype(BF16)
    off = 0
    for idx, (o_ref, wd) in enumerate(zip(out_refs, widths)):
        p = _dot(hb, w_ref[:, off:off + wd])
        if idx < 2:
            p = _rope(p, cos_ref[...], sin_ref[...])
        if idx < n_paired:
            for c in range(wd // LANES):
                o_ref[c] = p[:, c * LANES:(c + 1) * LANES]
        else:
            o_ref[...] = p
        off += wd


def _in_proj(x2, shift, scale, w_in_bf, cos_t, sin_t, batch, seq, widths, n_paired):
    n, d = x2.shape
    bb, tt = _token_tile(batch, seq)
    tm = bb * tt
    n_t = seq // tt
    table_blocks = cos_t.shape[0] // tm
    body = functools.partial(_in_proj_body, bb=bb, tt=tt, widths=widths, n_paired=n_paired)
    mod_spec = pl.BlockSpec((bb, 1, d), lambda i: (i // n_t, 0, 0))
    tab_spec = pl.BlockSpec((tm, cos_t.shape[1]), lambda i: (i % table_blocks, 0))
    out_specs, out_shape = [], []
    for idx, wd in enumerate(widths):
        if idx < n_paired:
            out_specs.append(pl.BlockSpec((wd // LANES, tm, LANES), lambda i: (0, i, 0)))
            out_shape.append(jax.ShapeDtypeStruct((wd // LANES, n, LANES), F32))
        else:
            out_specs.append(pl.BlockSpec((tm, wd), lambda i: (i, 0)))
            out_shape.append(jax.ShapeDtypeStruct((n, wd), F32))
    return pl.pallas_call(
        body,
        grid=(n // tm,),
        in_specs=[pl.BlockSpec((tm, d), lambda i: (i, 0)), mod_spec, mod_spec,
                  pl.BlockSpec(w_in_bf.shape, lambda i: (0, 0)), tab_spec, tab_spec],
        out_specs=out_specs,
        out_shape=out_shape,
        compiler_params=_params(("arbitrary",)),
        name="in_proj",
    )(x2, shift, scale, w_in_bf, cos_t, sin_t)


def _prompt_attn_body(q_ref, k_ref, v_ref, o_ref, m_ref, l_ref, acc_ref):
    n_pairs, seq, _ = q_ref.shape
    blk = WINDOW_STEPS
    lane = lax.broadcasted_iota(jnp.int32, (blk, LANES), 1)
    first = lane < HEAD_DIM
    scale = HEAD_DIM ** -0.5

    def rows(start, dil):
        return pl.ds(start, blk) if dil == 1 else pl.ds(start, blk, stride=dil)

    for b_idx, dil in enumerate(DILATIONS):
        n_blk = seq // dil // blk
        has_prev = n_blk > 1
        n_keys = 2 * blk if has_prev else blk
        q_row = lax.broadcasted_iota(jnp.int32, (2 * blk, n_keys), 0) & (blk - 1)
        k_col = lax.broadcasted_iota(jnp.int32, (2 * blk, n_keys), 1)
        is_first, is_last = b_idx == 0, b_idx == len(DILATIONS) - 1

        def unit(u, carry, dil=dil, has_prev=has_prev, q_row=q_row, k_col=k_col,
                 is_first=is_first, is_last=is_last):
            res = u & (dil - 1)
            i = u >> (dil.bit_length() - 1)
            start = res + dil * blk * i
            if dil == 1:
                start = pl.multiple_of(start, blk)
            if has_prev:
                prev_start = jnp.maximum(start - dil * blk, res)
                lo = jnp.maximum(q_row, jnp.where(i > 0, 0, blk))
                valid = (k_col >= lo) & (k_col <= q_row + blk)
            else:
                valid = k_col <= q_row
            for p in range(n_pairs):
                q = q_ref[p, rows(start, dil), :] * scale
                q2 = jnp.concatenate([jnp.where(first, q, 0.0), jnp.where(first, 0.0, q)], axis=0)
                kc = k_ref[p, rows(start, dil), :]
                vc = v_ref[p, rows(start, dil), :]
                if has_prev:
                    kc = jnp.concatenate([k_ref[p, rows(prev_start, dil), :], kc], axis=0)
                    vc = jnp.concatenate([v_ref[p, rows(prev_start, dil), :], vc], axis=0)
                s = jnp.where(valid, _dot_nt(q2.astype(BF16), kc.astype(BF16)), -jnp.inf)
                m2 = jnp.max(s, axis=-1, keepdims=True)
                e = jnp.exp(s - m2)
                l2 = jnp.sum(e, axis=-1, keepdims=True)
                pv = _dot(e.astype(BF16), vc.astype(BF16))
                m_b = jnp.where(first, m2[:blk], m2[blk:])
                l_b = jnp.where(first, l2[:blk], l2[blk:])
                acc_b = jnp.where(first, pv[:blk], pv[blk:])
                if not is_first:
                    m_o = m_ref[p, rows(start, dil), :]
                    m_n = jnp.maximum(m_o, m_b)
                    w_o, w_b = jnp.exp(m_o - m_n), jnp.exp(m_b - m_n)
                    l_b = w_o * l_ref[p, rows(start, dil), :] + w_b * l_b
                    acc_b = w_o * acc_ref[p, rows(start, dil), :] + w_b * acc_b
                    m_b = m_n
                if is_last:
                    o_ref[p, rows(start, dil), :] = acc_b / l_b
                else:
                    m_ref[p, rows(start, dil), :] = m_b
                    l_ref[p, rows(start, dil), :] = l_b
                    acc_ref[p, rows(start, dil), :] = acc_b
            return carry

        lax.fori_loop(0, dil * n_blk, unit, 0)


def _prompt_attention(q3, k3, v3, batch, seq):
    n_pairs = q3.shape[0]
    assert seq % (WINDOW_STEPS * max(DILATIONS)) == 0
    spec = pl.BlockSpec((n_pairs, seq, LANES), lambda b: (0, b, 0))
    return pl.pallas_call(
        _prompt_attn_body,
        grid=(batch,),
        in_specs=[spec, spec, spec],
        out_specs=spec,
        out_shape=jax.ShapeDtypeStruct(q3.shape, F32),
        scratch_shapes=[pltpu.VMEM((n_pairs, seq, LANES), F32)] * 3,
        compiler_params=_params(("arbitrary",)),
        name="prompt_attn",
    )(q3, k3, v3)


def _branch_count(delta):
    cnt = jnp.zeros(delta.shape, F32)
    for dil in DILATIONS:
        hit = (delta >= 0) & (delta <= WINDOW_STEPS * dil) & ((delta & (dil - 1)) == 0)
        cnt = cnt + jnp.where(hit, 1.0, 0.0)
    return cnt


def _cache_attn_body(q_ref, kn_ref, vn_ref, ks_ref, vs_ref, kd_ref, vd_ref, o_ref, *, past):
    n_pairs, t_new, _ = q_ref.shape
    n_grp, per_grp = ks_ref.shape[2], ks_ref.shape[3]
    period = max(DILATIONS)
    n_dense = kd_ref.shape[3]
    n_strided = n_grp * per_grp
    t_s = lax.broadcasted_iota(jnp.int32, (t_new, n_strided), 0)
    c_s = lax.broadcasted_iota(jnp.int32, (t_new, n_strided), 1)
    j_s = (c_s >> (per_grp.bit_length() - 1)) * period + (c_s & (per_grp - 1))
    cnt_s = _branch_count(past + t_s - j_s)
    t_d = lax.broadcasted_iota(jnp.int32, (t_new, n_dense), 0)
    j_d = lax.broadcasted_iota(jnp.int32, (t_new, n_dense), 1) + (past - n_dense)
    cnt_d = _branch_count(past + t_d - j_d)
    t_n = lax.broadcasted_iota(jnp.int32, (t_new, t_new), 0)
    j_n = lax.broadcasted_iota(jnp.int32, (t_new, t_new), 1)
    cnt_n = _branch_count(t_n - j_n)
    scale = HEAD_DIM ** -0.5
    for p in range(n_pairs):
        q_pair, kn_pair, vn_pair = q_ref[p] * scale, kn_ref[p], vn_ref[p]
        outs = []
        for hh in range(LANES // HEAD_DIM):
            h = p * (LANES // HEAD_DIM) + hh
            sl = slice(hh * HEAD_DIM, (hh + 1) * HEAD_DIM)
            q = q_pair[:, sl]
            k_s = ks_ref[0, 0, :, :, h, :].reshape(n_strided, HEAD_DIM).astype(BF16)
            v_s = vs_ref[0, 0, :, :, h, :].reshape(n_strided, HEAD_DIM).astype(BF16)
            k_d = kd_ref[0, 0, 0, :, h, :].astype(BF16)
            v_d = vd_ref[0, 0, 0, :, h, :].astype(BF16)
            qb = q.astype(BF16)
            s_s = jnp.where(cnt_s > 0, _dot_nt(qb, k_s), -jnp.inf)
            s_d = jnp.where(cnt_d > 0, _dot_nt(qb, k_d), -jnp.inf)
            s_n = jnp.where(cnt_n > 0, _dot_nt(q, kn_pair[:, sl]), -jnp.inf)
            m = jnp.maximum(jnp.maximum(jnp.max(s_s, axis=-1, keepdims=True),
                                        jnp.max(s_d, axis=-1, keepdims=True)),
                            jnp.max(s_n, axis=-1, keepdims=True))
            p_s = cnt_s * jnp.exp(s_s - m)
            p_d = cnt_d * jnp.exp(s_d - m)
            p_n = cnt_n * jnp.exp(s_n - m)
            den = (jnp.sum(p_s, axis=-1, keepdims=True) + jnp.sum(p_d, axis=-1, keepdims=True)
                   + jnp.sum(p_n, axis=-1, keepdims=True))
            acc = (_dot(p_s.astype(BF16), v_s) + _dot(p_d.astype(BF16), v_d) + _dot(p_n, vn_pair[:, sl]))
            outs.append(acc / den)
        o_ref[p] = jnp.concatenate(outs, axis=-1)


def _cache_attention(q3, kn3, vn3, cache_k, cache_v, layer, batch, seq):
    depth, _, past, n_heads, hd = cache_k.shape
    n_pairs = q3.shape[0]
    period = max(DILATIONS)
    dense = WINDOW_STEPS * sorted(DILATIONS)[-2]
    assert seq <= period and seq & (seq - 1) == 0 and past % dense == 0 and past > dense
    n_grp = (past - dense) // period
    strided = lambda a: a.reshape(depth, batch, past // period, period, n_heads, hd)
    tail = lambda a: a.reshape(depth, batch, past // dense, dense, n_heads, hd)
    new = pl.BlockSpec((n_pairs, seq, LANES), lambda b: (0, b, 0))
    s_spec = pl.BlockSpec((1, 1, n_grp, seq, n_heads, hd), lambda b: (layer, b, 0, 0, 0, 0))
    d_spec = pl.BlockSpec((1, 1, 1, dense, n_heads, hd), lambda b: (layer, b, past // dense - 1, 0, 0, 0))
    return pl.pallas_call(
        functools.partial(_cache_attn_body, past=past),
        grid=(batch,),
        in_specs=[new, new, new, s_spec, s_spec, d_spec, d_spec],
        out_specs=new,
        out_shape=jax.ShapeDtypeStruct(q3.shape, F32),
        compiler_params=_params(("arbitrary",)),
        name="cache_attn",
    )(q3, kn3, vn3, strided(cache_k), strided(cache_v), tail(cache_k), tail(cache_v))


def _lower_bound(logits, layer):
    e = jnp.exp(logits - jnp.max(logits, axis=0, keepdims=True))
    sm = e / jnp.sum(e, axis=0, keepdims=True)
    lb = jnp.zeros_like(sm[0:1])
    for j in range(1, layer + 1):
        lb = lb + sm[j:j + 1]
    return lb


def _hgrn_body(lbl_ref, nw_ref, q_ref, f_ref, i_ref, g_ref, s0_ref, o_ref, sout_ref, st_ref,
               *, layer, t_valid, n_chunks):
    c = REC_CHUNK
    d_rec = q_ref.shape[-1]
    n_pairs = d_rec // LANES
    rows_in = q_ref.shape[1]

    @pl.when(pl.program_id(1) == 0)
    def _():
        st_ref[...] = s0_ref[0]

    lb = _lower_bound(lbl_ref[...], layer)
    log_lb = jnp.log(lb)
    log_1m_lb = jnp.log1p(-lb)
    nw = nw_ref[...]

    r_i = lax.broadcasted_iota(jnp.int32, (c, c), 0)
    c_i = lax.broadcasted_iota(jnp.int32, (c, c), 1)
    tri = jnp.where(c_i <= r_i, 1.0, 0.0).astype(BF16)
    l_r = lax.broadcasted_iota(jnp.int32, (LANES, LANES), 0) >> (HEAD_DIM.bit_length() - 1)
    l_c = lax.broadcasted_iota(jnp.int32, (LANES, LANES), 1) >> (HEAD_DIM.bit_length() - 1)
    same_head = l_r == l_c
    head_sum = jnp.where(same_head, 1.0, 0.0).astype(BF16)
    s_r = lax.broadcasted_iota(jnp.int32, (c, c * c), 0)
    s_c = lax.broadcasted_iota(jnp.int32, (c, c * c), 1) >> (c.bit_length() - 1)
    pick_t = jnp.where(s_r == s_c, 1.0, 0.0).astype(BF16)
    row = lax.broadcasted_iota(jnp.int32, (c, LANES), 0)

    def load(ref, r0):
        if rows_in >= c:
            return ref[0, pl.ds(r0, c), :]
        pad = jnp.zeros((c - rows_in, d_rec), F32)
        return jnp.concatenate([ref[0], pad], axis=0)

    def chunk(ci, carry):
        r0 = pl.multiple_of(ci * c, c)
        z = load(f_ref, r0)
        log_sig = jnp.minimum(z, 0.0) - jnp.log1p(jnp.exp(-jnp.abs(z)))
        b_term = log_1m_lb + log_sig
        log_f = jnp.maximum(log_lb, b_term) + jnp.log1p(jnp.exp(-jnp.abs(log_lb - b_term)))
        kk = (1.0 - lb) * jax.nn.sigmoid(-z)
        if t_valid < c:
            live = lax.broadcasted_iota(jnp.int32, (c, d_rec), 0) < t_valid
            log_f = jnp.where(live, log_f, 0.0)
            kk = jnp.where(live, kk, 0.0)
        f_hi, f_mid, f_lo = _split3(log_f)
        b_all = _dot(tri, f_hi) + _dot(tri, f_mid) + _dot(tri, f_lo)
        q_all = load(q_ref, r0)
        v_all = load(i_ref, r0)
        g_all = load(g_ref, r0)
        outs = []
        for p in range(n_pairs):
            sl = slice(p * LANES, (p + 1) * LANES)
            b, q, k, v = b_all[:, sl], q_all[:, sl], kk[:, sl], v_all[:, sl]
            st = st_ref[p]
            xs = []
            for t in range(c):
                e = jnp.exp(jnp.minimum(b[t:t + 1, :] - b, 0.0))
                xs.append(jnp.where(row <= t, (q[t:t + 1, :] * e) * k, 0.0))
            x = jnp.concatenate(xs, axis=0).astype(BF16)
            a_rep = _dot(x, head_sum)
            y = (a_rep * jnp.concatenate([v] * c, axis=0)).astype(BF16)
            o = _dot(pick_t, y)
            o = o + _dot_nt((q * jnp.exp(b)).astype(BF16), st.astype(BF16))
            b_last = b[c - 1:c, :]
            k_hat = (k * jnp.exp(b_last - b)).astype(BF16)
            upd = _dot_tn(v.astype(BF16), k_hat)
            st_ref[p] = st * jnp.exp(b_last) + jnp.where(same_head, upd, 0.0)
            sq_hi, sq_mid, _ = _split3(o * o)
            ms = (_dot(sq_hi, head_sum) + _dot(sq_mid, head_sum)) * (1.0 / HEAD_DIM)
            o = o * lax.rsqrt(ms + NORM_EPS)
            outs.append(o * nw[:, sl] * _silu(g_all[:, sl]))
        out = jnp.concatenate(outs, axis=-1)
        if rows_in >= c:
            o_ref[0, pl.ds(r0, c), :] = out
        else:
            o_ref[0] = out[:rows_in]
        return carry

    lax.fori_loop(0, n_chunks, chunk, 0)

    @pl.when(pl.program_id(1) == pl.num_programs(1) - 1)
    def _():
        sout_ref[0] = st_ref[...]


def _pair_state(s):
    b, h, kd, vd = s.shape
    st = jnp.swapaxes(s, 2, 3).reshape(b, h // 2, 2, vd, kd)
    z = jnp.zeros_like(st[:, :, 0])
    top = jnp.concatenate([st[:, :, 0], z], axis=-1)
    bot = jnp.concatenate([z, st[:, :, 1]], axis=-1)
    return jnp.concatenate([top, bot], axis=-2)


def _unpair_state(sp):
    b, hp = sp.shape[:2]
    s0 = sp[:, :, :HEAD_DIM, :HEAD_DIM]
    s1 = sp[:, :, HEAD_DIM:, HEAD_DIM:]
    st = jnp.stack([s0, s1], axis=2).reshape(b, 2 * hp, HEAD_DIM, HEAD_DIM)
    return jnp.swapaxes(st, 2, 3)


def _hgrn(qr, fr, ir, gr, s0, lb_logits, norm_w, layer, batch, seq):
    d_rec = qr.shape[-1]
    n_pairs = d_rec // LANES
    tb = min(seq, ROW_TILE)
    assert seq % tb == 0 and (tb % REC_CHUNK == 0 or tb == seq < REC_CHUNK)
    n_chunks = max(tb // REC_CHUNK, 1)
    view = lambda a: a.reshape(batch, seq, d_rec)
    tok = pl.BlockSpec((1, tb, d_rec), lambda b, t: (b, t, 0))
    state = pl.BlockSpec((1, n_pairs, LANES, LANES), lambda b, t: (b, 0, 0, 0))
    o, s_new = pl.pallas_call(
        functools.partial(_hgrn_body, layer=layer, t_valid=min(seq, REC_CHUNK), n_chunks=n_chunks),
        grid=(batch, seq // tb),
        in_specs=[pl.BlockSpec(lb_logits.shape, lambda b, t: (0, 0)),
                  pl.BlockSpec((1, d_rec), lambda b, t: (0, 0)),
                  tok, tok, tok, tok, state],
        out_specs=[tok, state],
        out_shape=[jax.ShapeDtypeStruct((batch, seq, d_rec), F32),
                   jax.ShapeDtypeStruct((batch, n_pairs, LANES, LANES), F32)],
        scratch_shapes=[pltpu.VMEM((n_pairs, LANES, LANES), F32)],
        compiler_params=_params(("arbitrary", "arbitrary")),
        name="hgrn",
    )(lb_logits, norm_w.reshape(1, d_rec), view(qr), view(fr), view(ir), view(gr), _pair_state(s0))
    return o.reshape(batch * seq, d_rec), _unpair_state(s_new)


CONV_PAD = 8


def _conv_body(bc_ref, cc_ref, xc_ref, buf_ref, w_ref, b_ref, oc_ref, new_ref, up_ref):
    seq = cc_ref.shape[1]
    width = w_ref.shape[0]
    u = cc_ref[0] * xc_ref[0]
    up_ref[pl.ds(CONV_PAD, seq), :] = u
    up_ref[pl.ds(CONV_PAD - (width - 1), width - 1), :] = buf_ref[0]
    y = u * w_ref[width - 1:width, :] + b_ref[...]
    for i in range(width - 1):
        y = y + up_ref[pl.ds(CONV_PAD - (width - 1) + i, seq), :] * w_ref[i:i + 1, :]
    oc_ref[0] = bc_ref[0] * y
    new_ref[0] = up_ref[pl.ds(CONV_PAD + seq - (width - 1), width - 1), :]


def _short_conv(bc, cc, xc, buf, conv_w, conv_b, batch, seq):
    d_conv = bc.shape[-1]
    width = conv_w.shape[0]
    view = lambda a: a.reshape(batch, seq, d_conv)
    tok = pl.BlockSpec((1, seq, d_conv), lambda b: (b, 0, 0))
    tail = pl.BlockSpec((1, width - 1, d_conv), lambda b: (b, 0, 0))
    oc, new = pl.pallas_call(
        _conv_body,
        grid=(batch,),
        in_specs=[tok, tok, tok, tail,
                  pl.BlockSpec((width, d_conv), lambda b: (0, 0)),
                  pl.BlockSpec((1, d_conv), lambda b: (0, 0))],
        out_specs=[tok, tail],
        out_shape=[jax.ShapeDtypeStruct((batch, seq, d_conv), F32),
                   jax.ShapeDtypeStruct((batch, width - 1, d_conv), F32)],
        scratch_shapes=[pltpu.VMEM((CONV_PAD + seq, d_conv), F32)],
        compiler_params=_params(("arbitrary",)),
        name="short_conv",
    )(view(bc), view(cc), view(xc), buf, conv_w, conv_b.reshape(1, d_conv))
    return oc.reshape(batch * seq, d_conv), new


def _out_proj_body(oa_ref, or_ref, oc_ref, x_ref, g_ref, sh_ref, sc_ref, w_ref, xo_ref, h_ref, *, bb, tt):
    n_pairs = oa_ref.shape[0]
    d_att, d_rec = n_pairs * LANES, or_ref.shape[-1]
    mix = (_dot(or_ref[...].astype(BF16), w_ref[d_att:d_att + d_rec, :])
           + _dot(oc_ref[...].astype(BF16), w_ref[d_att + d_rec:, :]))
    for p in range(n_pairs):
        mix = mix + _dot(oa_ref[p].astype(BF16), w_ref[p * LANES:(p + 1) * LANES, :])
    d = x_ref.shape[-1]
    x = x_ref[...].reshape(bb, tt, d) + g_ref[...] * mix.reshape(bb, tt, d)
    xo_ref[...] = x.reshape(bb * tt, d)
    h = _rms(x) * (1.0 + sc_ref[...]) + sh_ref[...]
    h_ref[...] = h.reshape(bb * tt, d).astype(BF16)


def _out_proj(oa3, o_r, oc, x2, gate, shift, scale, w_out_bf, batch, seq):
    n, d = x2.shape
    bb, tt = _token_tile(batch, seq)
    tm = bb * tt
    n_t = seq // tt
    tok = lambda a: pl.BlockSpec((tm, a.shape[-1]), lambda i: (i, 0))
    mod_spec = pl.BlockSpec((bb, 1, d), lambda i: (i // n_t, 0, 0))
    return pl.pallas_call(
        functools.partial(_out_proj_body, bb=bb, tt=tt),
        grid=(n // tm,),
        in_specs=[pl.BlockSpec((oa3.shape[0], tm, LANES), lambda i: (0, i, 0)), tok(o_r), tok(oc), tok(x2)]
        + [mod_spec] * 3 + [pl.BlockSpec(w_out_bf.shape, lambda i: (0, 0))],
        out_specs=[pl.BlockSpec((tm, d), lambda i: (i, 0))] * 2,
        out_shape=[jax.ShapeDtypeStruct((n, d), F32), jax.ShapeDtypeStruct((n, d), BF16)],
        compiler_params=_params(("arbitrary",)),
        name="out_proj",
    )(oa3, o_r, oc, x2, gate, shift, scale, w_out_bf)


def _ffn_body(h_ref, x_ref, g_ref, wg_ref, wu_ref, wo_ref, fw_ref, o_ref, acc_ref, *, bb, tt, final):
    j = pl.program_id(1)

    @pl.when(j == 0)
    def _():
        acc_ref[...] = jnp.zeros_like(acc_ref)

    h = h_ref[...]
    a = (_silu(_dot(h, wg_ref[...])) * _dot(h, wu_ref[...])).astype(BF16)
    acc_ref[...] += _dot(a, wo_ref[...])

    @pl.when(j == pl.num_programs(1) - 1)
    def _():
        d = x_ref.shape[-1]
        x = x_ref[...].reshape(bb, tt, d) + g_ref[...] * acc_ref[...].reshape(bb, tt, d)
        if final:
            x = _rms(x) * fw_ref[...]
        o_ref[...] = x.reshape(bb * tt, d)


def _ffn(h2, x2, gate, w_ffn_in_bf, w_ffn_out_bf, final_w, batch, seq, final):
    n, d = x2.shape
    d_ff = w_ffn_out_bf.shape[0]
    bb, tt = _token_tile(batch, seq)
    tm = bb * tt
    n_t = seq // tt
    n_f = 2
    tf = d_ff // n_f
    assert tf % LANES == 0
    return pl.pallas_call(
        functools.partial(_ffn_body, bb=bb, tt=tt, final=final),
        grid=(n // tm, n_f),
        in_specs=[pl.BlockSpec((tm, d), lambda i, j: (i, 0)),
                  pl.BlockSpec((tm, d), lambda i, j: (i, 0)),
                  pl.BlockSpec((bb, 1, d), lambda i, j: (i // n_t, 0, 0)),
                  pl.BlockSpec((d, tf), lambda i, j: (0, j)),
                  pl.BlockSpec((d, tf), lambda i, j: (0, j + n_f)),
                  pl.BlockSpec((tf, d), lambda i, j: (j, 0)),
                  pl.BlockSpec((1, d), lambda i, j: (0, 0))],
        out_specs=pl.BlockSpec((tm, d), lambda i, j: (i, 0)),
        out_shape=jax.ShapeDtypeStruct((n, d), F32),
        scratch_shapes=[pltpu.VMEM((tm, d), F32)],
        compiler_params=_params(("arbitrary", "arbitrary")),
        name="ffn",
    )(h2, x2, gate, w_ffn_in_bf, w_ffn_in_bf, w_ffn_out_bf, final_w.reshape(1, d))


def _trunk(x, mods, pos, cache_k, cache_v, state_hgrn, state_conv, weights):
    (w_in, conv_w, conv_b, lb_logits, hgrn_norm_w, w_out, w_ffn_in, w_ffn_out, final_norm_w) = weights
    batch, seq, d = x.shape
    depth = w_in.shape[0]
    d_conv = conv_w.shape[-1]
    d_att = d_rec = (w_in.shape[-1] - 3 * d_conv) // 7
    n_heads = d_att // HEAD_DIM
    widths = (d_att,) * 3 + (d_rec,) * 4 + (d_conv,) * 3
    bb, tt = _token_tile(batch, seq)
    cos_t, sin_t = _rope_tables(pos, n_heads)
    if bb > 1:
        cos_t, sin_t = jnp.tile(cos_t, (bb, 1)), jnp.tile(sin_t, (bb, 1))
    x2 = x.reshape(batch * seq, d)
    ks, vs, hs, cs = [], [], [], []
    for l in range(depth):
        sh1, sc1, g1, sh2, sc2, g2 = [m.reshape(batch, 1, d) for m in jnp.split(mods[l], 6, axis=-1)]
        qa, ka, va, qr, fr, ir, gr, bc, cc, xc = _in_proj(
            x2, sh1, sc1, w_in[l], cos_t, sin_t, batch, seq, widths, 3)
        if cache_k is None:
            oa = _prompt_attention(qa, ka, va, batch, seq)
            s0 = jnp.zeros((batch, n_heads, HEAD_DIM, HEAD_DIM), F32)
            conv0 = jnp.zeros((batch, conv_w.shape[1] - 1, d_conv), F32)
        else:
            oa = _cache_attention(qa, ka, va, cache_k, cache_v, l, batch, seq)
            s0, conv0 = state_hgrn[l], state_conv[l]
        o_r, s_new = _hgrn(qr, fr, ir, gr, s0, lb_logits, hgrn_norm_w[l], l, batch, seq)
        oc, conv_new = _short_conv(bc, cc, xc, conv0, conv_w[l], conv_b[l], batch, seq)
        x2, h2 = _out_proj(oa, o_r, oc, x2, g1, sh2, sc2, w_out[l], batch, seq)
        x2 = _ffn(h2, x2, g2, w_ffn_in[l], w_ffn_out[l], final_norm_w, batch, seq, l == depth - 1)
        keep = min(WINDOW_STEPS * max(DILATIONS), seq)
        unpair = lambda a: jnp.swapaxes(a, 0, 1).reshape(batch, seq, n_heads, HEAD_DIM)[:, seq - keep:]
        ks.append(unpair(ka))
        vs.append(unpair(va))
        hs.append(s_new)
        cs.append(conv_new)
    return x2.reshape(batch, seq, d), jnp.stack(ks), jnp.stack(vs), jnp.stack(hs), jnp.stack(cs)


def kernel(x_prompt, x_sample, cache_k, cache_v, state_hgrn, state_conv, c_prompt, c_sample,
           w_ada, b_ada, w_in, conv_w, conv_b, hgrn_lb_logits, hgrn_norm_w, w_out,
           w_ffn_in, w_ffn_out, final_norm_w):
    n_prompt = c_prompt.shape[0]
    past_len = PAST_LEN
    assert cache_k.shape[2] == min(WINDOW_STEPS * max(DILATIONS), PAST_LEN) == PAST_LEN
    mods = _ada(jnp.concatenate([c_prompt, c_sample], axis=0), w_ada, b_ada)
    weights = (w_in.astype(BF16), conv_w, conv_b, hgrn_lb_logits.astype(F32), hgrn_norm_w,
               w_out.astype(BF16), w_ffn_in.astype(BF16), w_ffn_out.astype(BF16), final_norm_w)
    pos_p = jnp.arange(x_prompt.shape[1], dtype=jnp.int32)
    pos_s = past_len + jnp.arange(x_sample.shape[1], dtype=jnp.int32)
    out_p = _trunk(x_prompt, mods[:, :n_prompt], pos_p, None, None, None, None, weights)
    out_s = _trunk(x_sample, mods[:, n_prompt:], pos_s, cache_k, cache_v, state_hgrn, state_conv, weights)
    return (out_p[0], out_s[0]) + out_p[1:] + out_s[1:]
```

```python
import functools
import math

import jax
import jax.numpy as jnp
from jax import lax
from jax.experimental import pallas as pl
from jax.experimental.pallas import tpu as pltpu

F32 = jnp.float32
BF16 = jnp.bfloat16

HEAD_DIM = 64
LANES = 128
DILATIONS = (1, 4, 16)
WINDOW_STEPS = 128
PAST_LEN = 2048
ROPE_THETA = 10000.0
NORM_EPS = 1e-6
REC_CHUNK = 16
LOG2_E = 1.4426950408889634
ROW_TILE = 512
VMEM_LIMIT = 48 * 1024 * 1024


def _params(sem):
    return pltpu.CompilerParams(dimension_semantics=sem, vmem_limit_bytes=VMEM_LIMIT)


def _dot(a, b):
    return jnp.dot(a, b, preferred_element_type=F32)


def _dot_nt(a, b):
    return lax.dot_general(a, b, (((1,), (1,)), ((), ())), preferred_element_type=F32)


def _dot_tn(a, b):
    return lax.dot_general(a, b, (((0,), (0,)), ((), ())), preferred_element_type=F32)


def _split3(x):
    hi = x.astype(BF16)
    r = x - hi.astype(F32)
    mid = r.astype(BF16)
    lo = (r - mid.astype(F32)).astype(BF16)
    return hi, mid, lo


def _silu(x):
    return x * jax.nn.sigmoid(x)


def _rms(x):
    return x * lax.rsqrt(jnp.mean(x * x, axis=-1, keepdims=True) + NORM_EPS)


def _token_tile(batch, seq):
    if seq >= ROW_TILE:
        assert seq % ROW_TILE == 0
        return 1, ROW_TILE
    assert ROW_TILE % seq == 0 and seq % 8 == 0
    bb = min(batch, ROW_TILE // seq)
    assert batch % bb == 0
    return bb, seq


def _ada_body(c_ref, w_ref, b_ref, o_ref):
    s = _silu(c_ref[...])
    s_hi, s_mid, _ = _split3(s)
    w_hi, w_mid, _ = _split3(w_ref[0])
    o_ref[0] = _dot(s_hi, w_hi) + _dot(s_hi, w_mid) + _dot(s_mid, w_hi) + b_ref[0]


def _ada(c_all, w_ada, b_ada):
    depth, d, n6 = w_ada.shape
    bc = c_all.shape[0]
    tn = 512
    return pl.pallas_call(
        _ada_body,
        grid=(depth, n6 // tn),
        in_specs=[pl.BlockSpec((bc, d), lambda l, j: (0, 0)),
                  pl.BlockSpec((1, d, tn), lambda l, j: (l, 0, j)),
                  pl.BlockSpec((1, 1, tn), lambda l, j: (l, 0, j))],
        out_specs=pl.BlockSpec((1, bc, tn), lambda l, j: (l, 0, j)),
        out_shape=jax.ShapeDtypeStruct((depth, bc, n6), F32),
        compiler_params=_params(("arbitrary", "arbitrary")),
        name="ada",
    )(c_all, w_ada, b_ada.reshape(depth, 1, n6))


def _rope_tables(pos, n_heads):
    half = HEAD_DIM // 2
    freqs = ROPE_THETA ** (-jnp.arange(half, dtype=F32) / half)
    ang = pos.astype(F32)[:, None] * freqs[None, :]
    cos, sin = jnp.cos(ang), jnp.sin(ang)
    cos_t = jnp.tile(jnp.concatenate([cos, cos], axis=-1), (1, n_heads))
    sin_t = jnp.tile(jnp.concatenate([-sin, sin], axis=-1), (1, n_heads))
    return cos_t, sin_t


def _rope(x, cos, sin_signed):
    outs = []
    for c in range(x.shape[-1] // LANES):
        sl = slice(c * LANES, (c + 1) * LANES)
        xc = x[:, sl]
        lane = lax.broadcasted_iota(jnp.int32, xc.shape, 1)
        ahead = pltpu.roll(xc, LANES - HEAD_DIM // 2, axis=1)
        behind = pltpu.roll(xc, HEAD_DIM // 2, axis=1)
        rot = jnp.where((lane & (HEAD_DIM // 2)) == 0, ahead, behind)
        outs.append(xc * cos[:, sl] + rot * sin_signed[:, sl])
    return jnp.concatenate(outs, axis=-1)


def _in_proj_body(x_ref, sh_ref, sc_ref, w_ref, cos_ref, sin_ref, *out_refs, bb, tt, widths, n_paired):
    d = x_ref.shape[-1]
    x = x_ref[...].reshape(bb, tt, d)
    h = _rms(x) * (1.0 + sc_ref[...]) + sh_ref[...]
    hb = h.reshape(bb * tt, d).astype(BF16)
    off = 0
    for idx, (o_ref, wd) in enumerate(zip(out_refs, widths)):
        p = _dot(hb, w_ref[:, off:off + wd])
        if idx < 2:
            p = _rope(p, cos_ref[...], sin_ref[...])
        if idx < n_paired:
            for c in range(wd // LANES):
                o_ref[c] = p[:, c * LANES:(c + 1) * LANES]
        else:
            o_ref[...] = p
        off += wd


def _in_proj(x2, shift, scale, w_in_bf, cos_t, sin_t, batch, seq, widths, n_paired):
    n, d = x2.shape
    bb, tt = _token_tile(batch, seq)
    tm = bb * tt
    n_t = seq // tt
    table_blocks = cos_t.shape[0] // tm
    body = functools.partial(_in_proj_body, bb=bb, tt=tt, widths=widths, n_paired=n_paired)
    mod_spec = pl.BlockSpec((bb, 1, d), lambda i: (i // n_t, 0, 0))
    tab_spec = pl.BlockSpec((tm, cos_t.shape[1]), lambda i: (i % table_blocks, 0))
    out_specs, out_shape = [], []
    for idx, wd in enumerate(widths):
        if idx < n_paired:
            out_specs.append(pl.BlockSpec((wd // LANES, tm, LANES), lambda i: (0, i, 0)))
            out_shape.append(jax.ShapeDtypeStruct((wd // LANES, n, LANES), F32))
        else:
            out_specs.append(pl.BlockSpec((tm, wd), lambda i: (i, 0)))
            out_shape.append(jax.ShapeDtypeStruct((n, wd), F32))
    return pl.pallas_call(
        body,
        grid=(n // tm,),
        in_specs=[pl.BlockSpec((tm, d), lambda i: (i, 0)), mod_spec, mod_spec,
                  pl.BlockSpec(w_in_bf.shape, lambda i: (0, 0)), tab_spec, tab_spec],
        out_specs=out_specs,
        out_shape=out_shape,
        compiler_params=_params(("arbitrary",)),
        name="in_proj",
    )(x2, shift, scale, w_in_bf, cos_t, sin_t)


def _prompt_attn_body(q_ref, k_ref, v_ref, o_ref, m_ref, l_ref, acc_ref):
    n_pairs, seq, _ = q_ref.shape
    blk = WINDOW_STEPS
    lane = lax.broadcasted_iota(jnp.int32, (blk, LANES), 1)
    first = lane < HEAD_DIM
    scale = HEAD_DIM ** -0.5

    def rows(start, dil):
        return pl.ds(start, blk) if dil == 1 else pl.ds(start, blk, stride=dil)

    for b_idx, dil in enumerate(DILATIONS):
        n_blk = seq // dil // blk
        has_prev = n_blk > 1
        n_keys = 2 * blk if has_prev else blk
        q_row = lax.broadcasted_iota(jnp.int32, (2 * blk, n_keys), 0) & (blk - 1)
        k_col = lax.broadcasted_iota(jnp.int32, (2 * blk, n_keys), 1)
        is_first, is_last = b_idx == 0, b_idx == len(DILATIONS) - 1

        def unit(u, carry, dil=dil, has_prev=has_prev, q_row=q_row, k_col=k_col,
                 is_first=is_first, is_last=is_last):
            res = u & (dil - 1)
            i = u >> (dil.bit_length() - 1)
            start = res + dil * blk * i
            if dil == 1:
                start = pl.multiple_of(start, blk)
            if has_prev:
                prev_start = jnp.maximum(start - dil * blk, res)
                lo = jnp.maximum(q_row, jnp.where(i > 0, 0, blk))
                valid = (k_col >= lo) & (k_col <= q_row + blk)
            else:
                valid = k_col <= q_row
            for p in range(n_pairs):
                q = q_ref[p, rows(start, dil), :] * scale
                q2 = jnp.concatenate([jnp.where(first, q, 0.0), jnp.where(first, 0.0, q)], axis=0)
                kc = k_ref[p, rows(start, dil), :]
                vc = v_ref[p, rows(start, dil), :]
                if has_prev:
                    kc = jnp.concatenate([k_ref[p, rows(prev_start, dil), :], kc], axis=0)
                    vc = jnp.concatenate([v_ref[p, rows(prev_start, dil), :], vc], axis=0)
                s = jnp.where(valid, _dot_nt(q2.astype(BF16), kc.astype(BF16)), -jnp.inf)
                m2 = jnp.max(s, axis=-1, keepdims=True)
                e = jnp.exp(s - m2)
                l2 = jnp.sum(e, axis=-1, keepdims=True)
                pv = _dot(e.astype(BF16), vc.astype(BF16))
                m_b = jnp.where(first, m2[:blk], m2[blk:])
                l_b = jnp.where(first, l2[:blk], l2[blk:])
                acc_b = jnp.where(first, pv[:blk], pv[blk:])
                if not is_first:
                    m_o = m_ref[p, rows(start, dil), :]
                    m_n = jnp.maximum(m_o, m_b)
                    w_o, w_b = jnp.exp(m_o - m_n), jnp.exp(m_b - m_n)
                    l_b = w_o * l_ref[p, rows(start, dil), :] + w_b * l_b
                    acc_b = w_o * acc_ref[p, rows(start, dil), :] + w_b * acc_b
                    m_b = m_n
                if is_last:
                    o_ref[p, rows(start, dil), :] = acc_b / l_b
                else:
                    m_ref[p, rows(start, dil), :] = m_b
                    l_ref[p, rows(start, dil), :] = l_b
                    acc_ref[p, rows(start, dil), :] = acc_b
            return carry

        lax.fori_loop(0, dil * n_blk, unit, 0)


def _prompt_attention(q3, k3, v3, batch, seq):
    n_pairs = q3.shape[0]
    assert seq % (WINDOW_STEPS * max(DILATIONS)) == 0
    spec = pl.BlockSpec((n_pairs, seq, LANES), lambda b: (0, b, 0))
    return pl.pallas_call(
        _prompt_attn_body,
        grid=(batch,),
        in_specs=[spec, spec, spec],
        out_specs=spec,
        out_shape=jax.ShapeDtypeStruct(q3.shape, F32),
        scratch_shapes=[pltpu.VMEM((n_pairs, seq, LANES), F32)] * 3,
        compiler_params=_params(("arbitrary",)),
        name="prompt_attn",
    )(q3, k3, v3)


def _branch_count(delta):
    cnt = jnp.zeros(delta.shape, F32)
    for dil in DILATIONS:
        hit = (delta >= 0) & (delta <= WINDOW_STEPS * dil) & ((delta & (dil - 1)) == 0)
        cnt = cnt + jnp.where(hit, 1.0, 0.0)
    return cnt


def _cache_attn_body(q_ref, kn_ref, vn_ref, kt_ref, vt_ref, o_ref, *, past):
    n_pairs, t_new, _ = q_ref.shape
    t_c = lax.broadcasted_iota(jnp.int32, (t_new, past), 0)
    j_c = lax.broadcasted_iota(jnp.int32, (t_new, past), 1)
    cnt_c = _branch_count(past + t_c - j_c)
    t_n = lax.broadcasted_iota(jnp.int32, (t_new, t_new), 0)
    j_n = lax.broadcasted_iota(jnp.int32, (t_new, t_new), 1)
    cnt_n = _branch_count(t_n - j_n)
    scale = HEAD_DIM ** -0.5
    for p in range(n_pairs):
        q_pair, kn_pair, vn_pair = q_ref[p] * scale, kn_ref[p], vn_ref[p]
        outs = []
        for hh in range(LANES // HEAD_DIM):
            h = p * (LANES // HEAD_DIM) + hh
            sl = slice(hh * HEAD_DIM, (hh + 1) * HEAD_DIM)
            q = q_pair[:, sl]
            s_c = jnp.where(cnt_c > 0, _dot(q.astype(BF16), kt_ref[0, 0, h].astype(BF16)), -jnp.inf)
            s_n = jnp.where(cnt_n > 0, _dot_nt(q, kn_pair[:, sl]), -jnp.inf)
            m = jnp.maximum(jnp.max(s_c, axis=-1, keepdims=True), jnp.max(s_n, axis=-1, keepdims=True))
            p_c = cnt_c * jnp.exp(s_c - m)
            p_n = cnt_n * jnp.exp(s_n - m)
            den = jnp.sum(p_c, axis=-1, keepdims=True) + jnp.sum(p_n, axis=-1, keepdims=True)
            acc = _dot_nt(p_c.astype(BF16), vt_ref[0, 0, h].astype(BF16)) + _dot(p_n, vn_pair[:, sl])
            outs.append(acc / den)
        o_ref[p] = jnp.concatenate(outs, axis=-1)


def _cache_attention(q3, kn3, vn3, cache_k, cache_v, layer, batch, seq):
    depth, _, past, n_heads, hd = cache_k.shape
    n_pairs = q3.shape[0]
    by_head = lambda a: jnp.transpose(a, (0, 1, 3, 4, 2))
    new = pl.BlockSpec((n_pairs, seq, LANES), lambda b: (0, b, 0))
    old = pl.BlockSpec((1, 1, n_heads, hd, past), lambda b: (layer, b, 0, 0, 0))
    return pl.pallas_call(
        functools.partial(_cache_attn_body, past=past),
        grid=(batch,),
        in_specs=[new, new, new, old, old],
        out_specs=new,
        out_shape=jax.ShapeDtypeStruct(q3.shape, F32),
        compiler_params=_params(("arbitrary",)),
        name="cache_attn",
    )(q3, kn3, vn3, by_head(cache_k), by_head(cache_v))


def _lower_bound(logits, layer):
    e = jnp.exp(logits - jnp.max(logits, axis=0, keepdims=True))
    sm = e / jnp.sum(e, axis=0, keepdims=True)
    lb = jnp.zeros_like(sm[0:1])
    for j in range(1, layer + 1):
        lb = lb + sm[j:j + 1]
    return lb


def _hgrn_body(lbl_ref, nw_ref, q_ref, f_ref, i_ref, g_ref, s0_ref, o_ref, sout_ref,
               st_ref, b2_ref, kk_ref, eb_ref, q_s_ref, v_s_ref, qt_ref, kh_ref, o_s_ref,
               *, layer, n_chunks):
    c = REC_CHUNK
    d_rec = q_ref.shape[-1]
    n_pairs = d_rec // LANES
    rows_in = q_ref.shape[1]
    rows = n_chunks * c
    grp = min(rows, LANES)

    @pl.when(pl.program_id(1) == 0)
    def _():
        st_ref[...] = s0_ref[0]

    def padded(ref):
        x = ref[0]
        if rows_in < rows:
            x = jnp.concatenate([x, jnp.zeros((rows - rows_in, d_rec), F32)], axis=0)
        return x

    lb = _lower_bound(lbl_ref[...], layer)
    log_lb = jnp.log(lb)
    z = padded(f_ref)
    log_sig = jnp.minimum(z, 0.0) - jnp.log1p(jnp.exp(-jnp.abs(z)))
    b_term = jnp.log1p(-lb) + log_sig
    log_f = jnp.maximum(log_lb, b_term) + jnp.log1p(jnp.exp(-jnp.abs(log_lb - b_term)))
    kk = (1.0 - lb) * jax.nn.sigmoid(-z)
    if rows_in < rows:
        live = lax.broadcasted_iota(jnp.int32, (rows, d_rec), 0) < rows_in
        log_f = jnp.where(live, log_f, 0.0)
        kk = jnp.where(live, kk, 0.0)
    r_i = lax.broadcasted_iota(jnp.int32, (grp, grp), 0)
    c_i = lax.broadcasted_iota(jnp.int32, (grp, grp), 1)
    same_chunk = (r_i >> (c.bit_length() - 1)) == (c_i >> (c.bit_length() - 1))
    prefix = jnp.where(same_chunk & (c_i <= r_i), 1.0, 0.0).astype(BF16)
    suffix = jnp.where(same_chunk & (c_i > r_i), 1.0, 0.0).astype(BF16)
    q_all = padded(q_ref)
    for g in range(rows // grp):
        gs = slice(g * grp, (g + 1) * grp)
        parts = _split3(log_f[gs])
        b = sum(_dot(prefix, part) for part in parts)
        r = sum(_dot(suffix, part) for part in parts)
        eb = jnp.exp(b)
        b2_ref[gs, :] = b * LOG2_E
        eb_ref[gs, :] = eb
        qt_ref[gs, :] = (q_all[gs] * eb).astype(BF16)
        kh_ref[gs, :] = (kk[gs] * jnp.exp(r)).astype(BF16)
    kk_ref[...] = kk
    q_s_ref[...] = q_all
    v_s_ref[...] = padded(i_ref)

    l_r = lax.broadcasted_iota(jnp.int32, (LANES, LANES), 0) >> (HEAD_DIM.bit_length() - 1)
    l_c = lax.broadcasted_iota(jnp.int32, (LANES, LANES), 1) >> (HEAD_DIM.bit_length() - 1)
    same_head = l_r == l_c
    head_sum = jnp.where(same_head, 1.0, 0.0).astype(BF16)
    row = lax.broadcasted_iota(jnp.int32, (c, LANES), 0)

    def chunk(ci, carry):
        r0 = pl.multiple_of(ci * c, c)
        rs = pl.ds(r0, c)
        for p in range(n_pairs):
            sl = slice(p * LANES, (p + 1) * LANES)
            b2, q, v, k = b2_ref[rs, sl], q_s_ref[rs, sl], v_s_ref[rs, sl], kk_ref[rs, sl]
            xs = []
            for s_i in range(c):
                cap = jnp.where(row >= s_i, 0.0, -jnp.inf)
                e = jnp.exp2(jnp.minimum(b2 - b2[s_i:s_i + 1], cap))
                xs.append((q * e) * k[s_i:s_i + 1])
            a_rep = _dot(jnp.concatenate(xs, axis=0).astype(BF16), head_sum)
            o = a_rep[0:c] * v[0:1]
            for s_i in range(1, c):
                o = o + a_rep[s_i * c:(s_i + 1) * c] * v[s_i:s_i + 1]
            st = st_ref[p]
            o = o + _dot_nt(qt_ref[rs, sl], st.astype(BF16))
            upd = _dot_tn(v.astype(BF16), kh_ref[rs, sl])
            st_ref[p] = st * eb_ref[rs, sl][c - 1:c] + jnp.where(same_head, upd, 0.0)
            o_s_ref[rs, sl] = o
        return carry

    lax.fori_loop(0, n_chunks, chunk, 0, unroll=2 if n_chunks % 2 == 0 else 1)

    g_all = padded(g_ref)
    outs = []
    for p in range(n_pairs):
        sl = slice(p * LANES, (p + 1) * LANES)
        o = o_s_ref[:, sl]
        sq_hi, sq_mid, _ = _split3(o * o)
        ms = (_dot(sq_hi, head_sum) + _dot(sq_mid, head_sum)) * (1.0 / HEAD_DIM)
        outs.append(o * lax.rsqrt(ms + NORM_EPS) * nw_ref[:, sl] * _silu(g_all[:, sl]))
    o_ref[0] = jnp.concatenate(outs, axis=-1)[:rows_in]

    @pl.when(pl.program_id(1) == pl.num_programs(1) - 1)
    def _():
        sout_ref[0] = st_ref[...]


def _pair_state(s):
    b, h, kd, vd = s.shape
    st = jnp.swapaxes(s, 2, 3).reshape(b, h // 2, 2, vd, kd)
    z = jnp.zeros_like(st[:, :, 0])
    top = jnp.concatenate([st[:, :, 0], z], axis=-1)
    bot = jnp.concatenate([z, st[:, :, 1]], axis=-1)
    return jnp.concatenate([top, bot], axis=-2)


def _unpair_state(sp):
    b, hp = sp.shape[:2]
    s0 = sp[:, :, :HEAD_DIM, :HEAD_DIM]
    s1 = sp[:, :, HEAD_DIM:, HEAD_DIM:]
    st = jnp.stack([s0, s1], axis=2).reshape(b, 2 * hp, HEAD_DIM, HEAD_DIM)
    return jnp.swapaxes(st, 2, 3)


def _hgrn(qr, fr, ir, gr, s0, lb_logits, norm_w, layer, batch, seq):
    d_rec = qr.shape[-1]
    n_pairs = d_rec // LANES
    tb = min(seq, ROW_TILE)
    assert seq % tb == 0 and (tb % REC_CHUNK == 0 or tb == seq < REC_CHUNK)
    n_chunks = max(tb // REC_CHUNK, 1)
    rows = n_chunks * REC_CHUNK
    view = lambda a: a.reshape(batch, seq, d_rec)
    tok = pl.BlockSpec((1, tb, d_rec), lambda b, t: (b, t, 0))
    state = pl.BlockSpec((1, n_pairs, LANES, LANES), lambda b, t: (b, 0, 0, 0))
    o, s_new = pl.pallas_call(
        functools.partial(_hgrn_body, layer=layer, n_chunks=n_chunks),
        grid=(batch, seq // tb),
        in_specs=[pl.BlockSpec(lb_logits.shape, lambda b, t: (0, 0)),
                  pl.BlockSpec((1, d_rec), lambda b, t: (0, 0)),
                  tok, tok, tok, tok, state],
        out_specs=[tok, state],
        out_shape=[jax.ShapeDtypeStruct((batch, seq, d_rec), F32),
                   jax.ShapeDtypeStruct((batch, n_pairs, LANES, LANES), F32)],
        scratch_shapes=[pltpu.VMEM((n_pairs, LANES, LANES), F32)]
        + [pltpu.VMEM((rows, d_rec), F32)] * 5 + [pltpu.VMEM((rows, d_rec), BF16)] * 2
        + [pltpu.VMEM((rows, d_rec), F32)],
        compiler_params=_params(("arbitrary", "arbitrary")),
        name="hgrn",
    )(lb_logits, norm_w.reshape(1, d_rec), view(qr), view(fr), view(ir), view(gr), _pair_state(s0))
    return o.reshape(batch * seq, d_rec), _unpair_state(s_new)


CONV_PAD = 8


def _conv_body(bc_ref, cc_ref, xc_ref, buf_ref, w_ref, b_ref, oc_ref, new_ref, up_ref):
    seq = cc_ref.shape[1]
    width = w_ref.shape[0]
    u = cc_ref[0] * xc_ref[0]
    up_ref[pl.ds(CONV_PAD, seq), :] = u
    up_ref[pl.ds(CONV_PAD - (width - 1), width - 1), :] = buf_ref[0]
    y = u * w_ref[width - 1:width, :] + b_ref[...]
    for i in range(width - 1):
        y = y + up_ref[pl.ds(CONV_PAD - (width - 1) + i, seq), :] * w_ref[i:i + 1, :]
    oc_ref[0] = bc_ref[0] * y
    new_ref[0] = up_ref[pl.ds(CONV_PAD + seq - (width - 1), width - 1), :]


def _short_conv(bc, cc, xc, buf, conv_w, conv_b, batch, seq):
    d_conv = bc.shape[-1]
    width = conv_w.shape[0]
    view = lambda a: a.reshape(batch, seq, d_conv)
    tok = pl.BlockSpec((1, seq, d_conv), lambda b: (b, 0, 0))
    tail = pl.BlockSpec((1, width - 1, d_conv), lambda b: (b, 0, 0))
    oc, new = pl.pallas_call(
        _conv_body,
        grid=(batch,),
        in_specs=[tok, tok, tok, tail,
                  pl.BlockSpec((width, d_conv), lambda b: (0, 0)),
                  pl.BlockSpec((1, d_conv), lambda b: (0, 0))],
        out_specs=[tok, tail],
        out_shape=[jax.ShapeDtypeStruct((batch, seq, d_conv), F32),
                   jax.ShapeDtypeStruct((batch, width - 1, d_conv), F32)],
        scratch_shapes=[pltpu.VMEM((CONV_PAD + seq, d_conv), F32)],
        compiler_params=_params(("arbitrary",)),
        name="short_conv",
    )(view(bc), view(cc), view(xc), buf, conv_w, conv_b.reshape(1, d_conv))
    return oc.reshape(batch * seq, d_conv), new


def _out_proj_body(oa_ref, or_ref, oc_ref, x_ref, g_ref, sh_ref, sc_ref, w_ref, xo_ref, h_ref, *, bb, tt):
    n_pairs = oa_ref.shape[0]
    d_att, d_rec = n_pairs * LANES, or_ref.shape[-1]
    mix = (_dot(or_ref[...].astype(BF16), w_ref[d_att:d_att + d_rec, :])
           + _dot(oc_ref[...].astype(BF16), w_ref[d_att + d_rec:, :]))
    for p in range(n_pairs):
        mix = mix + _dot(oa_ref[p].astype(BF16), w_ref[p * LANES:(p + 1) * LANES, :])
    d = x_ref.shape[-1]
    x = x_ref[...].reshape(bb, tt, d) + g_ref[...] * mix.reshape(bb, tt, d)
    xo_ref[...] = x.reshape(bb * tt, d)
    h = _rms(x) * (1.0 + sc_ref[...]) + sh_ref[...]
    h_ref[...] = h.reshape(bb * tt, d).astype(BF16)


def _out_proj(oa3, o_r, oc, x2, gate, shift, scale, w_out_bf, batch, seq):
    n, d = x2.shape
    bb, tt = _token_tile(batch, seq)
    tm = bb * tt
    n_t = seq // tt
    tok = lambda a: pl.BlockSpec((tm, a.shape[-1]), lambda i: (i, 0))
    mod_spec = pl.BlockSpec((bb, 1, d), lambda i: (i // n_t, 0, 0))
    return pl.pallas_call(
        functools.partial(_out_proj_body, bb=bb, tt=tt),
        grid=(n // tm,),
        in_specs=[pl.BlockSpec((oa3.shape[0], tm, LANES), lambda i: (0, i, 0)), tok(o_r), tok(oc), tok(x2)]
        + [mod_spec] * 3 + [pl.BlockSpec(w_out_bf.shape, lambda i: (0, 0))],
        out_specs=[pl.BlockSpec((tm, d), lambda i: (i, 0))] * 2,
        out_shape=[jax.ShapeDtypeStruct((n, d), F32), jax.ShapeDtypeStruct((n, d), BF16)],
        compiler_params=_params(("arbitrary",)),
        name="out_proj",
    )(oa3, o_r, oc, x2, gate, shift, scale, w_out_bf)


def _ffn_body(h_ref, x_ref, g_ref, wg_ref, wu_ref, wo_ref, fw_ref, o_ref, acc_ref, *, bb, tt, final):
    j = pl.program_id(1)

    @pl.when(j == 0)
    def _():
        acc_ref[...] = jnp.zeros_like(acc_ref)

    h = h_ref[...]
    a = (_silu(_dot(h, wg_ref[...])) * _dot(h, wu_ref[...])).astype(BF16)
    acc_ref[...] += _dot(a, wo_ref[...])

    @pl.when(j == pl.num_programs(1) - 1)
    def _():
        d = x_ref.shape[-1]
        x = x_ref[...].reshape(bb, tt, d) + g_ref[...] * acc_ref[...].reshape(bb, tt, d)
        if final:
            x = _rms(x) * fw_ref[...]
        o_ref[...] = x.reshape(bb * tt, d)


def _ffn(h2, x2, gate, w_ffn_in_bf, w_ffn_out_bf, final_w, batch, seq, final):
    n, d = x2.shape
    d_ff = w_ffn_out_bf.shape[0]
    bb, tt = _token_tile(batch, seq)
    tm = bb * tt
    n_t = seq // tt
    n_f = 2
    tf = d_ff // n_f
    assert tf % LANES == 0
    return pl.pallas_call(
        functools.partial(_ffn_body, bb=bb, tt=tt, final=final),
        grid=(n // tm, n_f),
        in_specs=[pl.BlockSpec((tm, d), lambda i, j: (i, 0)),
                  pl.BlockSpec((tm, d), lambda i, j: (i, 0)),
                  pl.BlockSpec((bb, 1, d), lambda i, j: (i // n_t, 0, 0)),
                  pl.BlockSpec((d, tf), lambda i, j: (0, j)),
                  pl.BlockSpec((d, tf), lambda i, j: (0, j + n_f)),
                  pl.BlockSpec((tf, d), lambda i, j: (j, 0)),
                  pl.BlockSpec((1, d), lambda i, j: (0, 0))],
        out_specs=pl.BlockSpec((tm, d), lambda i, j: (i, 0)),
        out_shape=jax.ShapeDtypeStruct((n, d), F32),
        scratch_shapes=[pltpu.VMEM((tm, d), F32)],
        compiler_params=_params(("arbitrary", "arbitrary")),
        name="ffn",
    )(h2, x2, gate, w_ffn_in_bf, w_ffn_in_bf, w_ffn_out_bf, final_w.reshape(1, d))


def _trunk(x, mods, pos, cache_k, cache_v, state_hgrn, state_conv, weights):
    (w_in, conv_w, conv_b, lb_logits, hgrn_norm_w, w_out, w_ffn_in, w_ffn_out, final_norm_w) = weights
    batch, seq, d = x.shape
    depth = w_in.shape[0]
    d_conv = conv_w.shape[-1]
    d_att = d_rec = (w_in.shape[-1] - 3 * d_conv) // 7
    n_heads = d_att // HEAD_DIM
    widths = (d_att,) * 3 + (d_rec,) * 4 + (d_conv,) * 3
    bb, tt = _token_tile(batch, seq)
    cos_t, sin_t = _rope_tables(pos, n_heads)
    if bb > 1:
        cos_t, sin_t = jnp.tile(cos_t, (bb, 1)), jnp.tile(sin_t, (bb, 1))
    x2 = x.reshape(batch * seq, d)
    ks, vs, hs, cs = [], [], [], []
    for l in range(depth):
        sh1, sc1, g1, sh2, sc2, g2 = [m.reshape(batch, 1, d) for m in jnp.split(mods[l], 6, axis=-1)]
        qa, ka, va, qr, fr, ir, gr, bc, cc, xc = _in_proj(
            x2, sh1, sc1, w_in[l], cos_t, sin_t, batch, seq, widths, 3)
        if cache_k is None:
            oa = _prompt_attention(qa, ka, va, batch, seq)
            s0 = jnp.zeros((batch, n_heads, HEAD_DIM, HEAD_DIM), F32)
            conv0 = jnp.zeros((batch, conv_w.shape[1] - 1, d_conv), F32)
        else:
            oa = _cache_attention(qa, ka, va, cache_k, cache_v, l, batch, seq)
            s0, conv0 = state_hgrn[l], state_conv[l]
        o_r, s_new = _hgrn(qr, fr, ir, gr, s0, lb_logits, hgrn_norm_w[l], l, batch, seq)
        oc, conv_new = _short_conv(bc, cc, xc, conv0, conv_w[l], conv_b[l], batch, seq)
        x2, h2 = _out_proj(oa, o_r, oc, x2, g1, sh2, sc2, w_out[l], batch, seq)
        x2 = _ffn(h2, x2, g2, w_ffn_in[l], w_ffn_out[l], final_norm_w, batch, seq, l == depth - 1)
        keep = min(WINDOW_STEPS * max(DILATIONS), seq)
        unpair = lambda a: jnp.swapaxes(a, 0, 1).reshape(batch, seq, n_heads, HEAD_DIM)[:, seq - keep:]
        ks.append(unpair(ka))
        vs.append(unpair(va))
        hs.append(s_new)
        cs.append(conv_new)
    return x2.reshape(batch, seq, d), jnp.stack(ks), jnp.stack(vs), jnp.stack(hs), jnp.stack(cs)


def kernel(x_prompt, x_sample, cache_k, cache_v, state_hgrn, state_conv, c_prompt, c_sample,
           w_ada, b_ada, w_in, conv_w, conv_b, hgrn_lb_logits, hgrn_norm_w, w_out,
           w_ffn_in, w_ffn_out, final_norm_w):
    n_prompt = c_prompt.shape[0]
    past_len = PAST_LEN
    assert cache_k.shape[2] == min(WINDOW_STEPS * max(DILATIONS), PAST_LEN) == PAST_LEN
    mods = _ada(jnp.concatenate([c_prompt, c_sample], axis=0), w_ada, b_ada)
    weights = (w_in.astype(BF16), conv_w, conv_b, hgrn_lb_logits.astype(F32), hgrn_norm_w,
               w_out.astype(BF16), w_ffn_in.astype(BF16), w_ffn_out.astype(BF16), final_norm_w)
    pos_p = jnp.arange(x_prompt.shape[1], dtype=jnp.int32)
    pos_s = past_len + jnp.arange(x_sample.shape[1], dtype=jnp.int32)
    out_p = _trunk(x_prompt, mods[:, :n_prompt], pos_p, None, None, None, None, weights)
    out_s = _trunk(x_sample, mods[:, n_prompt:], pos_s, cache_k, cache_v, state_hgrn, state_conv, weights)
    return (out_p[0], out_s[0]) + out_p[1:] + out_s[1:]
```

```python
import functools
import math

import jax
import jax.numpy as jnp
from jax import lax
from jax.experimental import pallas as pl
from jax.experimental.pallas import tpu as pltpu

F32 = jnp.float32
BF16 = jnp.bfloat16

HEAD_DIM = 64
LANES = 128
DILATIONS = (1, 4, 16)
WINDOW_STEPS = 128
PAST_LEN = 2048
ROPE_THETA = 10000.0
NORM_EPS = 1e-6
REC_CHUNK = 16
UNROLL = 4
LOG2_E = 1.4426950408889634
ROW_TILE = 512
VMEM_LIMIT = 48 * 1024 * 1024


def _params(sem):
    return pltpu.CompilerParams(dimension_semantics=sem, vmem_limit_bytes=VMEM_LIMIT)


def _dot(a, b):
    return jnp.dot(a, b, preferred_element_type=F32)


def _dot_nt(a, b):
    return lax.dot_general(a, b, (((1,), (1,)), ((), ())), preferred_element_type=F32)


def _dot_tn(a, b):
    return lax.dot_general(a, b, (((0,), (0,)), ((), ())), preferred_element_type=F32)


def _split3(x):
    hi = x.astype(BF16)
    r = x - hi.astype(F32)
    mid = r.astype(BF16)
    lo = (r - mid.astype(F32)).astype(BF16)
    return hi, mid, lo


def _silu(x):
    return x * jax.nn.sigmoid(x)


def _rms(x):
    return x * lax.rsqrt(jnp.mean(x * x, axis=-1, keepdims=True) + NORM_EPS)


def _token_tile(batch, seq):
    if seq >= ROW_TILE:
        assert seq % ROW_TILE == 0
        return 1, ROW_TILE
    assert ROW_TILE % seq == 0 and seq % 8 == 0
    bb = min(batch, ROW_TILE // seq)
    assert batch % bb == 0
    return bb, seq


def _ada_body(c_ref, w_ref, b_ref, o_ref):
    s = _silu(c_ref[...])
    s_hi, s_mid, _ = _split3(s)
    w_hi, w_mid, _ = _split3(w_ref[0])
    o_ref[0] = _dot(s_hi, w_hi) + _dot(s_hi, w_mid) + _dot(s_mid, w_hi) + b_ref[0]


def _ada(c_all, w_ada, b_ada):
    depth, d, n6 = w_ada.shape
    bc = c_all.shape[0]
    tn = 512
    return pl.pallas_call(
        _ada_body,
        grid=(depth, n6 // tn),
        in_specs=[pl.BlockSpec((bc, d), lambda l, j: (0, 0)),
                  pl.BlockSpec((1, d, tn), lambda l, j: (l, 0, j)),
                  pl.BlockSpec((1, 1, tn), lambda l, j: (l, 0, j))],
        out_specs=pl.BlockSpec((1, bc, tn), lambda l, j: (l, 0, j)),
        out_shape=jax.ShapeDtypeStruct((depth, bc, n6), F32),
        compiler_params=_params(("arbitrary", "arbitrary")),
        name="ada",
    )(c_all, w_ada, b_ada.reshape(depth, 1, n6))


def _rope_tables(pos, n_heads):
    half = HEAD_DIM // 2
    freqs = ROPE_THETA ** (-jnp.arange(half, dtype=F32) / half)
    ang = pos.astype(F32)[:, None] * freqs[None, :]
    cos, sin = jnp.cos(ang), jnp.sin(ang)
    cos_t = jnp.tile(jnp.concatenate([cos, cos], axis=-1), (1, n_heads))
    sin_t = jnp.tile(jnp.concatenate([-sin, sin], axis=-1), (1, n_heads))
    return cos_t, sin_t


def _rope(x, cos, sin_signed):
    outs = []
    for c in range(x.shape[-1] // LANES):
        sl = slice(c * LANES, (c + 1) * LANES)
        xc = x[:, sl]
        lane = lax.broadcasted_iota(jnp.int32, xc.shape, 1)
        ahead = pltpu.roll(xc, LANES - HEAD_DIM // 2, axis=1)
        behind = pltpu.roll(xc, HEAD_DIM // 2, axis=1)
        rot = jnp.where((lane & (HEAD_DIM // 2)) == 0, ahead, behind)
        outs.append(xc * cos[:, sl] + rot * sin_signed[:, sl])
    return jnp.concatenate(outs, axis=-1)


def _in_proj_body(x_ref, sh_ref, sc_ref, w_ref, cos_ref, sin_ref, *out_refs, bb, tt, widths, n_paired):
    d = x_ref.shape[-1]
    x = x_ref[...].reshape(bb, tt, d)
    h = _rms(x) * (1.0 + sc_ref[...]) + sh_ref[...]
    hb = h.reshape(bb * tt, d).astype(BF16)
    off = 0
    for idx, (o_ref, wd) in enumerate(zip(out_refs, widths)):
        p = _dot(hb, w_ref[:, off:off + wd])
        if idx < 2:
            p = _rope(p, cos_ref[...], sin_ref[...])
        if idx < n_paired:
            for c in range(wd // LANES):
                o_ref[c] = p[:, c * LANES:(c + 1) * LANES]
        else:
            o_ref[...] = p
        off += wd


def _in_proj(x2, shift, scale, w_in_bf, cos_t, sin_t, batch, seq, widths, n_paired):
    n, d = x2.shape
    bb, tt = _token_tile(batch, seq)
    tm = bb * tt
    n_t = seq // tt
    table_blocks = cos_t.shape[0] // tm
    body = functools.partial(_in_proj_body, bb=bb, tt=tt, widths=widths, n_paired=n_paired)
    mod_spec = pl.BlockSpec((bb, 1, d), lambda i: (i // n_t, 0, 0))
    tab_spec = pl.BlockSpec((tm, cos_t.shape[1]), lambda i: (i % table_blocks, 0))
    out_specs, out_shape = [], []
    for idx, wd in enumerate(widths):
        if idx < n_paired:
            out_specs.append(pl.BlockSpec((wd // LANES, tm, LANES), lambda i: (0, i, 0)))
            out_shape.append(jax.ShapeDtypeStruct((wd // LANES, n, LANES), F32))
        else:
            out_specs.append(pl.BlockSpec((tm, wd), lambda i: (i, 0)))
            out_shape.append(jax.ShapeDtypeStruct((n, wd), F32))
    return pl.pallas_call(
        body,
        grid=(n // tm,),
        in_specs=[pl.BlockSpec((tm, d), lambda i: (i, 0)), mod_spec, mod_spec,
                  pl.BlockSpec(w_in_bf.shape, lambda i: (0, 0)), tab_spec, tab_spec],
        out_specs=out_specs,
        out_shape=out_shape,
        compiler_params=_params(("arbitrary",)),
        name="in_proj",
    )(x2, shift, scale, w_in_bf, cos_t, sin_t)


def _prompt_attn_body(q_ref, k_ref, v_ref, o_ref, m_ref, l_ref, acc_ref):
    n_pairs, seq, _ = q_ref.shape
    blk = WINDOW_STEPS
    lane = lax.broadcasted_iota(jnp.int32, (blk, LANES), 1)
    first = lane < HEAD_DIM
    scale = HEAD_DIM ** -0.5 * LOG2_E

    def rows(start, dil):
        return pl.ds(start, blk) if dil == 1 else pl.ds(start, blk, stride=dil)

    for b_idx, dil in enumerate(DILATIONS):
        n_blk = seq // dil // blk
        has_prev = n_blk > 1
        n_keys = 2 * blk if has_prev else blk
        q_row = lax.broadcasted_iota(jnp.int32, (2 * blk, n_keys), 0) & (blk - 1)
        k_col = lax.broadcasted_iota(jnp.int32, (2 * blk, n_keys), 1)
        is_first, is_last = b_idx == 0, b_idx == len(DILATIONS) - 1

        def unit(u, carry, dil=dil, has_prev=has_prev, q_row=q_row, k_col=k_col,
                 is_first=is_first, is_last=is_last):
            res = u & (dil - 1)
            i = u >> (dil.bit_length() - 1)
            start = res + dil * blk * i
            if dil == 1:
                start = pl.multiple_of(start, blk)
            if has_prev:
                prev_start = jnp.maximum(start - dil * blk, res)
                lo = jnp.maximum(q_row, jnp.where(i > 0, 0, blk))
                valid = (k_col >= lo) & (k_col <= q_row + blk)
            else:
                valid = k_col <= q_row
            loaded = []
            for p in range(n_pairs):
                q = q_ref[p, rows(start, dil), :] * scale
                kc = k_ref[p, rows(start, dil), :]
                vc = v_ref[p, rows(start, dil), :]
                if has_prev:
                    kc = jnp.concatenate([k_ref[p, rows(prev_start, dil), :], kc], axis=0)
                    vc = jnp.concatenate([v_ref[p, rows(prev_start, dil), :], vc], axis=0)
                old = None
                if not is_first:
                    old = (m_ref[p, rows(start, dil), :], l_ref[p, rows(start, dil), :],
                           acc_ref[p, rows(start, dil), :])
                loaded.append((q, kc, vc, old))
            scores = []
            for q, kc, vc, old in loaded:
                q2 = jnp.concatenate([jnp.where(first, q, 0.0), jnp.where(first, 0.0, q)], axis=0)
                scores.append(_dot_nt(q2.astype(BF16), kc.astype(BF16)))
            results = []
            for (q, kc, vc, old), s in zip(loaded, scores):
                s = jnp.where(valid, s, -jnp.inf)
                m2 = jnp.max(s, axis=-1, keepdims=True)
                e = jnp.exp2(s - m2).astype(BF16)
                v1 = jnp.concatenate([vc.astype(BF16), jnp.ones(vc.shape, BF16)], axis=1)
                pv = _dot(e, v1)
                m_b = jnp.where(first, m2[:blk], m2[blk:])
                l_b = jnp.where(first, pv[:blk, LANES:], pv[blk:, LANES:])
                acc_b = jnp.where(first, pv[:blk, :LANES], pv[blk:, :LANES])
                if not is_first:
                    m_o, l_o, acc_o = old
                    m_n = jnp.maximum(m_o, m_b)
                    w_o, w_b = jnp.exp2(m_o - m_n), jnp.exp2(m_b - m_n)
                    l_b = w_o * l_o + w_b * l_b
                    acc_b = w_o * acc_o + w_b * acc_b
                    m_b = m_n
                results.append((m_b, l_b, acc_b))
            for p, (m_b, l_b, acc_b) in enumerate(results):
                if is_last:
                    o_ref[p, rows(start, dil), :] = acc_b / l_b
                else:
                    m_ref[p, rows(start, dil), :] = m_b
                    l_ref[p, rows(start, dil), :] = l_b
                    acc_ref[p, rows(start, dil), :] = acc_b
            return carry

        lax.fori_loop(0, dil * n_blk, unit, 0)


def _prompt_attention(q3, k3, v3, batch, seq):
    n_pairs = q3.shape[0]
    assert seq % (WINDOW_STEPS * max(DILATIONS)) == 0
    spec = pl.BlockSpec((n_pairs, seq, LANES), lambda b: (0, b, 0))
    return pl.pallas_call(
        _prompt_attn_body,
        grid=(batch,),
        in_specs=[spec, spec, spec],
        out_specs=spec,
        out_shape=jax.ShapeDtypeStruct(q3.shape, F32),
        scratch_shapes=[pltpu.VMEM((n_pairs, seq, LANES), F32)] * 3,
        compiler_params=_params(("arbitrary",)),
        name="prompt_attn",
    )(q3, k3, v3)


def _branch_count(delta):
    cnt = jnp.zeros(delta.shape, F32)
    for dil in DILATIONS:
        hit = (delta >= 0) & (delta <= WINDOW_STEPS * dil) & ((delta & (dil - 1)) == 0)
        cnt = cnt + jnp.where(hit, 1.0, 0.0)
    return cnt


def _cache_attn_body(q_ref, kn_ref, vn_ref, kt_ref, vt_ref, o_ref, *, past):
    n_pairs, t_new, _ = q_ref.shape
    t_c = lax.broadcasted_iota(jnp.int32, (t_new, past), 0)
    j_c = lax.broadcasted_iota(jnp.int32, (t_new, past), 1)
    cnt_c = _branch_count(past + t_c - j_c)
    t_n = lax.broadcasted_iota(jnp.int32, (t_new, t_new), 0)
    j_n = lax.broadcasted_iota(jnp.int32, (t_new, t_new), 1)
    cnt_n = _branch_count(t_n - j_n)
    scale = HEAD_DIM ** -0.5
    for p in range(n_pairs):
        q_pair, kn_pair, vn_pair = q_ref[p] * scale, kn_ref[p], vn_ref[p]
        outs = []
        for hh in range(LANES // HEAD_DIM):
            h = p * (LANES // HEAD_DIM) + hh
            sl = slice(hh * HEAD_DIM, (hh + 1) * HEAD_DIM)
            q = q_pair[:, sl]
            s_c = jnp.where(cnt_c > 0, _dot(q.astype(BF16), kt_ref[0, 0, h].astype(BF16)), -jnp.inf)
            s_n = jnp.where(cnt_n > 0, _dot_nt(q, kn_pair[:, sl]), -jnp.inf)
            m = jnp.maximum(jnp.max(s_c, axis=-1, keepdims=True), jnp.max(s_n, axis=-1, keepdims=True))
            p_c = cnt_c * jnp.exp(s_c - m)
            p_n = cnt_n * jnp.exp(s_n - m)
            den = jnp.sum(p_c, axis=-1, keepdims=True) + jnp.sum(p_n, axis=-1, keepdims=True)
            acc = _dot_nt(p_c.astype(BF16), vt_ref[0, 0, h].astype(BF16)) + _dot(p_n, vn_pair[:, sl])
            outs.append(acc / den)
        o_ref[p] = jnp.concatenate(outs, axis=-1)


def _cache_attention(q3, kn3, vn3, cache_k, cache_v, layer, batch, seq):
    depth, _, past, n_heads, hd = cache_k.shape
    n_pairs = q3.shape[0]
    by_head = lambda a: jnp.transpose(a, (0, 1, 3, 4, 2))
    new = pl.BlockSpec((n_pairs, seq, LANES), lambda b: (0, b, 0))
    old = pl.BlockSpec((1, 1, n_heads, hd, past), lambda b: (layer, b, 0, 0, 0))
    return pl.pallas_call(
        functools.partial(_cache_attn_body, past=past),
        grid=(batch,),
        in_specs=[new, new, new, old, old],
        out_specs=new,
        out_shape=jax.ShapeDtypeStruct(q3.shape, F32),
        compiler_params=_params(("arbitrary",)),
        name="cache_attn",
    )(q3, kn3, vn3, by_head(cache_k), by_head(cache_v))


def _lower_bound(logits, layer):
    e = jnp.exp(logits - jnp.max(logits, axis=0, keepdims=True))
    sm = e / jnp.sum(e, axis=0, keepdims=True)
    lb = jnp.zeros_like(sm[0:1])
    for j in range(1, layer + 1):
        lb = lb + sm[j:j + 1]
    return lb


def _hgrn_body(lbl_ref, nw_ref, q_ref, f_ref, i_ref, g_ref, s0_ref, o_ref, sout_ref,
               st_ref, b2_ref, kk_ref, eb_ref, q_s_ref, v_s_ref, qt_ref, kh_ref, o_s_ref,
               *, layer, n_chunks):
    c = REC_CHUNK
    d_rec = q_ref.shape[-1]
    n_pairs = d_rec // LANES
    rows_in = q_ref.shape[1]
    rows = n_chunks * c
    grp = min(rows, LANES)

    @pl.when(pl.program_id(1) == 0)
    def _():
        st_ref[...] = s0_ref[0]

    def padded(ref):
        x = ref[0]
        if rows_in < rows:
            x = jnp.concatenate([x, jnp.zeros((rows - rows_in, d_rec), F32)], axis=0)
        return x

    lb = _lower_bound(lbl_ref[...], layer)
    log_lb = jnp.log(lb)
    z = padded(f_ref)
    log_sig = jnp.minimum(z, 0.0) - jnp.log1p(jnp.exp(-jnp.abs(z)))
    b_term = jnp.log1p(-lb) + log_sig
    log_f = jnp.maximum(log_lb, b_term) + jnp.log1p(jnp.exp(-jnp.abs(log_lb - b_term)))
    kk = (1.0 - lb) * jax.nn.sigmoid(-z)
    if rows_in < rows:
        live = lax.broadcasted_iota(jnp.int32, (rows, d_rec), 0) < rows_in
        log_f = jnp.where(live, log_f, 0.0)
        kk = jnp.where(live, kk, 0.0)
    r_i = lax.broadcasted_iota(jnp.int32, (grp, grp), 0)
    c_i = lax.broadcasted_iota(jnp.int32, (grp, grp), 1)
    same_chunk = (r_i >> (c.bit_length() - 1)) == (c_i >> (c.bit_length() - 1))
    prefix = jnp.where(same_chunk & (c_i <= r_i), 1.0, 0.0).astype(BF16)
    suffix = jnp.where(same_chunk & (c_i > r_i), 1.0, 0.0).astype(BF16)
    q_all = padded(q_ref)
    for g in range(rows // grp):
        gs = slice(g * grp, (g + 1) * grp)
        parts = _split3(log_f[gs])
        b = sum(_dot(prefix, part) for part in parts)
        r = sum(_dot(suffix, part) for part in parts)
        eb = jnp.exp(b)
        b2_ref[gs, :] = b * LOG2_E
        eb_ref[gs, :] = eb
        qt_ref[gs, :] = (q_all[gs] * eb).astype(BF16)
        kh_ref[gs, :] = (kk[gs] * jnp.exp(r)).astype(BF16)
    kk_ref[...] = kk
    q_s_ref[...] = q_all
    v_s_ref[...] = padded(i_ref)

    l_r = lax.broadcasted_iota(jnp.int32, (LANES, LANES), 0) >> (HEAD_DIM.bit_length() - 1)
    l_c = lax.broadcasted_iota(jnp.int32, (LANES, LANES), 1) >> (HEAD_DIM.bit_length() - 1)
    same_head = l_r == l_c
    head_sum = jnp.where(same_head, 1.0, 0.0).astype(BF16)
    row = lax.broadcasted_iota(jnp.int32, (c, LANES), 0)

    def chunk(ci, carry):
        r0 = pl.multiple_of(ci * c, c)
        rs = pl.ds(r0, c)
        for p in range(n_pairs):
            sl = slice(p * LANES, (p + 1) * LANES)
            b2, q, v, k = b2_ref[rs, sl], q_s_ref[rs, sl], v_s_ref[rs, sl], kk_ref[rs, sl]
            xs = []
            for s_i in range(c):
                cap = jnp.where(row >= s_i, 0.0, -jnp.inf)
                e = jnp.exp2(jnp.minimum(b2 - b2[s_i:s_i + 1], cap))
                xs.append((q * e) * k[s_i:s_i + 1])
            a_rep = _dot(jnp.concatenate(xs, axis=0).astype(BF16), head_sum)
            o = a_rep[0:c] * v[0:1]
            for s_i in range(1, c):
                o = o + a_rep[s_i * c:(s_i + 1) * c] * v[s_i:s_i + 1]
            st = st_ref[p]
            o = o + _dot_nt(qt_ref[rs, sl], st.astype(BF16))
            upd = _dot_tn(v.astype(BF16), kh_ref[rs, sl])
            st_ref[p] = st * eb_ref[rs, sl][c - 1:c] + jnp.where(same_head, upd, 0.0)
            o_s_ref[rs, sl] = o
        return carry

    lax.fori_loop(0, n_chunks, chunk, 0, unroll=UNROLL if n_chunks % UNROLL == 0 else 1)

    g_all = padded(g_ref)
    outs = []
    for p in range(n_pairs):
        sl = slice(p * LANES, (p + 1) * LANES)
        o = o_s_ref[:, sl]
        sq_hi, sq_mid, _ = _split3(o * o)
        ms = (_dot(sq_hi, head_sum) + _dot(sq_mid, head_sum)) * (1.0 / HEAD_DIM)
        outs.append(o * lax.rsqrt(ms + NORM_EPS) * nw_ref[:, sl] * _silu(g_all[:, sl]))
    o_ref[0] = jnp.concatenate(outs, axis=-1)[:rows_in]

    @pl.when(pl.program_id(1) == pl.num_programs(1) - 1)
    def _():
        sout_ref[0] = st_ref[...]


def _pair_state(s):
    b, h, kd, vd = s.shape
    st = jnp.swapaxes(s, 2, 3).reshape(b, h // 2, 2, vd, kd)
    z = jnp.zeros_like(st[:, :, 0])
    top = jnp.concatenate([st[:, :, 0], z], axis=-1)
    bot = jnp.concatenate([z, st[:, :, 1]], axis=-1)
    return jnp.concatenate([top, bot], axis=-2)


def _unpair_state(sp):
    b, hp = sp.shape[:2]
    s0 = sp[:, :, :HEAD_DIM, :HEAD_DIM]
    s1 = sp[:, :, HEAD_DIM:, HEAD_DIM:]
    st = jnp.stack([s0, s1], axis=2).reshape(b, 2 * hp, HEAD_DIM, HEAD_DIM)
    return jnp.swapaxes(st, 2, 3)


def _hgrn(qr, fr, ir, gr, s0, lb_logits, norm_w, layer, batch, seq):
    d_rec = qr.shape[-1]
    n_pairs = d_rec // LANES
    tb = min(seq, ROW_TILE)
    assert seq % tb == 0 and (tb % REC_CHUNK == 0 or tb == seq < REC_CHUNK)
    n_chunks = max(tb // REC_CHUNK, 1)
    rows = n_chunks * REC_CHUNK
    view = lambda a: a.reshape(batch, seq, d_rec)
    tok = pl.BlockSpec((1, tb, d_rec), lambda b, t: (b, t, 0))
    state = pl.BlockSpec((1, n_pairs, LANES, LANES), lambda b, t: (b, 0, 0, 0))
    o, s_new = pl.pallas_call(
        functools.partial(_hgrn_body, layer=layer, n_chunks=n_chunks),
        grid=(batch, seq // tb),
        in_specs=[pl.BlockSpec(lb_logits.shape, lambda b, t: (0, 0)),
                  pl.BlockSpec((1, d_rec), lambda b, t: (0, 0)),
                  tok, tok, tok, tok, state],
        out_specs=[tok, state],
        out_shape=[jax.ShapeDtypeStruct((batch, seq, d_rec), F32),
                   jax.ShapeDtypeStruct((batch, n_pairs, LANES, LANES), F32)],
        scratch_shapes=[pltpu.VMEM((n_pairs, LANES, LANES), F32)]
        + [pltpu.VMEM((rows, d_rec), F32)] * 5 + [pltpu.VMEM((rows, d_rec), BF16)] * 2
        + [pltpu.VMEM((rows, d_rec), F32)],
        compiler_params=_params(("arbitrary", "arbitrary")),
        name="hgrn",
    )(lb_logits, norm_w.reshape(1, d_rec), view(qr), view(fr), view(ir), view(gr), _pair_state(s0))
    return o.reshape(batch * seq, d_rec), _unpair_state(s_new)


def _hgrn_step_body(lblt_ref, nwt_ref, q_ref, f_ref, i_ref, g_ref, s0_ref, o_ref, sout_ref, *, layer):
    n_t, hd, _ = q_ref.shape
    logits = lblt_ref[...]
    e = jnp.exp(logits - jnp.max(logits, axis=1, keepdims=True))
    sm = e / jnp.sum(e, axis=1, keepdims=True)
    lb = jnp.zeros_like(sm[:, 0:1])
    for j in range(1, layer + 1):
        lb = lb + sm[:, j:j + 1]
    for t in range(n_t):
        z = f_ref[t]
        f_t = lb + (1.0 - lb) * jax.nn.sigmoid(z)
        k_t = (1.0 - lb) * jax.nn.sigmoid(-z)
        q_t, v_t = q_ref[t], i_ref[t]
        src = s0_ref if t == 0 else sout_ref
        o_t = jnp.zeros(v_t.shape, F32)
        for k in range(hd):
            s_k = f_t[k:k + 1] * src[0, 0, k] + k_t[k:k + 1] * v_t
            sout_ref[0, 0, k] = s_k
            o_t = o_t + s_k * q_t[k:k + 1]
        ms = jnp.mean(o_t * o_t, axis=0, keepdims=True)
        o_ref[t] = o_t * lax.rsqrt(ms + NORM_EPS) * nwt_ref[...] * _silu(g_ref[t])


def _hgrn_step(qr, fr, ir, gr, state_all, lb_logits, norm_w, layer, batch, seq):
    d_rec = qr.shape[-1]
    n_heads = d_rec // HEAD_DIM
    to_lanes = lambda a: jnp.transpose(a.reshape(batch, seq, d_rec), (1, 2, 0))
    tok = pl.BlockSpec((seq, HEAD_DIM, batch), lambda h: (0, h, 0))
    state_in = pl.BlockSpec((1, 1, HEAD_DIM, HEAD_DIM, batch), lambda h: (layer, h, 0, 0, 0))
    state_out = pl.BlockSpec((1, 1, HEAD_DIM, HEAD_DIM, batch), lambda h: (0, h, 0, 0, 0))
    o_t, s_new = pl.pallas_call(
        functools.partial(_hgrn_step_body, layer=layer),
        grid=(n_heads,),
        in_specs=[pl.BlockSpec((HEAD_DIM, lb_logits.shape[0]), lambda h: (h, 0)),
                  pl.BlockSpec((HEAD_DIM, 1), lambda h: (h, 0)),
                  tok, tok, tok, tok, state_in],
        out_specs=[tok, state_out],
        out_shape=[jax.ShapeDtypeStruct((seq, d_rec, batch), F32),
                   jax.ShapeDtypeStruct((1, n_heads, HEAD_DIM, HEAD_DIM, batch), F32)],
        compiler_params=_params(("arbitrary",)),
        name="hgrn_step",
    )(lb_logits.T, norm_w.reshape(d_rec, 1), to_lanes(qr), to_lanes(fr), to_lanes(ir), to_lanes(gr),
      jnp.transpose(state_all, (0, 2, 3, 4, 1)))
    o = jnp.transpose(o_t, (2, 0, 1)).reshape(batch * seq, d_rec)
    return o, jnp.transpose(s_new[0], (3, 0, 1, 2))


CONV_PAD = 8


def _conv_body(bc_ref, cc_ref, xc_ref, buf_ref, w_ref, b_ref, oc_ref, new_ref, up_ref):
    seq = cc_ref.shape[1]
    width = w_ref.shape[0]
    u = cc_ref[0] * xc_ref[0]
    up_ref[pl.ds(CONV_PAD, seq), :] = u
    up_ref[pl.ds(CONV_PAD - (width - 1), width - 1), :] = buf_ref[0]
    y = u * w_ref[width - 1:width, :] + b_ref[...]
    for i in range(width - 1):
        y = y + up_ref[pl.ds(CONV_PAD - (width - 1) + i, seq), :] * w_ref[i:i + 1, :]
    oc_ref[0] = bc_ref[0] * y
    new_ref[0] = up_ref[pl.ds(CONV_PAD + seq - (width - 1), width - 1), :]


def _short_conv(bc, cc, xc, buf, conv_w, conv_b, batch, seq):
    d_conv = bc.shape[-1]
    width = conv_w.shape[0]
    view = lambda a: a.reshape(batch, seq, d_conv)
    tok = pl.BlockSpec((1, seq, d_conv), lambda b: (b, 0, 0))
    tail = pl.BlockSpec((1, width - 1, d_conv), lambda b: (b, 0, 0))
    oc, new = pl.pallas_call(
        _conv_body,
        grid=(batch,),
        in_specs=[tok, tok, tok, tail,
                  pl.BlockSpec((width, d_conv), lambda b: (0, 0)),
                  pl.BlockSpec((1, d_conv), lambda b: (0, 0))],
        out_specs=[tok, tail],
        out_shape=[jax.ShapeDtypeStruct((batch, seq, d_conv), F32),
                   jax.ShapeDtypeStruct((batch, width - 1, d_conv), F32)],
        scratch_shapes=[pltpu.VMEM((CONV_PAD + seq, d_conv), F32)],
        compiler_params=_params(("arbitrary",)),
        name="short_conv",
    )(view(bc), view(cc), view(xc), buf, conv_w, conv_b.reshape(1, d_conv))
    return oc.reshape(batch * seq, d_conv), new


def _out_proj_body(oa_ref, or_ref, oc_ref, x_ref, g_ref, sh_ref, sc_ref, w_ref, xo_ref, h_ref, *, bb, tt):
    n_pairs = oa_ref.shape[0]
    d_att, d_rec = n_pairs * LANES, or_ref.shape[-1]
    mix = (_dot(or_ref[...].astype(BF16), w_ref[d_att:d_att + d_rec, :])
           + _dot(oc_ref[...].astype(BF16), w_ref[d_att + d_rec:, :]))
    for p in range(n_pairs):
        mix = mix + _dot(oa_ref[p].astype(BF16), w_ref[p * LANES:(p + 1) * LANES, :])
    d = x_ref.shape[-1]
    x = x_ref[...].reshape(bb, tt, d) + g_ref[...] * mix.reshape(bb, tt, d)
    xo_ref[...] = x.reshape(bb * tt, d)
    h = _rms(x) * (1.0 + sc_ref[...]) + sh_ref[...]
    h_ref[...] = h.reshape(bb * tt, d).astype(BF16)


def _out_proj(oa3, o_r, oc, x2, gate, shift, scale, w_out_bf, batch, seq):
    n, d = x2.shape
    bb, tt = _token_tile(batch, seq)
    tm = bb * tt
    n_t = seq // tt
    tok = lambda a: pl.BlockSpec((tm, a.shape[-1]), lambda i: (i, 0))
    mod_spec = pl.BlockSpec((bb, 1, d), lambda i: (i // n_t, 0, 0))
    return pl.pallas_call(
        functools.partial(_out_proj_body, bb=bb, tt=tt),
        grid=(n // tm,),
        in_specs=[pl.BlockSpec((oa3.shape[0], tm, LANES), lambda i: (0, i, 0)), tok(o_r), tok(oc), tok(x2)]
        + [mod_spec] * 3 + [pl.BlockSpec(w_out_bf.shape, lambda i: (0, 0))],
        out_specs=[pl.BlockSpec((tm, d), lambda i: (i, 0))] * 2,
        out_shape=[jax.ShapeDtypeStruct((n, d), F32), jax.ShapeDtypeStruct((n, d), BF16)],
        compiler_params=_params(("arbitrary",)),
        name="out_proj",
    )(oa3, o_r, oc, x2, gate, shift, scale, w_out_bf)


def _ffn_body(h_ref, x_ref, g_ref, wg_ref, wu_ref, wo_ref, fw_ref, o_ref, acc_ref, *, bb, tt, final):
    j = pl.program_id(1)

    @pl.when(j == 0)
    def _():
        acc_ref[...] = jnp.zeros_like(acc_ref)

    h = h_ref[...]
    a = (_silu(_dot(h, wg_ref[...])) * _dot(h, wu_ref[...])).astype(BF16)
    acc_ref[...] += _dot(a, wo_ref[...])

    @pl.when(j == pl.num_programs(1) - 1)
    def _():
        d = x_ref.shape[-1]
        x = x_ref[...].reshape(bb, tt, d) + g_ref[...] * acc_ref[...].reshape(bb, tt, d)
        if final:
            x = _rms(x) * fw_ref[...]
        o_ref[...] = x.reshape(bb * tt, d)


def _ffn(h2, x2, gate, w_ffn_in_bf, w_ffn_out_bf, final_w, batch, seq, final):
    n, d = x2.shape
    d_ff = w_ffn_out_bf.shape[0]
    bb, tt = _token_tile(batch, seq)
    tm = bb * tt
    n_t = seq // tt
    n_f = 2
    tf = d_ff // n_f
    assert tf % LANES == 0
    return pl.pallas_call(
        functools.partial(_ffn_body, bb=bb, tt=tt, final=final),
        grid=(n // tm, n_f),
        in_specs=[pl.BlockSpec((tm, d), lambda i, j: (i, 0)),
                  pl.BlockSpec((tm, d), lambda i, j: (i, 0)),
                  pl.BlockSpec((bb, 1, d), lambda i, j: (i // n_t, 0, 0)),
                  pl.BlockSpec((d, tf), lambda i, j: (0, j)),
                  pl.BlockSpec((d, tf), lambda i, j: (0, j + n_f)),
                  pl.BlockSpec((tf, d), lambda i, j: (j, 0)),
                  pl.BlockSpec((1, d), lambda i, j: (0, 0))],
        out_specs=pl.BlockSpec((tm, d), lambda i, j: (i, 0)),
        out_shape=jax.ShapeDtypeStruct((n, d), F32),
        scratch_shapes=[pltpu.VMEM((tm, d), F32)],
        compiler_params=_params(("arbitrary", "arbitrary")),
        name="ffn",
    )(h2, x2, gate, w_ffn_in_bf, w_ffn_in_bf, w_ffn_out_bf, final_w.reshape(1, d))


def _trunk(x, mods, pos, cache_k, cache_v, state_hgrn, state_conv, weights):
    (w_in, conv_w, conv_b, lb_logits, hgrn_norm_w, w_out, w_ffn_in, w_ffn_out, final_norm_w) = weights
    batch, seq, d = x.shape
    depth = w_in.shape[0]
    d_conv = conv_w.shape[-1]
    d_att = d_rec = (w_in.shape[-1] - 3 * d_conv) // 7
    n_heads = d_att // HEAD_DIM
    widths = (d_att,) * 3 + (d_rec,) * 4 + (d_conv,) * 3
    bb, tt = _token_tile(batch, seq)
    cos_t, sin_t = _rope_tables(pos, n_heads)
    if bb > 1:
        cos_t, sin_t = jnp.tile(cos_t, (bb, 1)), jnp.tile(sin_t, (bb, 1))
    x2 = x.reshape(batch * seq, d)
    ks, vs, hs, cs = [], [], [], []
    for l in range(depth):
        sh1, sc1, g1, sh2, sc2, g2 = [m.reshape(batch, 1, d) for m in jnp.split(mods[l], 6, axis=-1)]
        qa, ka, va, qr, fr, ir, gr, bc, cc, xc = _in_proj(
            x2, sh1, sc1, w_in[l], cos_t, sin_t, batch, seq, widths, 3)
        if cache_k is None:
            oa = _prompt_attention(qa, ka, va, batch, seq)
            s0 = jnp.zeros((batch, n_heads, HEAD_DIM, HEAD_DIM), F32)
            conv0 = jnp.zeros((batch, conv_w.shape[1] - 1, d_conv), F32)
        else:
            oa = _cache_attention(qa, ka, va, cache_k, cache_v, l, batch, seq)
            conv0 = state_conv[l]
        if cache_k is None:
            o_r, s_new = _hgrn(qr, fr, ir, gr, s0, lb_logits, hgrn_norm_w[l], l, batch, seq)
        else:
            o_r, s_new = _hgrn_step(qr, fr, ir, gr, state_hgrn, lb_logits, hgrn_norm_w[l], l, batch, seq)
        oc, conv_new = _short_conv(bc, cc, xc, conv0, conv_w[l], conv_b[l], batch, seq)
        x2, h2 = _out_proj(oa, o_r, oc, x2, g1, sh2, sc2, w_out[l], batch, seq)
        x2 = _ffn(h2, x2, g2, w_ffn_in[l], w_ffn_out[l], final_norm_w, batch, seq, l == depth - 1)
        ks.append(ka)
        vs.append(va)
        hs.append(s_new)
        cs.append(conv_new)
    keep = min(WINDOW_STEPS * max(DILATIONS), seq)

    def unpair(parts):
        a = jnp.stack(parts).reshape(depth, n_heads // 2, batch, seq, 2, HEAD_DIM)
        return jnp.transpose(a, (0, 2, 3, 1, 4, 5)).reshape(depth, batch, seq, n_heads, HEAD_DIM)[:, :, seq - keep:]

    return x2.reshape(batch, seq, d), unpair(ks), unpair(vs), jnp.stack(hs), jnp.stack(cs)


def kernel(x_prompt, x_sample, cache_k, cache_v, state_hgrn, state_conv, c_prompt, c_sample,
           w_ada, b_ada, w_in, conv_w, conv_b, hgrn_lb_logits, hgrn_norm_w, w_out,
           w_ffn_in, w_ffn_out, final_norm_w):
    n_prompt = c_prompt.shape[0]
    past_len = PAST_LEN
    assert cache_k.shape[2] == min(WINDOW_STEPS * max(DILATIONS), PAST_LEN) == PAST_LEN
    mods = _ada(jnp.concatenate([c_prompt, c_sample], axis=0), w_ada, b_ada)
    weights = (w_in.astype(BF16), conv_w, conv_b, hgrn_lb_logits.astype(F32), hgrn_norm_w,
               w_out.astype(BF16), w_ffn_in.astype(BF16), w_ffn_out.astype(BF16), final_norm_w)
    pos_p = jnp.arange(x_prompt.shape[1], dtype=jnp.int32)
    pos_s = past_len + jnp.arange(x_sample.shape[1], dtype=jnp.int32)
    out_p = _trunk(x_prompt, mods[:, :n_prompt], pos_p, None, None, None, None, weights)
    out_s = _trunk(x_sample, mods[:, n_prompt:], pos_s, cache_k, cache_v, state_hgrn, state_conv, weights)
    return (out_p[0], out_s[0]) + out_p[1:] + out_s[1:]
```

```python
import functools
import math

import jax
import jax.numpy as jnp
from jax import lax
from jax.experimental import pallas as pl
from jax.experimental.pallas import tpu as pltpu

F32 = jnp.float32
BF16 = jnp.bfloat16

HEAD_DIM = 64
LANES = 128
DILATIONS = (1, 4, 16)
WINDOW_STEPS = 128
PAST_LEN = 2048
ROPE_THETA = 10000.0
NORM_EPS = 1e-6
REC_CHUNK = 16
UNROLL = 4
LOG2_E = 1.4426950408889634
ROW_TILE = 512
CACHE_BATCH = 2
VMEM_LIMIT = 48 * 1024 * 1024


def _params(sem):
    return pltpu.CompilerParams(dimension_semantics=sem, vmem_limit_bytes=VMEM_LIMIT)


def _dot(a, b):
    return jnp.dot(a, b, preferred_element_type=F32)


def _dot_nt(a, b):
    return lax.dot_general(a, b, (((1,), (1,)), ((), ())), preferred_element_type=F32)


def _dot_tn(a, b):
    return lax.dot_general(a, b, (((0,), (0,)), ((), ())), preferred_element_type=F32)


def _split3(x):
    hi = x.astype(BF16)
    r = x - hi.astype(F32)
    mid = r.astype(BF16)
    lo = (r - mid.astype(F32)).astype(BF16)
    return hi, mid, lo


def _silu(x):
    return x * jax.nn.sigmoid(x)


def _rms(x):
    return x * lax.rsqrt(jnp.mean(x * x, axis=-1, keepdims=True) + NORM_EPS)


def _token_tile(batch, seq):
    if seq >= ROW_TILE:
        assert seq % ROW_TILE == 0
        return 1, ROW_TILE
    assert ROW_TILE % seq == 0 and seq % 8 == 0
    bb = min(batch, ROW_TILE // seq)
    assert batch % bb == 0
    return bb, seq


def _ada_body(c_ref, w_ref, b_ref, o_ref):
    s = _silu(c_ref[...])
    s_hi, s_mid, _ = _split3(s)
    w_hi, w_mid, _ = _split3(w_ref[0])
    o_ref[0] = _dot(s_hi, w_hi) + _dot(s_hi, w_mid) + _dot(s_mid, w_hi) + b_ref[0]


def _ada(c_all, w_ada, b_ada):
    depth, d, n6 = w_ada.shape
    bc = c_all.shape[0]
    tn = 512
    return pl.pallas_call(
        _ada_body,
        grid=(depth, n6 // tn),
        in_specs=[pl.BlockSpec((bc, d), lambda l, j: (0, 0)),
                  pl.BlockSpec((1, d, tn), lambda l, j: (l, 0, j)),
                  pl.BlockSpec((1, 1, tn), lambda l, j: (l, 0, j))],
        out_specs=pl.BlockSpec((1, bc, tn), lambda l, j: (l, 0, j)),
        out_shape=jax.ShapeDtypeStruct((depth, bc, n6), F32),
        compiler_params=_params(("arbitrary", "arbitrary")),
        name="ada",
    )(c_all, w_ada, b_ada.reshape(depth, 1, n6))


def _rope_tables(pos, n_heads):
    half = HEAD_DIM // 2
    freqs = ROPE_THETA ** (-jnp.arange(half, dtype=F32) / half)
    ang = pos.astype(F32)[:, None] * freqs[None, :]
    cos, sin = jnp.cos(ang), jnp.sin(ang)
    cos_t = jnp.tile(jnp.concatenate([cos, cos], axis=-1), (1, n_heads))
    sin_t = jnp.tile(jnp.concatenate([-sin, sin], axis=-1), (1, n_heads))
    return cos_t, sin_t


def _rope(x, cos, sin_signed):
    outs = []
    for c in range(x.shape[-1] // LANES):
        sl = slice(c * LANES, (c + 1) * LANES)
        xc = x[:, sl]
        lane = lax.broadcasted_iota(jnp.int32, xc.shape, 1)
        ahead = pltpu.roll(xc, LANES - HEAD_DIM // 2, axis=1)
        behind = pltpu.roll(xc, HEAD_DIM // 2, axis=1)
        rot = jnp.where((lane & (HEAD_DIM // 2)) == 0, ahead, behind)
        outs.append(xc * cos[:, sl] + rot * sin_signed[:, sl])
    return jnp.concatenate(outs, axis=-1)


def _in_proj_body(x_ref, sh_ref, sc_ref, w_ref, cos_ref, sin_ref, *out_refs, bb, tt, widths, n_paired):
    d = x_ref.shape[-1]
    x = x_ref[...].reshape(bb, tt, d)
    h = _rms(x) * (1.0 + sc_ref[...]) + sh_ref[...]
    hb = h.reshape(bb * tt, d).astype(BF16)
    off = 0
    for idx, (o_ref, wd) in enumerate(zip(out_refs, widths)):
        p = _dot(hb, w_ref[:, off:off + wd])
        if idx < 2:
            p = _rope(p, cos_ref[...], sin_ref[...])
        if idx < n_paired:
            for c in range(wd // LANES):
                o_ref[c] = p[:, c * LANES:(c + 1) * LANES]
        else:
            o_ref[...] = p
        off += wd


def _in_proj(x2, shift, scale, w_in_bf, cos_t, sin_t, batch, seq, widths, n_paired):
    n, d = x2.shape
    bb, tt = _token_tile(batch, seq)
    tm = bb * tt
    n_t = seq // tt
    table_blocks = cos_t.shape[0] // tm
    body = functools.partial(_in_proj_body, bb=bb, tt=tt, widths=widths, n_paired=n_paired)
    mod_spec = pl.BlockSpec((bb, 1, d), lambda i: (i // n_t, 0, 0))
    tab_spec = pl.BlockSpec((tm, cos_t.shape[1]), lambda i: (i % table_blocks, 0))
    out_specs, out_shape = [], []
    for idx, wd in enumerate(widths):
        if idx < n_paired:
            out_specs.append(pl.BlockSpec((wd // LANES, tm, LANES), lambda i: (0, i, 0)))
            out_shape.append(jax.ShapeDtypeStruct((wd // LANES, n, LANES), F32))
        else:
            out_specs.append(pl.BlockSpec((tm, wd), lambda i: (i, 0)))
            out_shape.append(jax.ShapeDtypeStruct((n, wd), F32))
    return pl.pallas_call(
        body,
        grid=(n // tm,),
        in_specs=[pl.BlockSpec((tm, d), lambda i: (i, 0)), mod_spec, mod_spec,
                  pl.BlockSpec(w_in_bf.shape, lambda i: (0, 0)), tab_spec, tab_spec],
        out_specs=out_specs,
        out_shape=out_shape,
        compiler_params=_params(("arbitrary",)),
        name="in_proj",
    )(x2, shift, scale, w_in_bf, cos_t, sin_t)


def _prompt_attn_body(q_ref, k_ref, v_ref, o_ref, m_ref, l_ref, acc_ref):
    n_pairs, seq, _ = q_ref.shape
    blk = WINDOW_STEPS
    lane = lax.broadcasted_iota(jnp.int32, (blk, LANES), 1)
    first = lane < HEAD_DIM
    scale = HEAD_DIM ** -0.5 * LOG2_E

    def rows(start, dil):
        return pl.ds(start, blk) if dil == 1 else pl.ds(start, blk, stride=dil)

    for b_idx, dil in enumerate(DILATIONS):
        n_blk = seq // dil // blk
        has_prev = n_blk > 1
        n_keys = 2 * blk if has_prev else blk
        q_row = lax.broadcasted_iota(jnp.int32, (2 * blk, n_keys), 0) & (blk - 1)
        k_col = lax.broadcasted_iota(jnp.int32, (2 * blk, n_keys), 1)
        is_first, is_last = b_idx == 0, b_idx == len(DILATIONS) - 1

        def unit(u, carry, dil=dil, has_prev=has_prev, q_row=q_row, k_col=k_col,
                 is_first=is_first, is_last=is_last):
            res = u & (dil - 1)
            i = u >> (dil.bit_length() - 1)
            start = res + dil * blk * i
            if dil == 1:
                start = pl.multiple_of(start, blk)
            if has_prev:
                prev_start = jnp.maximum(start - dil * blk, res)
                lo = jnp.maximum(q_row, jnp.where(i > 0, 0, blk))
                valid = (k_col >= lo) & (k_col <= q_row + blk)
            else:
                valid = k_col <= q_row
            loaded = []
            for p in range(n_pairs):
                q = q_ref[p, rows(start, dil), :] * scale
                kc = k_ref[p, rows(start, dil), :]
                vc = v_ref[p, rows(start, dil), :]
                if has_prev:
                    kc = jnp.concatenate([k_ref[p, rows(prev_start, dil), :], kc], axis=0)
                    vc = jnp.concatenate([v_ref[p, rows(prev_start, dil), :], vc], axis=0)
                old = None
                if not is_first:
                    old = (m_ref[p, rows(start, dil), :], l_ref[p, rows(start, dil), :],
                           acc_ref[p, rows(start, dil), :])
                loaded.append((q, kc, vc, old))
            scores = []
            for q, kc, vc, old in loaded:
                q2 = jnp.concatenate([jnp.where(first, q, 0.0), jnp.where(first, 0.0, q)], axis=0)
                scores.append(_dot_nt(q2.astype(BF16), kc.astype(BF16)))
            results = []
            for (q, kc, vc, old), s in zip(loaded, scores):
                s = jnp.where(valid, s, -jnp.inf)
                m2 = jnp.max(s, axis=-1, keepdims=True)
                e = jnp.exp2(s - m2).astype(BF16)
                v1 = jnp.concatenate([vc.astype(BF16), jnp.ones(vc.shape, BF16)], axis=1)
                pv = _dot(e, v1)
                m_b = jnp.where(first, m2[:blk], m2[blk:])
                l_b = jnp.where(first, pv[:blk, LANES:], pv[blk:, LANES:])
                acc_b = jnp.where(first, pv[:blk, :LANES], pv[blk:, :LANES])
                if not is_first:
                    m_o, l_o, acc_o = old
                    m_n = jnp.maximum(m_o, m_b)
                    w_o, w_b = jnp.exp2(m_o - m_n), jnp.exp2(m_b - m_n)
                    l_b = w_o * l_o + w_b * l_b
                    acc_b = w_o * acc_o + w_b * acc_b
                    m_b = m_n
                results.append((m_b, l_b, acc_b))
            for p, (m_b, l_b, acc_b) in enumerate(results):
                if is_last:
                    o_ref[p, rows(start, dil), :] = acc_b / l_b
                else:
                    m_ref[p, rows(start, dil), :] = m_b
                    l_ref[p, rows(start, dil), :] = l_b
                    acc_ref[p, rows(start, dil), :] = acc_b
            return carry

        lax.fori_loop(0, dil * n_blk, unit, 0, unroll=2)


def _prompt_attention(q3, k3, v3, batch, seq):
    n_pairs = q3.shape[0]
    assert seq % (WINDOW_STEPS * max(DILATIONS)) == 0
    spec = pl.BlockSpec((n_pairs, seq, LANES), lambda b: (0, b, 0))
    return pl.pallas_call(
        _prompt_attn_body,
        grid=(batch,),
        in_specs=[spec, spec, spec],
        out_specs=spec,
        out_shape=jax.ShapeDtypeStruct(q3.shape, F32),
        scratch_shapes=[pltpu.VMEM((n_pairs, seq, LANES), F32)] * 3,
        compiler_params=_params(("arbitrary",)),
        name="prompt_attn",
    )(q3, k3, v3)


def _branch_count(delta):
    cnt = jnp.zeros(delta.shape, F32)
    for dil in DILATIONS:
        hit = (delta >= 0) & (delta <= WINDOW_STEPS * dil) & ((delta & (dil - 1)) == 0)
        cnt = cnt + jnp.where(hit, 1.0, 0.0)
    return cnt


def _cache_attn_body(q_ref, kn_ref, vn_ref, kt_ref, vt_ref, o_ref, *, past, t_new):
    n_pairs = q_ref.shape[0]
    t_c = lax.broadcasted_iota(jnp.int32, (t_new, past), 0)
    j_c = lax.broadcasted_iota(jnp.int32, (t_new, past), 1)
    cnt_c = _branch_count(past + t_c - j_c)
    t_n = lax.broadcasted_iota(jnp.int32, (t_new, t_new), 0)
    j_n = lax.broadcasted_iota(jnp.int32, (t_new, t_new), 1)
    cnt_n = _branch_count(t_n - j_n)
    scale = HEAD_DIM ** -0.5
    for bi in range(kt_ref.shape[1]):
        rs = slice(bi * t_new, (bi + 1) * t_new)
        for p in range(n_pairs):
            q_pair, kn_pair, vn_pair = q_ref[p, rs, :] * scale, kn_ref[p, rs, :], vn_ref[p, rs, :]
            outs = []
            for hh in range(LANES // HEAD_DIM):
                h = p * (LANES // HEAD_DIM) + hh
                sl = slice(hh * HEAD_DIM, (hh + 1) * HEAD_DIM)
                q = q_pair[:, sl]
                s_c = jnp.where(cnt_c > 0, _dot(q.astype(BF16), kt_ref[0, bi, h].astype(BF16)), -jnp.inf)
                s_n = jnp.where(cnt_n > 0, _dot_nt(q, kn_pair[:, sl]), -jnp.inf)
                m = jnp.maximum(jnp.max(s_c, axis=-1, keepdims=True), jnp.max(s_n, axis=-1, keepdims=True))
                p_c = cnt_c * jnp.exp(s_c - m)
                p_n = cnt_n * jnp.exp(s_n - m)
                den = jnp.sum(p_c, axis=-1, keepdims=True) + jnp.sum(p_n, axis=-1, keepdims=True)
                acc = _dot_nt(p_c.astype(BF16), vt_ref[0, bi, h].astype(BF16)) + _dot(p_n, vn_pair[:, sl])
                outs.append(acc / den)
            o_ref[p, rs, :] = jnp.concatenate(outs, axis=-1)


def _cache_attention(q3, kn3, vn3, cache_k, cache_v, layer, batch, seq):
    depth, _, past, n_heads, hd = cache_k.shape
    n_pairs = q3.shape[0]
    bb = CACHE_BATCH if batch % CACHE_BATCH == 0 else 1
    by_head = lambda a: jnp.transpose(a, (0, 1, 3, 4, 2))
    new = pl.BlockSpec((n_pairs, bb * seq, LANES), lambda b: (0, b, 0))
    old = pl.BlockSpec((1, bb, n_heads, hd, past), lambda b: (layer, b, 0, 0, 0))
    return pl.pallas_call(
        functools.partial(_cache_attn_body, past=past, t_new=seq),
        grid=(batch // bb,),
        in_specs=[new, new, new, old, old],
        out_specs=new,
        out_shape=jax.ShapeDtypeStruct(q3.shape, F32),
        compiler_params=_params(("arbitrary",)),
        name="cache_attn",
    )(q3, kn3, vn3, by_head(cache_k), by_head(cache_v))


def _lower_bound(logits, layer):
    e = jnp.exp(logits - jnp.max(logits, axis=0, keepdims=True))
    sm = e / jnp.sum(e, axis=0, keepdims=True)
    lb = jnp.zeros_like(sm[0:1])
    for j in range(1, layer + 1):
        lb = lb + sm[j:j + 1]
    return lb


def _hgrn_body(lbl_ref, nw_ref, q_ref, f_ref, i_ref, g_ref, s0_ref, o_ref, sout_ref,
               st_ref, b2_ref, kk_ref, eb_ref, q_s_ref, v_s_ref, qt_ref, kh_ref, o_s_ref,
               *, layer, n_chunks):
    c = REC_CHUNK
    d_rec = q_ref.shape[-1]
    n_pairs = d_rec // LANES
    rows_in = q_ref.shape[1]
    rows = n_chunks * c
    grp = min(rows, LANES)

    @pl.when(pl.program_id(1) == 0)
    def _():
        st_ref[...] = s0_ref[0]

    def padded(ref):
        x = ref[0]
        if rows_in < rows:
            x = jnp.concatenate([x, jnp.zeros((rows - rows_in, d_rec), F32)], axis=0)
        return x

    lb = _lower_bound(lbl_ref[...], layer)
    log_lb = jnp.log(lb)
    z = padded(f_ref)
    log_sig = jnp.minimum(z, 0.0) - jnp.log1p(jnp.exp(-jnp.abs(z)))
    b_term = jnp.log1p(-lb) + log_sig
    log_f = jnp.maximum(log_lb, b_term) + jnp.log1p(jnp.exp(-jnp.abs(log_lb - b_term)))
    kk = (1.0 - lb) * jax.nn.sigmoid(-z)
    if rows_in < rows:
        live = lax.broadcasted_iota(jnp.int32, (rows, d_rec), 0) < rows_in
        log_f = jnp.where(live, log_f, 0.0)
        kk = jnp.where(live, kk, 0.0)
    r_i = lax.broadcasted_iota(jnp.int32, (grp, grp), 0)
    c_i = lax.broadcasted_iota(jnp.int32, (grp, grp), 1)
    same_chunk = (r_i >> (c.bit_length() - 1)) == (c_i >> (c.bit_length() - 1))
    prefix = jnp.where(same_chunk & (c_i <= r_i), 1.0, 0.0).astype(BF16)
    suffix = jnp.where(same_chunk & (c_i > r_i), 1.0, 0.0).astype(BF16)
    q_all = padded(q_ref)
    for g in range(rows // grp):
        gs = slice(g * grp, (g + 1) * grp)
        parts = _split3(log_f[gs])
        b = sum(_dot(prefix, part) for part in parts)
        r = sum(_dot(suffix, part) for part in parts)
        eb = jnp.exp(b)
        b2_ref[gs, :] = b * LOG2_E
        eb_ref[gs, :] = eb
        qt_ref[gs, :] = (q_all[gs] * eb).astype(BF16)
        kh_ref[gs, :] = (kk[gs] * jnp.exp(r)).astype(BF16)
    kk_ref[...] = kk
    q_s_ref[...] = q_all
    v_s_ref[...] = padded(i_ref)

    l_r = lax.broadcasted_iota(jnp.int32, (LANES, LANES), 0) >> (HEAD_DIM.bit_length() - 1)
    l_c = lax.broadcasted_iota(jnp.int32, (LANES, LANES), 1) >> (HEAD_DIM.bit_length() - 1)
    same_head = l_r == l_c
    head_sum = jnp.where(same_head, 1.0, 0.0).astype(BF16)
    row = lax.broadcasted_iota(jnp.int32, (c, LANES), 0)

    def chunk(ci, carry):
        r0 = pl.multiple_of(ci * c, c)
        rs = pl.ds(r0, c)
        for p in range(n_pairs):
            sl = slice(p * LANES, (p + 1) * LANES)
            b2, q, v, k = b2_ref[rs, sl], q_s_ref[rs, sl], v_s_ref[rs, sl], kk_ref[rs, sl]
            xs = []
            for s_i in range(c):
                cap = jnp.where(row >= s_i, 0.0, -jnp.inf)
                e = jnp.exp2(jnp.minimum(b2 - b2[s_i:s_i + 1], cap))
                xs.append((q * e) * k[s_i:s_i + 1])
            a_rep = _dot(jnp.concatenate(xs, axis=0).astype(BF16), head_sum)
            o = a_rep[0:c] * v[0:1]
            for s_i in range(1, c):
                o = o + a_rep[s_i * c:(s_i + 1) * c] * v[s_i:s_i + 1]
            st = st_ref[p]
            o = o + _dot_nt(qt_ref[rs, sl], st.astype(BF16))
            upd = _dot_tn(v.astype(BF16), kh_ref[rs, sl])
            st_ref[p] = st * eb_ref[rs, sl][c - 1:c] + jnp.where(same_head, upd, 0.0)
            o_s_ref[rs, sl] = o
        return carry

    lax.fori_loop(0, n_chunks, chunk, 0, unroll=UNROLL if n_chunks % UNROLL == 0 else 1)

    g_all = padded(g_ref)
    outs = []
    for p in range(n_pairs):
        sl = slice(p * LANES, (p + 1) * LANES)
        o = o_s_ref[:, sl]
        sq_hi, sq_mid, _ = _split3(o * o)
        ms = (_dot(sq_hi, head_sum) + _dot(sq_mid, head_sum)) * (1.0 / HEAD_DIM)
        outs.append(o * lax.rsqrt(ms + NORM_EPS) * nw_ref[:, sl] * _silu(g_all[:, sl]))
    o_ref[0] = jnp.concatenate(outs, axis=-1)[:rows_in]

    @pl.when(pl.program_id(1) == pl.num_programs(1) - 1)
    def _():
        sout_ref[0] = st_ref[...]


def _pair_state(s):
    b, h, kd, vd = s.shape
    st = jnp.swapaxes(s, 2, 3).reshape(b, h // 2, 2, vd, kd)
    z = jnp.zeros_like(st[:, :, 0])
    top = jnp.concatenate([st[:, :, 0], z], axis=-1)
    bot = jnp.concatenate([z, st[:, :, 1]], axis=-1)
    return jnp.concatenate([top, bot], axis=-2)


def _unpair_state(sp):
    b, hp = sp.shape[:2]
    s0 = sp[:, :, :HEAD_DIM, :HEAD_DIM]
    s1 = sp[:, :, HEAD_DIM:, HEAD_DIM:]
    st = jnp.stack([s0, s1], axis=2).reshape(b, 2 * hp, HEAD_DIM, HEAD_DIM)
    return jnp.swapaxes(st, 2, 3)


def _hgrn(qr, fr, ir, gr, s0, lb_logits, norm_w, layer, batch, seq):
    d_rec = qr.shape[-1]
    n_pairs = d_rec // LANES
    tb = min(seq, ROW_TILE)
    assert seq % tb == 0 and (tb % REC_CHUNK == 0 or tb == seq < REC_CHUNK)
    n_chunks = max(tb // REC_CHUNK, 1)
    rows = n_chunks * REC_CHUNK
    view = lambda a: a.reshape(batch, seq, d_rec)
    tok = pl.BlockSpec((1, tb, d_rec), lambda b, t: (b, t, 0))
    state = pl.BlockSpec((1, n_pairs, LANES, LANES), lambda b, t: (b, 0, 0, 0))
    o, s_new = pl.pallas_call(
        functools.partial(_hgrn_body, layer=layer, n_chunks=n_chunks),
        grid=(batch, seq // tb),
        in_specs=[pl.BlockSpec(lb_logits.shape, lambda b, t: (0, 0)),
                  pl.BlockSpec((1, d_rec), lambda b, t: (0, 0)),
                  tok, tok, tok, tok, state],
        out_specs=[tok, state],
        out_shape=[jax.ShapeDtypeStruct((batch, seq, d_rec), F32),
                   jax.ShapeDtypeStruct((batch, n_pairs, LANES, LANES), F32)],
        scratch_shapes=[pltpu.VMEM((n_pairs, LANES, LANES), F32)]
        + [pltpu.VMEM((rows, d_rec), F32)] * 5 + [pltpu.VMEM((rows, d_rec), BF16)] * 2
        + [pltpu.VMEM((rows, d_rec), F32)],
        compiler_params=_params(("arbitrary", "arbitrary")),
        name="hgrn",
    )(lb_logits, norm_w.reshape(1, d_rec), view(qr), view(fr), view(ir), view(gr), _pair_state(s0))
    return o.reshape(batch * seq, d_rec), _unpair_state(s_new)


def _hgrn_step_body(lblt_ref, nwt_ref, q_ref, f_ref, i_ref, g_ref, s0_ref, o_ref, sout_ref, *, layer):
    n_t, hd, _ = q_ref.shape
    logits = lblt_ref[...]
    e = jnp.exp(logits - jnp.max(logits, axis=1, keepdims=True))
    sm = e / jnp.sum(e, axis=1, keepdims=True)
    lb = jnp.zeros_like(sm[:, 0:1])
    for j in range(1, layer + 1):
        lb = lb + sm[:, j:j + 1]
    for t in range(n_t):
        z = f_ref[t]
        f_t = lb + (1.0 - lb) * jax.nn.sigmoid(z)
        k_t = (1.0 - lb) * jax.nn.sigmoid(-z)
        q_t, v_t = q_ref[t], i_ref[t]
        src = s0_ref if t == 0 else sout_ref
        o_t = jnp.zeros(v_t.shape, F32)
        for k in range(hd):
            s_k = f_t[k:k + 1] * src[0, 0, k] + k_t[k:k + 1] * v_t
            sout_ref[0, 0, k] = s_k
            o_t = o_t + s_k * q_t[k:k + 1]
        ms = jnp.mean(o_t * o_t, axis=0, keepdims=True)
        o_ref[t] = o_t * lax.rsqrt(ms + NORM_EPS) * nwt_ref[...] * _silu(g_ref[t])


def _hgrn_step(qr, fr, ir, gr, state_all, lb_logits, norm_w, layer, batch, seq):
    d_rec = qr.shape[-1]
    n_heads = d_rec // HEAD_DIM
    to_lanes = lambda a: jnp.transpose(a.reshape(batch, seq, d_rec), (1, 2, 0))
    tok = pl.BlockSpec((seq, HEAD_DIM, batch), lambda h: (0, h, 0))
    state_in = pl.BlockSpec((1, 1, HEAD_DIM, HEAD_DIM, batch), lambda h: (layer, h, 0, 0, 0))
    state_out = pl.BlockSpec((1, 1, HEAD_DIM, HEAD_DIM, batch), lambda h: (0, h, 0, 0, 0))
    o_t, s_new = pl.pallas_call(
        functools.partial(_hgrn_step_body, layer=layer),
        grid=(n_heads,),
        in_specs=[pl.BlockSpec((HEAD_DIM, lb_logits.shape[0]), lambda h: (h, 0)),
                  pl.BlockSpec((HEAD_DIM, 1), lambda h: (h, 0)),
                  tok, tok, tok, tok, state_in],
        out_specs=[tok, state_out],
        out_shape=[jax.ShapeDtypeStruct((seq, d_rec, batch), F32),
                   jax.ShapeDtypeStruct((1, n_heads, HEAD_DIM, HEAD_DIM, batch), F32)],
        compiler_params=_params(("arbitrary",)),
        name="hgrn_step",
    )(lb_logits.T, norm_w.reshape(d_rec, 1), to_lanes(qr), to_lanes(fr), to_lanes(ir), to_lanes(gr),
      jnp.transpose(state_all, (0, 2, 3, 4, 1)))
    o = jnp.transpose(o_t, (2, 0, 1)).reshape(batch * seq, d_rec)
    return o, jnp.transpose(s_new[0], (3, 0, 1, 2))


CONV_PAD = 8


def _conv_body(bc_ref, cc_ref, xc_ref, buf_ref, w_ref, b_ref, oc_ref, new_ref, up_ref):
    seq = cc_ref.shape[1]
    width = w_ref.shape[0]
    u = cc_ref[0] * xc_ref[0]
    up_ref[pl.ds(CONV_PAD, seq), :] = u
    up_ref[pl.ds(CONV_PAD - (width - 1), width - 1), :] = buf_ref[0]
    y = u * w_ref[width - 1:width, :] + b_ref[...]
    for i in range(width - 1):
        y = y + up_ref[pl.ds(CONV_PAD - (width - 1) + i, seq), :] * w_ref[i:i + 1, :]
    oc_ref[0] = bc_ref[0] * y
    new_ref[0] = up_ref[pl.ds(CONV_PAD + seq - (width - 1), width - 1), :]


def _short_conv(bc, cc, xc, buf, conv_w, conv_b, batch, seq):
    d_conv = bc.shape[-1]
    width = conv_w.shape[0]
    view = lambda a: a.reshape(batch, seq, d_conv)
    tok = pl.BlockSpec((1, seq, d_conv), lambda b: (b, 0, 0))
    tail = pl.BlockSpec((1, width - 1, d_conv), lambda b: (b, 0, 0))
    oc, new = pl.pallas_call(
        _conv_body,
        grid=(batch,),
        in_specs=[tok, tok, tok, tail,
                  pl.BlockSpec((width, d_conv), lambda b: (0, 0)),
                  pl.BlockSpec((1, d_conv), lambda b: (0, 0))],
        out_specs=[tok, tail],
        out_shape=[jax.ShapeDtypeStruct((batch, seq, d_conv), F32),
                   jax.ShapeDtypeStruct((batch, width - 1, d_conv), F32)],
        scratch_shapes=[pltpu.VMEM((CONV_PAD + seq, d_conv), F32)],
        compiler_params=_params(("arbitrary",)),
        name="short_conv",
    )(view(bc), view(cc), view(xc), buf, conv_w, conv_b.reshape(1, d_conv))
    return oc.reshape(batch * seq, d_conv), new


def _mix_ffn_body(oa_ref, or_ref, oc_ref, x_ref, g1_ref, sh_ref, sc_ref, g2_ref, w_ref, wg_ref, wu_ref, wo_ref,
                  fw_ref, o_ref, x_s_ref, h_s_ref, acc_ref, *, bb, tt, final):
    j = pl.program_id(1)
    d = x_ref.shape[-1]

    @pl.when(j == 0)
    def _():
        n_pairs = oa_ref.shape[0]
        d_att, d_rec = n_pairs * LANES, or_ref.shape[-1]
        mix = (_dot(or_ref[...].astype(BF16), w_ref[d_att:d_att + d_rec, :])
               + _dot(oc_ref[...].astype(BF16), w_ref[d_att + d_rec:, :]))
        for p in range(n_pairs):
            mix = mix + _dot(oa_ref[p].astype(BF16), w_ref[p * LANES:(p + 1) * LANES, :])
        x = x_ref[...].reshape(bb, tt, d) + g1_ref[...] * mix.reshape(bb, tt, d)
        x_s_ref[...] = x.reshape(bb * tt, d)
        h = _rms(x) * (1.0 + sc_ref[...]) + sh_ref[...]
        h_s_ref[...] = h.reshape(bb * tt, d).astype(BF16)
        acc_ref[...] = jnp.zeros_like(acc_ref)

    h = h_s_ref[...]
    a = (_silu(_dot(h, wg_ref[...])) * _dot(h, wu_ref[...])).astype(BF16)
    acc_ref[...] += _dot(a, wo_ref[...])

    @pl.when(j == pl.num_programs(1) - 1)
    def _():
        x = x_s_ref[...].reshape(bb, tt, d) + g2_ref[...] * acc_ref[...].reshape(bb, tt, d)
        if final:
            x = _rms(x) * fw_ref[...]
        o_ref[...] = x.reshape(bb * tt, d)


def _mix_ffn(oa3, o_r, oc, x2, g1, shift, scale, g2, w_out_bf, w_ffn_in_bf, w_ffn_out_bf, final_w, batch, seq, final):
    n, d = x2.shape
    d_ff = w_ffn_out_bf.shape[0]
    bb, tt = _token_tile(batch, seq)
    tm = bb * tt
    n_t = seq // tt
    n_f = 2
    tf = d_ff // n_f
    assert tf % LANES == 0
    tok = lambda a: pl.BlockSpec((tm, a.shape[-1]), lambda i, j: (i, 0))
    mod_spec = pl.BlockSpec((bb, 1, d), lambda i, j: (i // n_t, 0, 0))
    return pl.pallas_call(
        functools.partial(_mix_ffn_body, bb=bb, tt=tt, final=final),
        grid=(n // tm, n_f),
        in_specs=[pl.BlockSpec((oa3.shape[0], tm, LANES), lambda i, j: (0, i, 0)), tok(o_r), tok(oc), tok(x2)]
        + [mod_spec] * 4
        + [pl.BlockSpec(w_out_bf.shape, lambda i, j: (0, 0)),
           pl.BlockSpec((d, tf), lambda i, j: (0, j)),
           pl.BlockSpec((d, tf), lambda i, j: (0, j + n_f)),
           pl.BlockSpec((tf, d), lambda i, j: (j, 0)),
           pl.BlockSpec((1, d), lambda i, j: (0, 0))],
        out_specs=pl.BlockSpec((tm, d), lambda i, j: (i, 0)),
        out_shape=jax.ShapeDtypeStruct((n, d), F32),
        scratch_shapes=[pltpu.VMEM((tm, d), F32), pltpu.VMEM((tm, d), BF16), pltpu.VMEM((tm, d), F32)],
        compiler_params=_params(("arbitrary", "arbitrary")),
        name="mix_ffn",
    )(oa3, o_r, oc, x2, g1, shift, scale, g2, w_out_bf, w_ffn_in_bf, w_ffn_in_bf, w_ffn_out_bf,
      final_w.reshape(1, d))


def _kv_layout_body(*refs, depth):
    ins, outs = refs[:2 * depth], refs[2 * depth:]
    for which, o_ref in enumerate(outs):
        for l in range(depth):
            src = ins[which * depth + l]
            for p in range(src.shape[0]):
                o_ref[l, 0, p * LANES:(p + 1) * LANES, :] = src[p].T


def _kv_layout(ks, vs, batch, seq):
    depth = len(ks)
    n_pairs = ks[0].shape[0]
    d_att = n_pairs * LANES
    tt = min(seq, ROW_TILE)
    n_t = seq // tt
    src = pl.BlockSpec((n_pairs, tt, LANES), lambda b, j: (0, b * n_t + j, 0))
    dst = pl.BlockSpec((depth, 1, d_att, tt), lambda b, j: (0, b, 0, j))
    shape = jax.ShapeDtypeStruct((depth, batch, d_att, seq), F32)
    k_t, v_t = pl.pallas_call(
        functools.partial(_kv_layout_body, depth=depth),
        grid=(batch, n_t),
        in_specs=[src] * (2 * depth),
        out_specs=[dst, dst],
        out_shape=[shape, shape],
        compiler_params=_params(("arbitrary", "arbitrary")),
        name="kv_layout",
    )(*ks, *vs)
    heads = lambda a: jnp.transpose(a.reshape(depth, batch, d_att // HEAD_DIM, HEAD_DIM, seq), (0, 1, 4, 2, 3))
    return heads(k_t), heads(v_t)


def _trunk(x, mods, pos, cache_k, cache_v, state_hgrn, state_conv, weights):
    (w_in, conv_w, conv_b, lb_logits, hgrn_norm_w, w_out, w_ffn_in, w_ffn_out, final_norm_w) = weights
    batch, seq, d = x.shape
    depth = w_in.shape[0]
    d_conv = conv_w.shape[-1]
    d_att = d_rec = (w_in.shape[-1] - 3 * d_conv) // 7
    n_heads = d_att // HEAD_DIM
    widths = (d_att,) * 3 + (d_rec,) * 4 + (d_conv,) * 3
    bb, tt = _token_tile(batch, seq)
    cos_t, sin_t = _rope_tables(pos, n_heads)
    if bb > 1:
        cos_t, sin_t = jnp.tile(cos_t, (bb, 1)), jnp.tile(sin_t, (bb, 1))
    x2 = x.reshape(batch * seq, d)
    ks, vs, hs, cs = [], [], [], []
    for l in range(depth):
        sh1, sc1, g1, sh2, sc2, g2 = [m.reshape(batch, 1, d) for m in jnp.split(mods[l], 6, axis=-1)]
        qa, ka, va, qr, fr, ir, gr, bc, cc, xc = _in_proj(
            x2, sh1, sc1, w_in[l], cos_t, sin_t, batch, seq, widths, 3)
        if cache_k is None:
            oa = _prompt_attention(qa, ka, va, batch, seq)
            s0 = jnp.zeros((batch, n_heads, HEAD_DIM, HEAD_DIM), F32)
            conv0 = jnp.zeros((batch, conv_w.shape[1] - 1, d_conv), F32)
        else:
            oa = _cache_attention(qa, ka, va, cache_k, cache_v, l, batch, seq)
            conv0 = state_conv[l]
        if cache_k is None:
            o_r, s_new = _hgrn(qr, fr, ir, gr, s0, lb_logits, hgrn_norm_w[l], l, batch, seq)
        else:
            o_r, s_new = _hgrn_step(qr, fr, ir, gr, state_hgrn, lb_logits, hgrn_norm_w[l], l, batch, seq)
        oc, conv_new = _short_conv(bc, cc, xc, conv0, conv_w[l], conv_b[l], batch, seq)
        x2 = _mix_ffn(oa, o_r, oc, x2, g1, sh2, sc2, g2, w_out[l], w_ffn_in[l], w_ffn_out[l], final_norm_w,
                      batch, seq, l == depth - 1)
        ks.append(ka)
        vs.append(va)
        hs.append(s_new)
        cs.append(conv_new)
    keep = min(WINDOW_STEPS * max(DILATIONS), seq)
    k_out, v_out = _kv_layout(ks, vs, batch, seq)
    return (x2.reshape(batch, seq, d), k_out[:, :, seq - keep:], v_out[:, :, seq - keep:],
            jnp.stack(hs), jnp.stack(cs))


def kernel(x_prompt, x_sample, cache_k, cache_v, state_hgrn, state_conv, c_prompt, c_sample,
           w_ada, b_ada, w_in, conv_w, conv_b, hgrn_lb_logits, hgrn_norm_w, w_out,
           w_ffn_in, w_ffn_out, final_norm_w):
    n_prompt = c_prompt.shape[0]
    past_len = PAST_LEN
    assert cache_k.shape[2] == min(WINDOW_STEPS * max(DILATIONS), PAST_LEN) == PAST_LEN
    mods = _ada(jnp.concatenate([c_prompt, c_sample], axis=0), w_ada, b_ada)
    weights = (w_in.astype(BF16), conv_w, conv_b, hgrn_lb_logits.astype(F32), hgrn_norm_w,
               w_out.astype(BF16), w_ffn_in.astype(BF16), w_ffn_out.astype(BF16), final_norm_w)
    pos_p = jnp.arange(x_prompt.shape[1], dtype=jnp.int32)
    pos_s = past_len + jnp.arange(x_sample.shape[1], dtype=jnp.int32)
    out_p = _trunk(x_prompt, mods[:, :n_prompt], pos_p, None, None, None, None, weights)
    out_s = _trunk(x_sample, mods[:, n_prompt:], pos_s, cache_k, cache_v, state_hgrn, state_conv, weights)
    return (out_p[0], out_s[0]) + out_p[1:] + out_s[1:]
```

```python
import functools
import math

import jax
import jax.numpy as jnp
from jax import lax
from jax.experimental import pallas as pl
from jax.experimental.pallas import tpu as pltpu

F32 = jnp.float32
BF16 = jnp.bfloat16

HEAD_DIM = 64
LANES = 128
DILATIONS = (1, 4, 16)
WINDOW_STEPS = 128
PAST_LEN = 2048
ROPE_THETA = 10000.0
NORM_EPS = 1e-6
REC_CHUNK = 16
UNROLL = 4
LOG2_E = 1.4426950408889634
ROW_TILE = 512
MXU_WIDTH = 256
FFN_COLS = 1536
CACHE_BATCH = 2
VMEM_LIMIT = 48 * 1024 * 1024


def _params(sem):
    return pltpu.CompilerParams(dimension_semantics=sem, vmem_limit_bytes=VMEM_LIMIT)


def _dot(a, b):
    return jnp.dot(a, b, preferred_element_type=F32)


def _dot_nt(a, b):
    return lax.dot_general(a, b, (((1,), (1,)), ((), ())), preferred_element_type=F32)


def _dot_tn(a, b):
    return lax.dot_general(a, b, (((0,), (0,)), ((), ())), preferred_element_type=F32)


def _split3(x):
    hi = x.astype(BF16)
    r = x - hi.astype(F32)
    mid = r.astype(BF16)
    lo = (r - mid.astype(F32)).astype(BF16)
    return hi, mid, lo


def _silu(x):
    return x * jax.nn.sigmoid(x)


def _rms(x):
    return x * lax.rsqrt(jnp.mean(x * x, axis=-1, keepdims=True) + NORM_EPS)


def _token_tile(batch, seq):
    if seq >= ROW_TILE:
        assert seq % ROW_TILE == 0
        return 1, ROW_TILE
    assert ROW_TILE % seq == 0 and seq % 8 == 0
    bb = min(batch, ROW_TILE // seq)
    assert batch % bb == 0
    return bb, seq


def _ada_body(c_ref, w_ref, b_ref, o_ref):
    s = _silu(c_ref[...])
    s_hi, s_mid, _ = _split3(s)
    w_hi, w_mid, _ = _split3(w_ref[0])
    o_ref[0] = _dot(s_hi, w_hi) + _dot(s_hi, w_mid) + _dot(s_mid, w_hi) + b_ref[0]


def _ada(c_all, w_ada, b_ada):
    depth, d, n6 = w_ada.shape
    bc = c_all.shape[0]
    tn = 512
    return pl.pallas_call(
        _ada_body,
        grid=(depth, n6 // tn),
        in_specs=[pl.BlockSpec((bc, d), lambda l, j: (0, 0)),
                  pl.BlockSpec((1, d, tn), lambda l, j: (l, 0, j)),
                  pl.BlockSpec((1, 1, tn), lambda l, j: (l, 0, j))],
        out_specs=pl.BlockSpec((1, bc, tn), lambda l, j: (l, 0, j)),
        out_shape=jax.ShapeDtypeStruct((depth, bc, n6), F32),
        compiler_params=_params(("arbitrary", "arbitrary")),
        name="ada",
    )(c_all, w_ada, b_ada.reshape(depth, 1, n6))


def _rope_tables(pos, n_heads):
    half = HEAD_DIM // 2
    freqs = ROPE_THETA ** (-jnp.arange(half, dtype=F32) / half)
    ang = pos.astype(F32)[:, None] * freqs[None, :]
    cos, sin = jnp.cos(ang), jnp.sin(ang)
    cos_t = jnp.tile(jnp.concatenate([cos, cos], axis=-1), (1, n_heads))
    sin_t = jnp.tile(jnp.concatenate([-sin, sin], axis=-1), (1, n_heads))
    return cos_t, sin_t


def _rope(x, cos, sin_signed):
    outs = []
    for c in range(x.shape[-1] // LANES):
        sl = slice(c * LANES, (c + 1) * LANES)
        xc = x[:, sl]
        lane = lax.broadcasted_iota(jnp.int32, xc.shape, 1)
        ahead = pltpu.roll(xc, LANES - HEAD_DIM // 2, axis=1)
        behind = pltpu.roll(xc, HEAD_DIM // 2, axis=1)
        rot = jnp.where((lane & (HEAD_DIM // 2)) == 0, ahead, behind)
        outs.append(xc * cos[:, sl] + rot * sin_signed[:, sl])
    return jnp.concatenate(outs, axis=-1)


def _in_proj_body(x_ref, sh_ref, sc_ref, w_ref, cos_ref, sin_ref, *out_refs, bb, tt, widths, n_paired, runs):
    d = x_ref.shape[-1]
    x = x_ref[...].reshape(bb, tt, d)
    h = _rms(x) * (1.0 + sc_ref[...]) + sh_ref[...]
    hb = h.reshape(bb * tt, d).astype(BF16)
    for first_out, n_out in runs:
        off, wd = sum(widths[:first_out]), widths[first_out]
        group = _dot(hb, w_ref[:, off:off + wd * n_out])
        for n in range(n_out):
            i = first_out + n
            p = group[:, n * wd:(n + 1) * wd]
            if i < 2:
                p = _rope(p, cos_ref[...], sin_ref[...])
            if i < n_paired:
                for c in range(wd // LANES):
                    out_refs[i][c] = p[:, c * LANES:(c + 1) * LANES]
            else:
                out_refs[i][...] = p


def _equal_runs(widths):
    runs = []
    for i, wd in enumerate(widths):
        if runs and widths[runs[-1][0]] == wd:
            runs[-1][1] += 1
        else:
            runs.append([i, 1])
    return tuple((a, n) for a, n in runs)


def _in_proj(x2, shift, scale, w_in_bf, cos_t, sin_t, batch, seq, widths, n_paired):
    n, d = x2.shape
    bb, tt = _token_tile(batch, seq)
    tm = bb * tt
    n_t = seq // tt
    table_blocks = cos_t.shape[0] // tm
    body = functools.partial(_in_proj_body, bb=bb, tt=tt, widths=widths, n_paired=n_paired, runs=_equal_runs(widths))
    mod_spec = pl.BlockSpec((bb, 1, d), lambda i: (i // n_t, 0, 0))
    tab_spec = pl.BlockSpec((tm, cos_t.shape[1]), lambda i: (i % table_blocks, 0))
    out_specs, out_shape = [], []
    for idx, wd in enumerate(widths):
        if idx < n_paired:
            out_specs.append(pl.BlockSpec((wd // LANES, tm, LANES), lambda i: (0, i, 0)))
            out_shape.append(jax.ShapeDtypeStruct((wd // LANES, n, LANES), F32))
        else:
            out_specs.append(pl.BlockSpec((tm, wd), lambda i: (i, 0)))
            out_shape.append(jax.ShapeDtypeStruct((n, wd), F32))
    return pl.pallas_call(
        body,
        grid=(n // tm,),
        in_specs=[pl.BlockSpec((tm, d), lambda i: (i, 0)), mod_spec, mod_spec,
                  pl.BlockSpec(w_in_bf.shape, lambda i: (0, 0)), tab_spec, tab_spec],
        out_specs=out_specs,
        out_shape=out_shape,
        compiler_params=_params(("arbitrary",)),
        name="in_proj",
    )(x2, shift, scale, w_in_bf, cos_t, sin_t)


def _prompt_attn_body(q_ref, k_ref, v_ref, o_ref, m_ref, l_ref, acc_ref):
    n_pairs, seq, _ = q_ref.shape
    blk = WINDOW_STEPS
    lane = lax.broadcasted_iota(jnp.int32, (blk, LANES), 1)
    first = lane < HEAD_DIM
    scale = HEAD_DIM ** -0.5 * LOG2_E

    def rows(start, dil):
        return pl.ds(start, blk) if dil == 1 else pl.ds(start, blk, stride=dil)

    for b_idx, dil in enumerate(DILATIONS):
        n_blk = seq // dil // blk
        has_prev = n_blk > 1
        n_keys = 2 * blk if has_prev else blk
        q_row = lax.broadcasted_iota(jnp.int32, (2 * blk, n_keys), 0) & (blk - 1)
        k_col = lax.broadcasted_iota(jnp.int32, (2 * blk, n_keys), 1)
        is_first, is_last = b_idx == 0, b_idx == len(DILATIONS) - 1

        def unit(u, carry, dil=dil, has_prev=has_prev, q_row=q_row, k_col=k_col,
                 is_first=is_first, is_last=is_last):
            res = u & (dil - 1)
            i = u >> (dil.bit_length() - 1)
            start = res + dil * blk * i
            if dil == 1:
                start = pl.multiple_of(start, blk)
            if has_prev:
                prev_start = jnp.maximum(start - dil * blk, res)
                lo = jnp.maximum(q_row, jnp.where(i > 0, 0, blk))
                valid = (k_col >= lo) & (k_col <= q_row + blk)
            else:
                valid = k_col <= q_row
            loaded = []
            for p in range(n_pairs):
                q = q_ref[p, rows(start, dil), :] * scale
                kc = k_ref[p, rows(start, dil), :]
                vc = v_ref[p, rows(start, dil), :]
                if has_prev:
                    kc = jnp.concatenate([k_ref[p, rows(prev_start, dil), :], kc], axis=0)
                    vc = jnp.concatenate([v_ref[p, rows(prev_start, dil), :], vc], axis=0)
                old = None
                if not is_first:
                    old = (m_ref[p, rows(start, dil), :], l_ref[p, rows(start, dil), :],
                           acc_ref[p, rows(start, dil), :])
                loaded.append((q, kc, vc, old))
            scores = []
            for q, kc, vc, old in loaded:
                q2 = jnp.concatenate([jnp.where(first, q, 0.0), jnp.where(first, 0.0, q)], axis=0)
                scores.append(_dot_nt(q2.astype(BF16), kc.astype(BF16)))
            results = []
            for (q, kc, vc, old), s in zip(loaded, scores):
                s = jnp.where(valid, s, -jnp.inf)
                m2 = jnp.max(s, axis=-1, keepdims=True)
                e = jnp.exp2(s - m2).astype(BF16)
                v1 = jnp.concatenate([vc.astype(BF16), jnp.ones(vc.shape, BF16)], axis=1)
                pv = _dot(e, v1)
                m_b = jnp.where(first, m2[:blk], m2[blk:])
                l_b = jnp.where(first, pv[:blk, LANES:], pv[blk:, LANES:])
                acc_b = jnp.where(first, pv[:blk, :LANES], pv[blk:, :LANES])
                if not is_first:
                    m_o, l_o, acc_o = old
                    m_n = jnp.maximum(m_o, m_b)
                    w_o, w_b = jnp.exp2(m_o - m_n), jnp.exp2(m_b - m_n)
                    l_b = w_o * l_o + w_b * l_b
                    acc_b = w_o * acc_o + w_b * acc_b
                    m_b = m_n
                results.append((m_b, l_b, acc_b))
            for p, (m_b, l_b, acc_b) in enumerate(results):
                if is_last:
                    o_ref[p, rows(start, dil), :] = acc_b / l_b
                else:
                    m_ref[p, rows(start, dil), :] = m_b
                    l_ref[p, rows(start, dil), :] = l_b
                    acc_ref[p, rows(start, dil), :] = acc_b
            return carry

        lax.fori_loop(0, dil * n_blk, unit, 0, unroll=2)


def _prompt_attention(q3, k3, v3, batch, seq):
    n_pairs = q3.shape[0]
    assert seq % (WINDOW_STEPS * max(DILATIONS)) == 0
    spec = pl.BlockSpec((n_pairs, seq, LANES), lambda b: (0, b, 0))
    return pl.pallas_call(
        _prompt_attn_body,
        grid=(batch,),
        in_specs=[spec, spec, spec],
        out_specs=spec,
        out_shape=jax.ShapeDtypeStruct(q3.shape, F32),
        scratch_shapes=[pltpu.VMEM((n_pairs, seq, LANES), F32)] * 3,
        compiler_params=_params(("arbitrary",)),
        name="prompt_attn",
    )(q3, k3, v3)


def _branch_count(delta):
    cnt = jnp.zeros(delta.shape, F32)
    for dil in DILATIONS:
        hit = (delta >= 0) & (delta <= WINDOW_STEPS * dil) & ((delta & (dil - 1)) == 0)
        cnt = cnt + jnp.where(hit, 1.0, 0.0)
    return cnt


def _cache_attn_body(q_ref, kn_ref, vn_ref, kt_ref, vt_ref, o_ref, *, past, t_new):
    n_pairs = q_ref.shape[0]
    t_c = lax.broadcasted_iota(jnp.int32, (t_new, past), 0)
    j_c = lax.broadcasted_iota(jnp.int32, (t_new, past), 1)
    cnt_c = _branch_count(past + t_c - j_c)
    t_n = lax.broadcasted_iota(jnp.int32, (t_new, t_new), 0)
    j_n = lax.broadcasted_iota(jnp.int32, (t_new, t_new), 1)
    cnt_n = _branch_count(t_n - j_n)
    scale = HEAD_DIM ** -0.5
    for bi in range(kt_ref.shape[1]):
        rs = slice(bi * t_new, (bi + 1) * t_new)
        for p in range(n_pairs):
            q_pair, kn_pair, vn_pair = q_ref[p, rs, :] * scale, kn_ref[p, rs, :], vn_ref[p, rs, :]
            outs = []
            for hh in range(LANES // HEAD_DIM):
                h = p * (LANES // HEAD_DIM) + hh
                sl = slice(hh * HEAD_DIM, (hh + 1) * HEAD_DIM)
                q = q_pair[:, sl]
                s_c = jnp.where(cnt_c > 0, _dot(q.astype(BF16), kt_ref[0, bi, h].astype(BF16)), -jnp.inf)
                s_n = jnp.where(cnt_n > 0, _dot_nt(q, kn_pair[:, sl]), -jnp.inf)
                m = jnp.maximum(jnp.max(s_c, axis=-1, keepdims=True), jnp.max(s_n, axis=-1, keepdims=True))
                p_c = cnt_c * jnp.exp(s_c - m)
                p_n = cnt_n * jnp.exp(s_n - m)
                den = jnp.sum(p_c, axis=-1, keepdims=True) + jnp.sum(p_n, axis=-1, keepdims=True)
                acc = _dot_nt(p_c.astype(BF16), vt_ref[0, bi, h].astype(BF16)) + _dot(p_n, vn_pair[:, sl])
                outs.append(acc / den)
            o_ref[p, rs, :] = jnp.concatenate(outs, axis=-1)


def _cache_attention(q3, kn3, vn3, cache_k, cache_v, layer, batch, seq):
    depth, _, past, n_heads, hd = cache_k.shape
    n_pairs = q3.shape[0]
    bb = CACHE_BATCH if batch % CACHE_BATCH == 0 else 1
    by_head = lambda a: jnp.transpose(a, (0, 1, 3, 4, 2))
    new = pl.BlockSpec((n_pairs, bb * seq, LANES), lambda b: (0, b, 0))
    old = pl.BlockSpec((1, bb, n_heads, hd, past), lambda b: (layer, b, 0, 0, 0))
    return pl.pallas_call(
        functools.partial(_cache_attn_body, past=past, t_new=seq),
        grid=(batch // bb,),
        in_specs=[new, new, new, old, old],
        out_specs=new,
        out_shape=jax.ShapeDtypeStruct(q3.shape, F32),
        compiler_params=_params(("arbitrary",)),
        name="cache_attn",
    )(q3, kn3, vn3, by_head(cache_k), by_head(cache_v))


def _lower_bound(logits, layer):
    e = jnp.exp(logits - jnp.max(logits, axis=0, keepdims=True))
    sm = e / jnp.sum(e, axis=0, keepdims=True)
    lb = jnp.zeros_like(sm[0:1])
    for j in range(1, layer + 1):
        lb = lb + sm[j:j + 1]
    return lb


def _hgrn_body(lbl_ref, nw_ref, q_ref, f_ref, i_ref, g_ref, s0_ref, o_ref, sout_ref,
               st_ref, b2_ref, kk_ref, eb_ref, q_s_ref, v_s_ref, qt_ref, kh_ref, o_s_ref,
               *, layer, n_chunks):
    c = REC_CHUNK
    d_rec = q_ref.shape[-1]
    n_pairs = d_rec // LANES
    rows_in = q_ref.shape[1]
    rows = n_chunks * c
    grp = min(rows, LANES)

    @pl.when(pl.program_id(1) == 0)
    def _():
        st_ref[...] = s0_ref[0]

    def padded(ref):
        x = ref[0]
        if rows_in < rows:
            x = jnp.concatenate([x, jnp.zeros((rows - rows_in, d_rec), F32)], axis=0)
        return x

    lb = _lower_bound(lbl_ref[...], layer)
    log_lb = jnp.log(lb)
    z = padded(f_ref)
    log_sig = jnp.minimum(z, 0.0) - jnp.log1p(jnp.exp(-jnp.abs(z)))
    b_term = jnp.log1p(-lb) + log_sig
    log_f = jnp.maximum(log_lb, b_term) + jnp.log1p(jnp.exp(-jnp.abs(log_lb - b_term)))
    kk = (1.0 - lb) * jax.nn.sigmoid(-z)
    if rows_in < rows:
        live = lax.broadcasted_iota(jnp.int32, (rows, d_rec), 0) < rows_in
        log_f = jnp.where(live, log_f, 0.0)
        kk = jnp.where(live, kk, 0.0)
    r_i = lax.broadcasted_iota(jnp.int32, (grp, grp), 0)
    c_i = lax.broadcasted_iota(jnp.int32, (grp, grp), 1)
    same_chunk = (r_i >> (c.bit_length() - 1)) == (c_i >> (c.bit_length() - 1))
    prefix = jnp.where(same_chunk & (c_i <= r_i), 1.0, 0.0).astype(BF16)
    suffix = jnp.where(same_chunk & (c_i > r_i), 1.0, 0.0).astype(BF16)
    q_all = padded(q_ref)
    for g in range(rows // grp):
        gs = slice(g * grp, (g + 1) * grp)
        parts = _split3(log_f[gs])
        b = sum(_dot(prefix, part) for part in parts)
        r = sum(_dot(suffix, part) for part in parts)
        eb = jnp.exp(b)
        b2_ref[gs, :] = b * LOG2_E
        eb_ref[gs, :] = eb
        qt_ref[gs, :] = (q_all[gs] * eb).astype(BF16)
        kh_ref[gs, :] = (kk[gs] * jnp.exp(r)).astype(BF16)
    kk_ref[...] = kk
    q_s_ref[...] = q_all
    v_s_ref[...] = padded(i_ref)

    l_r = lax.broadcasted_iota(jnp.int32, (LANES, LANES), 0) >> (HEAD_DIM.bit_length() - 1)
    l_c = lax.broadcasted_iota(jnp.int32, (LANES, LANES), 1) >> (HEAD_DIM.bit_length() - 1)
    same_head = l_r == l_c
    head_sum = jnp.where(same_head, 1.0, 0.0).astype(BF16)
    row = lax.broadcasted_iota(jnp.int32, (c, LANES), 0)

    def chunk(ci, carry):
        r0 = pl.multiple_of(ci * c, c)
        rs = pl.ds(r0, c)
        for p in range(n_pairs):
            sl = slice(p * LANES, (p + 1) * LANES)
            b2, q, v, k = b2_ref[rs, sl], q_s_ref[rs, sl], v_s_ref[rs, sl], kk_ref[rs, sl]
            xs = []
            for s_i in range(c):
                cap = jnp.where(row >= s_i, 0.0, -jnp.inf)
                e = jnp.exp2(jnp.minimum(b2 - b2[s_i:s_i + 1], cap))
                xs.append((q * e) * k[s_i:s_i + 1])
            a_rep = _dot(jnp.concatenate(xs, axis=0).astype(BF16), head_sum)
            o = a_rep[0:c] * v[0:1]
            for s_i in range(1, c):
                o = o + a_rep[s_i * c:(s_i + 1) * c] * v[s_i:s_i + 1]
            st = st_ref[p]
            o = o + _dot_nt(qt_ref[rs, sl], st.astype(BF16))
            upd = _dot_tn(v.astype(BF16), kh_ref[rs, sl])
            st_ref[p] = st * eb_ref[rs, sl][c - 1:c] + jnp.where(same_head, upd, 0.0)
            o_s_ref[rs, sl] = o
        return carry

    lax.fori_loop(0, n_chunks, chunk, 0, unroll=UNROLL if n_chunks % UNROLL == 0 else 1)

    g_all = padded(g_ref)
    outs = []
    for p in range(n_pairs):
        sl = slice(p * LANES, (p + 1) * LANES)
        o = o_s_ref[:, sl]
        sq_hi, sq_mid, _ = _split3(o * o)
        ms = (_dot(sq_hi, head_sum) + _dot(sq_mid, head_sum)) * (1.0 / HEAD_DIM)
        outs.append(o * lax.rsqrt(ms + NORM_EPS) * nw_ref[:, sl] * _silu(g_all[:, sl]))
    o_ref[0] = jnp.concatenate(outs, axis=-1)[:rows_in]

    @pl.when(pl.program_id(1) == pl.num_programs(1) - 1)
    def _():
        sout_ref[0] = st_ref[...]


def _pair_state(s):
    b, h, kd, vd = s.shape
    st = jnp.swapaxes(s, 2, 3).reshape(b, h // 2, 2, vd, kd)
    z = jnp.zeros_like(st[:, :, 0])
    top = jnp.concatenate([st[:, :, 0], z], axis=-1)
    bot = jnp.concatenate([z, st[:, :, 1]], axis=-1)
    return jnp.concatenate([top, bot], axis=-2)


def _unpair_state(sp):
    b, hp = sp.shape[:2]
    s0 = sp[:, :, :HEAD_DIM, :HEAD_DIM]
    s1 = sp[:, :, HEAD_DIM:, HEAD_DIM:]
    st = jnp.stack([s0, s1], axis=2).reshape(b, 2 * hp, HEAD_DIM, HEAD_DIM)
    return jnp.swapaxes(st, 2, 3)


def _hgrn(qr, fr, ir, gr, s0, lb_logits, norm_w, layer, batch, seq):
    d_rec = qr.shape[-1]
    n_pairs = d_rec // LANES
    tb = min(seq, ROW_TILE)
    assert seq % tb == 0 and (tb % REC_CHUNK == 0 or tb == seq < REC_CHUNK)
    n_chunks = max(tb // REC_CHUNK, 1)
    rows = n_chunks * REC_CHUNK
    view = lambda a: a.reshape(batch, seq, d_rec)
    tok = pl.BlockSpec((1, tb, d_rec), lambda b, t: (b, t, 0))
    state = pl.BlockSpec((1, n_pairs, LANES, LANES), lambda b, t: (b, 0, 0, 0))
    o, s_new = pl.pallas_call(
        functools.partial(_hgrn_body, layer=layer, n_chunks=n_chunks),
        grid=(batch, seq // tb),
        in_specs=[pl.BlockSpec(lb_logits.shape, lambda b, t: (0, 0)),
                  pl.BlockSpec((1, d_rec), lambda b, t: (0, 0)),
                  tok, tok, tok, tok, state],
        out_specs=[tok, state],
        out_shape=[jax.ShapeDtypeStruct((batch, seq, d_rec), F32),
                   jax.ShapeDtypeStruct((batch, n_pairs, LANES, LANES), F32)],
        scratch_shapes=[pltpu.VMEM((n_pairs, LANES, LANES), F32)]
        + [pltpu.VMEM((rows, d_rec), F32)] * 5 + [pltpu.VMEM((rows, d_rec), BF16)] * 2
        + [pltpu.VMEM((rows, d_rec), F32)],
        compiler_params=_params(("arbitrary", "arbitrary")),
        name="hgrn",
    )(lb_logits, norm_w.reshape(1, d_rec), view(qr), view(fr), view(ir), view(gr), _pair_state(s0))
    return o.reshape(batch * seq, d_rec), _unpair_state(s_new)


def _hgrn_step_body(lblt_ref, nwt_ref, q_ref, f_ref, i_ref, g_ref, s0_ref, o_ref, sout_ref, *, layer):
    n_t, hd, _ = q_ref.shape
    logits = lblt_ref[...]
    e = jnp.exp(logits - jnp.max(logits, axis=1, keepdims=True))
    sm = e / jnp.sum(e, axis=1, keepdims=True)
    lb = jnp.zeros_like(sm[:, 0:1])
    for j in range(1, layer + 1):
        lb = lb + sm[:, j:j + 1]
    for t in range(n_t):
        z = f_ref[t]
        f_t = lb + (1.0 - lb) * jax.nn.sigmoid(z)
        k_t = (1.0 - lb) * jax.nn.sigmoid(-z)
        q_t, v_t = q_ref[t], i_ref[t]
        src = s0_ref if t == 0 else sout_ref
        o_t = jnp.zeros(v_t.shape, F32)
        for k in range(hd):
            s_k = f_t[k:k + 1] * src[0, 0, k] + k_t[k:k + 1] * v_t
            sout_ref[0, 0, k] = s_k
            o_t = o_t + s_k * q_t[k:k + 1]
        ms = jnp.mean(o_t * o_t, axis=0, keepdims=True)
        o_ref[t] = o_t * lax.rsqrt(ms + NORM_EPS) * nwt_ref[...] * _silu(g_ref[t])


def _hgrn_step(qr, fr, ir, gr, state_all, lb_logits, norm_w, layer, batch, seq):
    d_rec = qr.shape[-1]
    n_heads = d_rec // HEAD_DIM
    to_lanes = lambda a: jnp.transpose(a.reshape(batch, seq, d_rec), (1, 2, 0))
    tok = pl.BlockSpec((seq, HEAD_DIM, batch), lambda h: (0, h, 0))
    state_in = pl.BlockSpec((1, 1, HEAD_DIM, HEAD_DIM, batch), lambda h: (layer, h, 0, 0, 0))
    state_out = pl.BlockSpec((1, 1, HEAD_DIM, HEAD_DIM, batch), lambda h: (0, h, 0, 0, 0))
    o_t, s_new = pl.pallas_call(
        functools.partial(_hgrn_step_body, layer=layer),
        grid=(n_heads,),
        in_specs=[pl.BlockSpec((HEAD_DIM, lb_logits.shape[0]), lambda h: (h, 0)),
                  pl.BlockSpec((HEAD_DIM, 1), lambda h: (h, 0)),
                  tok, tok, tok, tok, state_in],
        out_specs=[tok, state_out],
        out_shape=[jax.ShapeDtypeStruct((seq, d_rec, batch), F32),
                   jax.ShapeDtypeStruct((1, n_heads, HEAD_DIM, HEAD_DIM, batch), F32)],
        compiler_params=_params(("arbitrary",)),
        name="hgrn_step",
    )(lb_logits.T, norm_w.reshape(d_rec, 1), to_lanes(qr), to_lanes(fr), to_lanes(ir), to_lanes(gr),
      jnp.transpose(state_all, (0, 2, 3, 4, 1)))
    o = jnp.transpose(o_t, (2, 0, 1)).reshape(batch * seq, d_rec)
    return o, jnp.transpose(s_new[0], (3, 0, 1, 2))


CONV_PAD = 8


def _conv_body(bc_ref, cc_ref, xc_ref, buf_ref, w_ref, b_ref, oc_ref, new_ref, up_ref):
    seq = cc_ref.shape[1]
    width = w_ref.shape[0]
    u = cc_ref[0] * xc_ref[0]
    up_ref[pl.ds(CONV_PAD, seq), :] = u
    up_ref[pl.ds(CONV_PAD - (width - 1), width - 1), :] = buf_ref[0]
    y = u * w_ref[width - 1:width, :] + b_ref[...]
    for i in range(width - 1):
        y = y + up_ref[pl.ds(CONV_PAD - (width - 1) + i, seq), :] * w_ref[i:i + 1, :]
    oc_ref[0] = bc_ref[0] * y
    new_ref[0] = up_ref[pl.ds(CONV_PAD + seq - (width - 1), width - 1), :]


def _short_conv(bc, cc, xc, buf, conv_w, conv_b, batch, seq):
    d_conv = bc.shape[-1]
    width = conv_w.shape[0]
    view = lambda a: a.reshape(batch, seq, d_conv)
    tok = pl.BlockSpec((1, seq, d_conv), lambda b: (b, 0, 0))
    tail = pl.BlockSpec((1, width - 1, d_conv), lambda b: (b, 0, 0))
    oc, new = pl.pallas_call(
        _conv_body,
        grid=(batch,),
        in_specs=[tok, tok, tok, tail,
                  pl.BlockSpec((width, d_conv), lambda b: (0, 0)),
                  pl.BlockSpec((1, d_conv), lambda b: (0, 0))],
        out_specs=[tok, tail],
        out_shape=[jax.ShapeDtypeStruct((batch, seq, d_conv), F32),
                   jax.ShapeDtypeStruct((batch, width - 1, d_conv), F32)],
        scratch_shapes=[pltpu.VMEM((CONV_PAD + seq, d_conv), F32)],
        compiler_params=_params(("arbitrary",)),
        name="short_conv",
    )(view(bc), view(cc), view(xc), buf, conv_w, conv_b.reshape(1, d_conv))
    return oc.reshape(batch * seq, d_conv), new


def _mix_ffn_body(oa_ref, or_ref, oc_ref, x_ref, g1_ref, sh_ref, sc_ref, g2_ref, w_ref, wi_ref, wo_ref,
                  fw_ref, o_ref, *, bb, tt, final, cuts):
    d = x_ref.shape[-1]
    d_ff = wo_ref.shape[0]
    n_pairs = oa_ref.shape[0]
    d_att, d_rec = n_pairs * LANES, or_ref.shape[-1]
    mix = (_dot(or_ref[...].astype(BF16), w_ref[d_att:d_att + d_rec, :])
           + _dot(oc_ref[...].astype(BF16), w_ref[d_att + d_rec:, :]))
    for p in range(n_pairs):
        mix = mix + _dot(oa_ref[p].astype(BF16), w_ref[p * LANES:(p + 1) * LANES, :])
    x = x_ref[...].reshape(bb, tt, d) + g1_ref[...] * mix.reshape(bb, tt, d)
    h = (_rms(x) * (1.0 + sc_ref[...]) + sh_ref[...]).reshape(bb * tt, d).astype(BF16)
    acc = None
    for c0, c1 in zip(cuts[:-1], cuts[1:]):
        a = (_silu(_dot(h, wi_ref[:, c0:c1])) * _dot(h, wi_ref[:, d_ff + c0:d_ff + c1])).astype(BF16)
        part = _dot(a, wo_ref[c0:c1, :])
        acc = part if acc is None else acc + part
    x = x + g2_ref[...] * acc.reshape(bb, tt, d)
    if final:
        x = _rms(x) * fw_ref[...]
    o_ref[...] = x.reshape(bb * tt, d)


def _mix_ffn(oa3, o_r, oc, x2, g1, shift, scale, g2, w_out_bf, w_ffn_in_bf, w_ffn_out_bf, final_w, batch, seq, final):
    n, d = x2.shape
    d_ff = w_ffn_out_bf.shape[0]
    bb, tt = _token_tile(batch, seq)
    tm = bb * tt
    n_t = seq // tt
    assert d_ff % MXU_WIDTH == 0
    n_cut = -(-d_ff // FFN_COLS)
    cuts = tuple(round(i * (d_ff // MXU_WIDTH) / n_cut) * MXU_WIDTH for i in range(n_cut + 1))
    tok = lambda a: pl.BlockSpec((tm, a.shape[-1]), lambda i: (i, 0))
    mod_spec = pl.BlockSpec((bb, 1, d), lambda i: (i // n_t, 0, 0))
    held = lambda a: pl.BlockSpec(a.shape, lambda i: (0,) * a.ndim, pipeline_mode=pl.Buffered(1))
    return pl.pallas_call(
        functools.partial(_mix_ffn_body, bb=bb, tt=tt, final=final, cuts=cuts),
        grid=(n // tm,),
        in_specs=[pl.BlockSpec((oa3.shape[0], tm, LANES), lambda i: (0, i, 0)), tok(o_r), tok(oc), tok(x2)]
        + [mod_spec] * 4
        + [held(w_out_bf), held(w_ffn_in_bf), held(w_ffn_out_bf), pl.BlockSpec((1, d), lambda i: (0, 0))],
        out_specs=pl.BlockSpec((tm, d), lambda i: (i, 0)),
        out_shape=jax.ShapeDtypeStruct((n, d), F32),
        compiler_params=_params(("arbitrary",)),
        name="mix_ffn",
    )(oa3, o_r, oc, x2, g1, shift, scale, g2, w_out_bf, w_ffn_in_bf, w_ffn_out_bf, final_w.reshape(1, d))


def _kv_layout_body(*refs, depth):
    ins, outs = refs[:2 * depth], refs[2 * depth:]
    for which, o_ref in enumerate(outs):
        for l in range(depth):
            src = ins[which * depth + l]
            for p in range(src.shape[0]):
                o_ref[l, 0, p * LANES:(p + 1) * LANES, :] = src[p].T


def _kv_layout(ks, vs, batch, seq):
    depth = len(ks)
    n_pairs = ks[0].shape[0]
    d_att = n_pairs * LANES
    tt = min(seq, ROW_TILE)
    n_t = seq // tt
    src = pl.BlockSpec((n_pairs, tt, LANES), lambda b, j: (0, b * n_t + j, 0))
    dst = pl.BlockSpec((depth, 1, d_att, tt), lambda b, j: (0, b, 0, j))
    shape = jax.ShapeDtypeStruct((depth, batch, d_att, seq), F32)
    k_t, v_t = pl.pallas_call(
        functools.partial(_kv_layout_body, depth=depth),
        grid=(batch, n_t),
        in_specs=[src] * (2 * depth),
        out_specs=[dst, dst],
        out_shape=[shape, shape],
        compiler_params=_params(("arbitrary", "arbitrary")),
        name="kv_layout",
    )(*ks, *vs)
    heads = lambda a: jnp.transpose(a.reshape(depth, batch, d_att // HEAD_DIM, HEAD_DIM, seq), (0, 1, 4, 2, 3))
    return heads(k_t), heads(v_t)


def _trunk(x, mods, pos, cache_k, cache_v, state_hgrn, state_conv, weights):
    (w_in, conv_w, conv_b, lb_logits, hgrn_norm_w, w_out, w_ffn_in, w_ffn_out, final_norm_w) = weights
    batch, seq, d = x.shape
    depth = w_in.shape[0]
    d_conv = conv_w.shape[-1]
    d_att = d_rec = (w_in.shape[-1] - 3 * d_conv) // 7
    n_heads = d_att // HEAD_DIM
    widths = (d_att,) * 3 + (d_rec,) * 4 + (d_conv,) * 3
    bb, tt = _token_tile(batch, seq)
    cos_t, sin_t = _rope_tables(pos, n_heads)
    if bb > 1:
        cos_t, sin_t = jnp.tile(cos_t, (bb, 1)), jnp.tile(sin_t, (bb, 1))
    x2 = x.reshape(batch * seq, d)
    ks, vs, hs, cs = [], [], [], []
    for l in range(depth):
        sh1, sc1, g1, sh2, sc2, g2 = [m.reshape(batch, 1, d) for m in jnp.split(mods[l], 6, axis=-1)]
        qa, ka, va, qr, fr, ir, gr, bc, cc, xc = _in_proj(
            x2, sh1, sc1, w_in[l], cos_t, sin_t, batch, seq, widths, 3)
        if cache_k is None:
            oa = _prompt_attention(qa, ka, va, batch, seq)
            s0 = jnp.zeros((batch, n_heads, HEAD_DIM, HEAD_DIM), F32)
            conv0 = jnp.zeros((batch, conv_w.shape[1] - 1, d_conv), F32)
        else:
            oa = _cache_attention(qa, ka, va, cache_k, cache_v, l, batch, seq)
            conv0 = state_conv[l]
        if cache_k is None:
            o_r, s_new = _hgrn(qr, fr, ir, gr, s0, lb_logits, hgrn_norm_w[l], l, batch, seq)
        else:
            o_r, s_new = _hgrn_step(qr, fr, ir, gr, state_hgrn, lb_logits, hgrn_norm_w[l], l, batch, seq)
        oc, conv_new = _short_conv(bc, cc, xc, conv0, conv_w[l], conv_b[l], batch, seq)
        x2 = _mix_ffn(oa, o_r, oc, x2, g1, sh2, sc2, g2, w_out[l], w_ffn_in[l], w_ffn_out[l], final_norm_w,
                      batch, seq, l == depth - 1)
        ks.append(ka)
        vs.append(va)
        hs.append(s_new)
        cs.append(conv_new)
    keep = min(WINDOW_STEPS * max(DILATIONS), seq)
    if seq % LANES == 0:
        k_out, v_out = _kv_layout(ks, vs, batch, seq)
    else:
        unpair = lambda parts: jnp.transpose(
            jnp.stack(parts).reshape(depth, n_heads // 2, batch, seq, 2, HEAD_DIM),
            (0, 2, 3, 1, 4, 5)).reshape(depth, batch, seq, n_heads, HEAD_DIM)
        k_out, v_out = unpair(ks), unpair(vs)
    return (x2.reshape(batch, seq, d), k_out[:, :, seq - keep:], v_out[:, :, seq - keep:],
            jnp.stack(hs), jnp.stack(cs))


def kernel(x_prompt, x_sample, cache_k, cache_v, state_hgrn, state_conv, c_prompt, c_sample,
           w_ada, b_ada, w_in, conv_w, conv_b, hgrn_lb_logits, hgrn_norm_w, w_out,
           w_ffn_in, w_ffn_out, final_norm_w):
    n_prompt = c_prompt.shape[0]
    past_len = PAST_LEN
    assert cache_k.shape[2] == min(WINDOW_STEPS * max(DILATIONS), PAST_LEN) == PAST_LEN
    mods = _ada(jnp.concatenate([c_prompt, c_sample], axis=0), w_ada, b_ada)
    weights = (w_in.astype(BF16), conv_w, conv_b, hgrn_lb_logits.astype(F32), hgrn_norm_w,
               w_out.astype(BF16), w_ffn_in.astype(BF16), w_ffn_out.astype(BF16), final_norm_w)
    pos_p = jnp.arange(x_prompt.shape[1], dtype=jnp.int32)
    pos_s = past_len + jnp.arange(x_sample.shape[1], dtype=jnp.int32)
    out_p = _trunk(x_prompt, mods[:, :n_prompt], pos_p, None, None, None, None, weights)
    out_s = _trunk(x_sample, mods[:, n_prompt:], pos_s, cache_k, cache_v, state_hgrn, state_conv, weights)
    return (out_p[0], out_s[0]) + out_p[1:] + out_s[1:]
```

```python
import functools
import math

import jax
import jax.numpy as jnp
from jax import lax
from jax.experimental import pallas as pl
from jax.experimental.pallas import tpu as pltpu

F32 = jnp.float32
BF16 = jnp.bfloat16

HEAD_DIM = 64
LANES = 128
DILATIONS = (1, 4, 16)
WINDOW_STEPS = 128
PAST_LEN = 2048
ROPE_THETA = 10000.0
NORM_EPS = 1e-6
REC_CHUNK = 16
UNROLL = 8
LOG2_E = 1.4426950408889634
ROW_TILE = 512
MXU_WIDTH = 256
FFN_COLS = 1536
CACHE_BATCH = 2
VMEM_LIMIT = 48 * 1024 * 1024


def _params(sem):
    return pltpu.CompilerParams(dimension_semantics=sem, vmem_limit_bytes=VMEM_LIMIT)


def _dot(a, b):
    return jnp.dot(a, b, preferred_element_type=F32)


def _dot_nt(a, b):
    return lax.dot_general(a, b, (((1,), (1,)), ((), ())), preferred_element_type=F32)


def _dot_tn(a, b):
    return lax.dot_general(a, b, (((0,), (0,)), ((), ())), preferred_element_type=F32)


def _split3(x):
    hi = x.astype(BF16)
    r = x - hi.astype(F32)
    mid = r.astype(BF16)
    lo = (r - mid.astype(F32)).astype(BF16)
    return hi, mid, lo


def _silu(x):
    return x * jax.nn.sigmoid(x)


def _rms(x):
    return x * lax.rsqrt(jnp.mean(x * x, axis=-1, keepdims=True) + NORM_EPS)


def _token_tile(batch, seq):
    if seq >= ROW_TILE:
        assert seq % ROW_TILE == 0
        return 1, ROW_TILE
    assert ROW_TILE % seq == 0 and seq % 8 == 0
    bb = min(batch, ROW_TILE // seq)
    assert batch % bb == 0
    return bb, seq


def _ada_body(c_ref, w_ref, b_ref, o_ref):
    s = _silu(c_ref[...])
    s_hi, s_mid, _ = _split3(s)
    w_hi, w_mid, _ = _split3(w_ref[0])
    o_ref[0] = _dot(s_hi, w_hi) + _dot(s_hi, w_mid) + _dot(s_mid, w_hi) + b_ref[0]


def _ada(c_all, w_ada, b_ada):
    depth, d, n6 = w_ada.shape
    bc = c_all.shape[0]
    tn = 512
    return pl.pallas_call(
        _ada_body,
        grid=(depth, n6 // tn),
        in_specs=[pl.BlockSpec((bc, d), lambda l, j: (0, 0)),
                  pl.BlockSpec((1, d, tn), lambda l, j: (l, 0, j)),
                  pl.BlockSpec((1, 1, tn), lambda l, j: (l, 0, j))],
        out_specs=pl.BlockSpec((1, bc, tn), lambda l, j: (l, 0, j)),
        out_shape=jax.ShapeDtypeStruct((depth, bc, n6), F32),
        compiler_params=_params(("arbitrary", "arbitrary")),
        name="ada",
    )(c_all, w_ada, b_ada.reshape(depth, 1, n6))


def _rope_tables(pos, n_heads):
    half = HEAD_DIM // 2
    freqs = ROPE_THETA ** (-jnp.arange(half, dtype=F32) / half)
    ang = pos.astype(F32)[:, None] * freqs[None, :]
    cos, sin = jnp.cos(ang), jnp.sin(ang)
    cos_t = jnp.tile(jnp.concatenate([cos, cos], axis=-1), (1, n_heads))
    sin_t = jnp.tile(jnp.concatenate([-sin, sin], axis=-1), (1, n_heads))
    return cos_t, sin_t


def _rope(x, cos, sin_signed):
    outs = []
    for c in range(x.shape[-1] // LANES):
        sl = slice(c * LANES, (c + 1) * LANES)
        xc = x[:, sl]
        lane = lax.broadcasted_iota(jnp.int32, xc.shape, 1)
        ahead = pltpu.roll(xc, LANES - HEAD_DIM // 2, axis=1)
        behind = pltpu.roll(xc, HEAD_DIM // 2, axis=1)
        rot = jnp.where((lane & (HEAD_DIM // 2)) == 0, ahead, behind)
        outs.append(xc * cos[:, sl] + rot * sin_signed[:, sl])
    return jnp.concatenate(outs, axis=-1)


def _in_proj_body(x_ref, sh_ref, sc_ref, w_ref, cos_ref, sin_ref, *out_refs, bb, tt, widths, n_paired, runs):
    d = x_ref.shape[-1]
    x = x_ref[...].reshape(bb, tt, d)
    h = _rms(x) * (1.0 + sc_ref[...]) + sh_ref[...]
    hb = h.reshape(bb * tt, d).astype(BF16)
    for first_out, n_out in runs:
        off, wd = sum(widths[:first_out]), widths[first_out]
        group = _dot(hb, w_ref[:, off:off + wd * n_out])
        for n in range(n_out):
            i = first_out + n
            p = group[:, n * wd:(n + 1) * wd]
            if i < 2:
                p = _rope(p, cos_ref[...], sin_ref[...])
            if i < n_paired:
                for c in range(wd // LANES):
                    out_refs[i][c] = p[:, c * LANES:(c + 1) * LANES]
            else:
                out_refs[i][...] = p


def _equal_runs(widths):
    runs = []
    for i, wd in enumerate(widths):
        if runs and widths[runs[-1][0]] == wd:
            runs[-1][1] += 1
        else:
            runs.append([i, 1])
    return tuple((a, n) for a, n in runs)


def _in_proj(x2, shift, scale, w_in_bf, cos_t, sin_t, batch, seq, widths, n_paired):
    n, d = x2.shape
    bb, tt = _token_tile(batch, seq)
    tm = bb * tt
    n_t = seq // tt
    table_blocks = cos_t.shape[0] // tm
    body = functools.partial(_in_proj_body, bb=bb, tt=tt, widths=widths, n_paired=n_paired, runs=_equal_runs(widths))
    mod_spec = pl.BlockSpec((bb, 1, d), lambda i: (i // n_t, 0, 0))
    tab_spec = pl.BlockSpec((tm, cos_t.shape[1]), lambda i: (i % table_blocks, 0))
    out_specs, out_shape = [], []
    for idx, wd in enumerate(widths):
        if idx < n_paired:
            out_specs.append(pl.BlockSpec((wd // LANES, tm, LANES), lambda i: (0, i, 0)))
            out_shape.append(jax.ShapeDtypeStruct((wd // LANES, n, LANES), F32))
        else:
            out_specs.append(pl.BlockSpec((tm, wd), lambda i: (i, 0)))
            out_shape.append(jax.ShapeDtypeStruct((n, wd), F32))
    return pl.pallas_call(
        body,
        grid=(n // tm,),
        in_specs=[pl.BlockSpec((tm, d), lambda i: (i, 0)), mod_spec, mod_spec,
                  pl.BlockSpec(w_in_bf.shape, lambda i: (0, 0)), tab_spec, tab_spec],
        out_specs=out_specs,
        out_shape=out_shape,
        compiler_params=_params(("arbitrary",)),
        name="in_proj",
    )(x2, shift, scale, w_in_bf, cos_t, sin_t)


def _prompt_attn_body(q_ref, k_ref, v_ref, o_ref, m_ref, l_ref, acc_ref):
    n_pairs, seq, _ = q_ref.shape
    blk = WINDOW_STEPS
    lane = lax.broadcasted_iota(jnp.int32, (blk, LANES), 1)
    first = lane < HEAD_DIM
    scale = HEAD_DIM ** -0.5 * LOG2_E

    def rows(start, dil):
        return pl.ds(start, blk) if dil == 1 else pl.ds(start, blk, stride=dil)

    for b_idx, dil in enumerate(DILATIONS):
        n_blk = seq // dil // blk
        has_prev = n_blk > 1
        n_keys = 2 * blk if has_prev else blk
        q_row = lax.broadcasted_iota(jnp.int32, (2 * blk, n_keys), 0) & (blk - 1)
        k_col = lax.broadcasted_iota(jnp.int32, (2 * blk, n_keys), 1)
        is_first, is_last = b_idx == 0, b_idx == len(DILATIONS) - 1

        def unit(u, carry, dil=dil, has_prev=has_prev, q_row=q_row, k_col=k_col,
                 is_first=is_first, is_last=is_last):
            res = u & (dil - 1)
            i = u >> (dil.bit_length() - 1)
            start = res + dil * blk * i
            if dil == 1:
                start = pl.multiple_of(start, blk)
            if has_prev:
                prev_start = jnp.maximum(start - dil * blk, res)
                lo = jnp.maximum(q_row, jnp.where(i > 0, 0, blk))
                valid = (k_col >= lo) & (k_col <= q_row + blk)
            else:
                valid = k_col <= q_row
            loaded = []
            for p in range(n_pairs):
                q = q_ref[p, rows(start, dil), :] * scale
                kc = k_ref[p, rows(start, dil), :]
                vc = v_ref[p, rows(start, dil), :]
                if has_prev:
                    kc = jnp.concatenate([k_ref[p, rows(prev_start, dil), :], kc], axis=0)
                    vc = jnp.concatenate([v_ref[p, rows(prev_start, dil), :], vc], axis=0)
                old = None
                if not is_first:
                    old = (m_ref[p, rows(start, dil), :], l_ref[p, rows(start, dil), :],
                           acc_ref[p, rows(start, dil), :])
                loaded.append((q, kc, vc, old))
            scores = []
            for q, kc, vc, old in loaded:
                q2 = jnp.concatenate([jnp.where(first, q, 0.0), jnp.where(first, 0.0, q)], axis=0)
                scores.append(_dot_nt(q2.astype(BF16), kc.astype(BF16)))
            results = []
            for (q, kc, vc, old), s in zip(loaded, scores):
                s = jnp.where(valid, s, -jnp.inf)
                m2 = jnp.max(s, axis=-1, keepdims=True)
                e = jnp.exp2(s - m2).astype(BF16)
                v1 = jnp.concatenate([vc.astype(BF16), jnp.ones(vc.shape, BF16)], axis=1)
                pv = _dot(e, v1)
                m_b = jnp.where(first, m2[:blk], m2[blk:])
                l_b = jnp.where(first, pv[:blk, LANES:], pv[blk:, LANES:])
                acc_b = jnp.where(first, pv[:blk, :LANES], pv[blk:, :LANES])
                if not is_first:
                    m_o, l_o, acc_o = old
                    m_n = jnp.maximum(m_o, m_b)
                    w_o, w_b = jnp.exp2(m_o - m_n), jnp.exp2(m_b - m_n)
                    l_b = w_o * l_o + w_b * l_b
                    acc_b = w_o * acc_o + w_b * acc_b
                    m_b = m_n
                results.append((m_b, l_b, acc_b))
            for p, (m_b, l_b, acc_b) in enumerate(results):
                if is_last:
                    o_ref[p, rows(start, dil), :] = acc_b / l_b
                else:
                    m_ref[p, rows(start, dil), :] = m_b
                    l_ref[p, rows(start, dil), :] = l_b
                    acc_ref[p, rows(start, dil), :] = acc_b
            return carry

        lax.fori_loop(0, dil * n_blk, unit, 0, unroll=4)


def _prompt_attention(q3, k3, v3, batch, seq):
    n_pairs = q3.shape[0]
    assert seq % (WINDOW_STEPS * max(DILATIONS)) == 0
    spec = pl.BlockSpec((n_pairs, seq, LANES), lambda b: (0, b, 0))
    return pl.pallas_call(
        _prompt_attn_body,
        grid=(batch,),
        in_specs=[spec, spec, spec],
        out_specs=spec,
        out_shape=jax.ShapeDtypeStruct(q3.shape, F32),
        scratch_shapes=[pltpu.VMEM((n_pairs, seq, LANES), F32)] * 3,
        compiler_params=_params(("arbitrary",)),
        name="prompt_attn",
    )(q3, k3, v3)


def _branch_count(delta):
    cnt = jnp.zeros(delta.shape, F32)
    for dil in DILATIONS:
        hit = (delta >= 0) & (delta <= WINDOW_STEPS * dil) & ((delta & (dil - 1)) == 0)
        cnt = cnt + jnp.where(hit, 1.0, 0.0)
    return cnt


def _cache_attn_body(q_ref, kn_ref, vn_ref, kt_ref, vt_ref, o_ref, *, past, t_new):
    n_pairs = q_ref.shape[0]
    t_c = lax.broadcasted_iota(jnp.int32, (t_new, past), 0)
    j_c = lax.broadcasted_iota(jnp.int32, (t_new, past), 1)
    cnt_c = _branch_count(past + t_c - j_c)
    t_n = lax.broadcasted_iota(jnp.int32, (t_new, t_new), 0)
    j_n = lax.broadcasted_iota(jnp.int32, (t_new, t_new), 1)
    cnt_n = _branch_count(t_n - j_n)
    scale = HEAD_DIM ** -0.5
    for bi in range(kt_ref.shape[1]):
        rs = slice(bi * t_new, (bi + 1) * t_new)
        for p in range(n_pairs):
            q_pair, kn_pair, vn_pair = q_ref[p, rs, :] * scale, kn_ref[p, rs, :], vn_ref[p, rs, :]
            outs = []
            for hh in range(LANES // HEAD_DIM):
                h = p * (LANES // HEAD_DIM) + hh
                sl = slice(hh * HEAD_DIM, (hh + 1) * HEAD_DIM)
                q = q_pair[:, sl]
                s_c = jnp.where(cnt_c > 0, _dot(q.astype(BF16), kt_ref[0, bi, h].astype(BF16)), -jnp.inf)
                s_n = jnp.where(cnt_n > 0, _dot_nt(q, kn_pair[:, sl]), -jnp.inf)
                m = jnp.maximum(jnp.max(s_c, axis=-1, keepdims=True), jnp.max(s_n, axis=-1, keepdims=True))
                p_c = cnt_c * jnp.exp(s_c - m)
                p_n = cnt_n * jnp.exp(s_n - m)
                den = jnp.sum(p_c, axis=-1, keepdims=True) + jnp.sum(p_n, axis=-1, keepdims=True)
                acc = _dot_nt(p_c.astype(BF16), vt_ref[0, bi, h].astype(BF16)) + _dot(p_n, vn_pair[:, sl])
                outs.append(acc / den)
            o_ref[p, rs, :] = jnp.concatenate(outs, axis=-1)


def _cache_attention(q3, kn3, vn3, cache_k, cache_v, layer, batch, seq):
    depth, _, past, n_heads, hd = cache_k.shape
    n_pairs = q3.shape[0]
    bb = CACHE_BATCH if batch % CACHE_BATCH == 0 else 1
    by_head = lambda a: jnp.transpose(a, (0, 1, 3, 4, 2))
    new = pl.BlockSpec((n_pairs, bb * seq, LANES), lambda b: (0, b, 0))
    old = pl.BlockSpec((1, bb, n_heads, hd, past), lambda b: (layer, b, 0, 0, 0))
    return pl.pallas_call(
        functools.partial(_cache_attn_body, past=past, t_new=seq),
        grid=(batch // bb,),
        in_specs=[new, new, new, old, old],
        out_specs=new,
        out_shape=jax.ShapeDtypeStruct(q3.shape, F32),
        compiler_params=_params(("arbitrary",)),
        name="cache_attn",
    )(q3, kn3, vn3, by_head(cache_k), by_head(cache_v))


def _lower_bound(logits, layer):
    e = jnp.exp(logits - jnp.max(logits, axis=0, keepdims=True))
    sm = e / jnp.sum(e, axis=0, keepdims=True)
    lb = jnp.zeros_like(sm[0:1])
    for j in range(1, layer + 1):
        lb = lb + sm[j:j + 1]
    return lb


def _hgrn_body(lbl_ref, nw_ref, q_ref, f_ref, i_ref, g_ref, s0_ref, o_ref, sout_ref,
               st_ref, b2_ref, kk_ref, eb_ref, q_s_ref, v_s_ref, qt_ref, kh_ref, o_s_ref,
               *, layer, n_chunks):
    c = REC_CHUNK
    d_rec = q_ref.shape[-1]
    n_pairs = d_rec // LANES
    rows_in = q_ref.shape[1]
    rows = n_chunks * c
    grp = min(rows, LANES)

    @pl.when(pl.program_id(1) == 0)
    def _():
        st_ref[...] = s0_ref[0]

    def padded(ref):
        x = ref[0]
        if rows_in < rows:
            x = jnp.concatenate([x, jnp.zeros((rows - rows_in, d_rec), F32)], axis=0)
        return x

    lb = _lower_bound(lbl_ref[...], layer)
    log_lb = jnp.log(lb)
    z = padded(f_ref)
    t = jnp.exp(-jnp.abs(z))
    log_sig = jnp.minimum(z, 0.0) - jnp.log(1.0 + t)
    b_term = jnp.log1p(-lb) + log_sig
    log_f = jnp.maximum(log_lb, b_term) + jnp.log(1.0 + jnp.exp(-jnp.abs(log_lb - b_term)))
    inv = 1.0 / (1.0 + t)
    kk = (1.0 - lb) * jnp.where(z > 0.0, t * inv, inv)
    if rows_in < rows:
        live = lax.broadcasted_iota(jnp.int32, (rows, d_rec), 0) < rows_in
        log_f = jnp.where(live, log_f, 0.0)
        kk = jnp.where(live, kk, 0.0)
    r_i = lax.broadcasted_iota(jnp.int32, (grp, grp), 0)
    c_i = lax.broadcasted_iota(jnp.int32, (grp, grp), 1)
    same_chunk = (r_i >> (c.bit_length() - 1)) == (c_i >> (c.bit_length() - 1))
    prefix = jnp.where(same_chunk & (c_i <= r_i), 1.0, 0.0).astype(BF16)
    suffix = jnp.where(same_chunk & (c_i > r_i), 1.0, 0.0).astype(BF16)
    q_all = padded(q_ref)
    for g in range(rows // grp):
        gs = slice(g * grp, (g + 1) * grp)
        parts = _split3(log_f[gs])
        b = sum(_dot(prefix, part) for part in parts)
        r = sum(_dot(suffix, part) for part in parts)
        eb = jnp.exp(b)
        b2_ref[gs, :] = b * LOG2_E
        eb_ref[gs, :] = eb
        qt_ref[gs, :] = (q_all[gs] * eb).astype(BF16)
        kh_ref[gs, :] = (kk[gs] * jnp.exp(r)).astype(BF16)
    kk_ref[...] = kk
    q_s_ref[...] = q_all
    v_s_ref[...] = padded(i_ref)

    l_r = lax.broadcasted_iota(jnp.int32, (LANES, LANES), 0) >> (HEAD_DIM.bit_length() - 1)
    l_c = lax.broadcasted_iota(jnp.int32, (LANES, LANES), 1) >> (HEAD_DIM.bit_length() - 1)
    same_head = l_r == l_c
    head_sum = jnp.where(same_head, 1.0, 0.0).astype(BF16)
    row = lax.broadcasted_iota(jnp.int32, (c, LANES), 0)

    def chunk(ci, carry):
        r0 = pl.multiple_of(ci * c, c)
        rs = pl.ds(r0, c)
        for p in range(n_pairs):
            sl = slice(p * LANES, (p + 1) * LANES)
            b2, q, v, k = b2_ref[rs, sl], q_s_ref[rs, sl], v_s_ref[rs, sl], kk_ref[rs, sl]
            xs = []
            for s_i in range(c):
                cap = jnp.where(row >= s_i, 0.0, -jnp.inf)
                e = jnp.exp2(jnp.minimum(b2 - b2[s_i:s_i + 1], cap))
                xs.append((q * e) * k[s_i:s_i + 1])
            a_rep = _dot(jnp.concatenate(xs, axis=0).astype(BF16), head_sum)
            o = a_rep[0:c] * v[0:1]
            for s_i in range(1, c):
                o = o + a_rep[s_i * c:(s_i + 1) * c] * v[s_i:s_i + 1]
            st = st_ref[p]
            o = o + _dot_nt(qt_ref[rs, sl], st.astype(BF16))
            upd = _dot_tn(v.astype(BF16), kh_ref[rs, sl])
            st_ref[p] = st * eb_ref[rs, sl][c - 1:c] + jnp.where(same_head, upd, 0.0)
            o_s_ref[rs, sl] = o
        return carry

    lax.fori_loop(0, n_chunks, chunk, 0, unroll=UNROLL if n_chunks % UNROLL == 0 else 1)

    g_all = padded(g_ref)
    outs = []
    for p in range(n_pairs):
        sl = slice(p * LANES, (p + 1) * LANES)
        o = o_s_ref[:, sl]
        sq_hi, sq_mid, _ = _split3(o * o)
        ms = (_dot(sq_hi, head_sum) + _dot(sq_mid, head_sum)) * (1.0 / HEAD_DIM)
        outs.append(o * lax.rsqrt(ms + NORM_EPS) * nw_ref[:, sl] * _silu(g_all[:, sl]))
    o_ref[0] = jnp.concatenate(outs, axis=-1)[:rows_in]

    @pl.when(pl.program_id(1) == pl.num_programs(1) - 1)
    def _():
        sout_ref[0] = st_ref[...]


def _pair_state(s):
    b, h, kd, vd = s.shape
    st = jnp.swapaxes(s, 2, 3).reshape(b, h // 2, 2, vd, kd)
    z = jnp.zeros_like(st[:, :, 0])
    top = jnp.concatenate([st[:, :, 0], z], axis=-1)
    bot = jnp.concatenate([z, st[:, :, 1]], axis=-1)
    return jnp.concatenate([top, bot], axis=-2)


def _unpair_state(sp):
    b, hp = sp.shape[:2]
    s0 = sp[:, :, :HEAD_DIM, :HEAD_DIM]
    s1 = sp[:, :, HEAD_DIM:, HEAD_DIM:]
    st = jnp.stack([s0, s1], axis=2).reshape(b, 2 * hp, HEAD_DIM, HEAD_DIM)
    return jnp.swapaxes(st, 2, 3)


def _hgrn(qr, fr, ir, gr, s0, lb_logits, norm_w, layer, batch, seq):
    d_rec = qr.shape[-1]
    n_pairs = d_rec // LANES
    tb = min(seq, ROW_TILE)
    assert seq % tb == 0 and (tb % REC_CHUNK == 0 or tb == seq < REC_CHUNK)
    n_chunks = max(tb // REC_CHUNK, 1)
    rows = n_chunks * REC_CHUNK
    view = lambda a: a.reshape(batch, seq, d_rec)
    tok = pl.BlockSpec((1, tb, d_rec), lambda b, t: (b, t, 0))
    state = pl.BlockSpec((1, n_pairs, LANES, LANES), lambda b, t: (b, 0, 0, 0))
    o, s_new = pl.pallas_call(
        functools.partial(_hgrn_body, layer=layer, n_chunks=n_chunks),
        grid=(batch, seq // tb),
        in_specs=[pl.BlockSpec(lb_logits.shape, lambda b, t: (0, 0)),
                  pl.BlockSpec((1, d_rec), lambda b, t: (0, 0)),
                  tok, tok, tok, tok, state],
        out_specs=[tok, state],
        out_shape=[jax.ShapeDtypeStruct((batch, seq, d_rec), F32),
                   jax.ShapeDtypeStruct((batch, n_pairs, LANES, LANES), F32)],
        scratch_shapes=[pltpu.VMEM((n_pairs, LANES, LANES), F32)]
        + [pltpu.VMEM((rows, d_rec), F32)] * 5 + [pltpu.VMEM((rows, d_rec), BF16)] * 2
        + [pltpu.VMEM((rows, d_rec), F32)],
        compiler_params=_params(("arbitrary", "arbitrary")),
        name="hgrn",
    )(lb_logits, norm_w.reshape(1, d_rec), view(qr), view(fr), view(ir), view(gr), _pair_state(s0))
    return o.reshape(batch * seq, d_rec), _unpair_state(s_new)


def _hgrn_step_body(lblt_ref, nwt_ref, q_ref, f_ref, i_ref, g_ref, s0_ref, o_ref, sout_ref, *, layer):
    n_t, hd, _ = q_ref.shape
    logits = lblt_ref[...]
    e = jnp.exp(logits - jnp.max(logits, axis=1, keepdims=True))
    sm = e / jnp.sum(e, axis=1, keepdims=True)
    lb = jnp.zeros_like(sm[:, 0:1])
    for j in range(1, layer + 1):
        lb = lb + sm[:, j:j + 1]
    for t in range(n_t):
        z = f_ref[t]
        f_t = lb + (1.0 - lb) * jax.nn.sigmoid(z)
        k_t = (1.0 - lb) * jax.nn.sigmoid(-z)
        q_t, v_t = q_ref[t], i_ref[t]
        src = s0_ref if t == 0 else sout_ref
        o_t = jnp.zeros(v_t.shape, F32)
        for k in range(hd):
            s_k = f_t[k:k + 1] * src[0, 0, k] + k_t[k:k + 1] * v_t
            sout_ref[0, 0, k] = s_k
            o_t = o_t + s_k * q_t[k:k + 1]
        ms = jnp.mean(o_t * o_t, axis=0, keepdims=True)
        o_ref[t] = o_t * lax.rsqrt(ms + NORM_EPS) * nwt_ref[...] * _silu(g_ref[t])


def _hgrn_step(qr, fr, ir, gr, state_all, lb_logits, norm_w, layer, batch, seq):
    d_rec = qr.shape[-1]
    n_heads = d_rec // HEAD_DIM
    to_lanes = lambda a: jnp.transpose(a.reshape(batch, seq, d_rec), (1, 2, 0))
    tok = pl.BlockSpec((seq, HEAD_DIM, batch), lambda h: (0, h, 0))
    state_in = pl.BlockSpec((1, 1, HEAD_DIM, HEAD_DIM, batch), lambda h: (layer, h, 0, 0, 0))
    state_out = pl.BlockSpec((1, 1, HEAD_DIM, HEAD_DIM, batch), lambda h: (0, h, 0, 0, 0))
    o_t, s_new = pl.pallas_call(
        functools.partial(_hgrn_step_body, layer=layer),
        grid=(n_heads,),
        in_specs=[pl.BlockSpec((HEAD_DIM, lb_logits.shape[0]), lambda h: (h, 0)),
                  pl.BlockSpec((HEAD_DIM, 1), lambda h: (h, 0)),
                  tok, tok, tok, tok, state_in],
        out_specs=[tok, state_out],
        out_shape=[jax.ShapeDtypeStruct((seq, d_rec, batch), F32),
                   jax.ShapeDtypeStruct((1, n_heads, HEAD_DIM, HEAD_DIM, batch), F32)],
        compiler_params=_params(("arbitrary",)),
        name="hgrn_step",
    )(lb_logits.T, norm_w.reshape(d_rec, 1), to_lanes(qr), to_lanes(fr), to_lanes(ir), to_lanes(gr),
      jnp.transpose(state_all, (0, 2, 3, 4, 1)))
    o = jnp.transpose(o_t, (2, 0, 1)).reshape(batch * seq, d_rec)
    return o, jnp.transpose(s_new[0], (3, 0, 1, 2))


CONV_PAD = 8


def _conv_body(bc_ref, cc_ref, xc_ref, buf_ref, w_ref, b_ref, oc_ref, new_ref, up_ref):
    seq = cc_ref.shape[1]
    width = w_ref.shape[0]
    u = cc_ref[0] * xc_ref[0]
    up_ref[pl.ds(CONV_PAD, seq), :] = u
    up_ref[pl.ds(CONV_PAD - (width - 1), width - 1), :] = buf_ref[0]
    y = u * w_ref[width - 1:width, :] + b_ref[...]
    for i in range(width - 1):
        y = y + up_ref[pl.ds(CONV_PAD - (width - 1) + i, seq), :] * w_ref[i:i + 1, :]
    oc_ref[0] = bc_ref[0] * y
    new_ref[0] = up_ref[pl.ds(CONV_PAD + seq - (width - 1), width - 1), :]


def _short_conv(bc, cc, xc, buf, conv_w, conv_b, batch, seq):
    d_conv = bc.shape[-1]
    width = conv_w.shape[0]
    view = lambda a: a.reshape(batch, seq, d_conv)
    tok = pl.BlockSpec((1, seq, d_conv), lambda b: (b, 0, 0))
    tail = pl.BlockSpec((1, width - 1, d_conv), lambda b: (b, 0, 0))
    oc, new = pl.pallas_call(
        _conv_body,
        grid=(batch,),
        in_specs=[tok, tok, tok, tail,
                  pl.BlockSpec((width, d_conv), lambda b: (0, 0)),
                  pl.BlockSpec((1, d_conv), lambda b: (0, 0))],
        out_specs=[tok, tail],
        out_shape=[jax.ShapeDtypeStruct((batch, seq, d_conv), F32),
                   jax.ShapeDtypeStruct((batch, width - 1, d_conv), F32)],
        scratch_shapes=[pltpu.VMEM((CONV_PAD + seq, d_conv), F32)],
        compiler_params=_params(("arbitrary",)),
        name="short_conv",
    )(view(bc), view(cc), view(xc), buf, conv_w, conv_b.reshape(1, d_conv))
    return oc.reshape(batch * seq, d_conv), new


def _mix_ffn_body(oa_ref, or_ref, oc_ref, x_ref, g1_ref, sh_ref, sc_ref, g2_ref, w_ref, wi_ref, wo_ref,
                  fw_ref, o_ref, *, bb, tt, final, cuts):
    d = x_ref.shape[-1]
    d_ff = wo_ref.shape[0]
    n_pairs = oa_ref.shape[0]
    d_att, d_rec = n_pairs * LANES, or_ref.shape[-1]
    mix = (_dot(or_ref[...].astype(BF16), w_ref[d_att:d_att + d_rec, :])
           + _dot(oc_ref[...].astype(BF16), w_ref[d_att + d_rec:, :]))
    for p in range(n_pairs):
        mix = mix + _dot(oa_ref[p].astype(BF16), w_ref[p * LANES:(p + 1) * LANES, :])
    x = x_ref[...].reshape(bb, tt, d) + g1_ref[...] * mix.reshape(bb, tt, d)
    h = (_rms(x) * (1.0 + sc_ref[...]) + sh_ref[...]).reshape(bb * tt, d).astype(BF16)
    acc = None
    for c0, c1 in zip(cuts[:-1], cuts[1:]):
        a = (_silu(_dot(h, wi_ref[:, c0:c1])) * _dot(h, wi_ref[:, d_ff + c0:d_ff + c1])).astype(BF16)
        part = _dot(a, wo_ref[c0:c1, :])
        acc = part if acc is None else acc + part
    x = x + g2_ref[...] * acc.reshape(bb, tt, d)
    if final:
        x = _rms(x) * fw_ref[...]
    o_ref[...] = x.reshape(bb * tt, d)


def _mix_ffn(oa3, o_r, oc, x2, g1, shift, scale, g2, w_out_bf, w_ffn_in_bf, w_ffn_out_bf, final_w, batch, seq, final):
    n, d = x2.shape
    d_ff = w_ffn_out_bf.shape[0]
    bb, tt = _token_tile(batch, seq)
    tm = bb * tt
    n_t = seq // tt
    assert d_ff % MXU_WIDTH == 0
    n_cut = -(-d_ff // FFN_COLS)
    cuts = tuple(round(i * (d_ff // MXU_WIDTH) / n_cut) * MXU_WIDTH for i in range(n_cut + 1))
    tok = lambda a: pl.BlockSpec((tm, a.shape[-1]), lambda i: (i, 0))
    mod_spec = pl.BlockSpec((bb, 1, d), lambda i: (i // n_t, 0, 0))
    held = lambda a: pl.BlockSpec(a.shape, lambda i: (0,) * a.ndim, pipeline_mode=pl.Buffered(1))
    return pl.pallas_call(
        functools.partial(_mix_ffn_body, bb=bb, tt=tt, final=final, cuts=cuts),
        grid=(n // tm,),
        in_specs=[pl.BlockSpec((oa3.shape[0], tm, LANES), lambda i: (0, i, 0)), tok(o_r), tok(oc), tok(x2)]
        + [mod_spec] * 4
        + [held(w_out_bf), held(w_ffn_in_bf), held(w_ffn_out_bf), pl.BlockSpec((1, d), lambda i: (0, 0))],
        out_specs=pl.BlockSpec((tm, d), lambda i: (i, 0)),
        out_shape=jax.ShapeDtypeStruct((n, d), F32),
        compiler_params=_params(("arbitrary",)),
        name="mix_ffn",
    )(oa3, o_r, oc, x2, g1, shift, scale, g2, w_out_bf, w_ffn_in_bf, w_ffn_out_bf, final_w.reshape(1, d))


def _kv_layout_body(*refs, depth):
    ins, outs = refs[:2 * depth], refs[2 * depth:]
    for which, o_ref in enumerate(outs):
        for l in range(depth):
            src = ins[which * depth + l]
            for p in range(src.shape[0]):
                o_ref[l, 0, p * LANES:(p + 1) * LANES, :] = src[p].T


def _kv_layout(ks, vs, batch, seq):
    depth = len(ks)
    n_pairs = ks[0].shape[0]
    d_att = n_pairs * LANES
    tt = min(seq, ROW_TILE)
    n_t = seq // tt
    src = pl.BlockSpec((n_pairs, tt, LANES), lambda b, j: (0, b * n_t + j, 0))
    dst = pl.BlockSpec((depth, 1, d_att, tt), lambda b, j: (0, b, 0, j))
    shape = jax.ShapeDtypeStruct((depth, batch, d_att, seq), F32)
    k_t, v_t = pl.pallas_call(
        functools.partial(_kv_layout_body, depth=depth),
        grid=(batch, n_t),
        in_specs=[src] * (2 * depth),
        out_specs=[dst, dst],
        out_shape=[shape, shape],
        compiler_params=_params(("arbitrary", "arbitrary")),
        name="kv_layout",
    )(*ks, *vs)
    heads = lambda a: jnp.transpose(a.reshape(depth, batch, d_att // HEAD_DIM, HEAD_DIM, seq), (0, 1, 4, 2, 3))
    return heads(k_t), heads(v_t)


def _trunk(x, mods, pos, cache_k, cache_v, state_hgrn, state_conv, weights):
    (w_in, conv_w, conv_b, lb_logits, hgrn_norm_w, w_out, w_ffn_in, w_ffn_out, final_norm_w) = weights
    batch, seq, d = x.shape
    depth = w_in.shape[0]
    d_conv = conv_w.shape[-1]
    d_att = d_rec = (w_in.shape[-1] - 3 * d_conv) // 7
    n_heads = d_att // HEAD_DIM
    widths = (d_att,) * 3 + (d_rec,) * 4 + (d_conv,) * 3
    bb, tt = _token_tile(batch, seq)
    cos_t, sin_t = _rope_tables(pos, n_heads)
    if bb > 1:
        cos_t, sin_t = jnp.tile(cos_t, (bb, 1)), jnp.tile(sin_t, (bb, 1))
    x2 = x.reshape(batch * seq, d)
    ks, vs, hs, cs = [], [], [], []
    for l in range(depth):
        sh1, sc1, g1, sh2, sc2, g2 = [m.reshape(batch, 1, d) for m in jnp.split(mods[l], 6, axis=-1)]
        qa, ka, va, qr, fr, ir, gr, bc, cc, xc = _in_proj(
            x2, sh1, sc1, w_in[l], cos_t, sin_t, batch, seq, widths, 3)
        if cache_k is None:
            oa = _prompt_attention(qa, ka, va, batch, seq)
            s0 = jnp.zeros((batch, n_heads, HEAD_DIM, HEAD_DIM), F32)
            conv0 = jnp.zeros((batch, conv_w.shape[1] - 1, d_conv), F32)
        else:
            oa = _cache_attention(qa, ka, va, cache_k, cache_v, l, batch, seq)
            conv0 = state_conv[l]
        if cache_k is None:
            o_r, s_new = _hgrn(qr, fr, ir, gr, s0, lb_logits, hgrn_norm_w[l], l, batch, seq)
        else:
            o_r, s_new = _hgrn_step(qr, fr, ir, gr, state_hgrn, lb_logits, hgrn_norm_w[l], l, batch, seq)
        oc, conv_new = _short_conv(bc, cc, xc, conv0, conv_w[l], conv_b[l], batch, seq)
        x2 = _mix_ffn(oa, o_r, oc, x2, g1, sh2, sc2, g2, w_out[l], w_ffn_in[l], w_ffn_out[l], final_norm_w,
                      batch, seq, l == depth - 1)
        ks.append(ka)
        vs.append(va)
        hs.append(s_new)
        cs.append(conv_new)
    keep = min(WINDOW_STEPS * max(DILATIONS), seq)
    if seq % LANES == 0:
        k_out, v_out = _kv_layout(ks, vs, batch, seq)
    else:
        unpair = lambda parts: jnp.transpose(
            jnp.stack(parts).reshape(depth, n_heads // 2, batch, seq, 2, HEAD_DIM),
            (0, 2, 3, 1, 4, 5)).reshape(depth, batch, seq, n_heads, HEAD_DIM)
        k_out, v_out = unpair(ks), unpair(vs)
    return (x2.reshape(batch, seq, d), k_out[:, :, seq - keep:], v_out[:, :, seq - keep:],
            jnp.stack(hs), jnp.stack(cs))


def kernel(x_prompt, x_sample, cache_k, cache_v, state_hgrn, state_conv, c_prompt, c_sample,
           w_ada, b_ada, w_in, conv_w, conv_b, hgrn_lb_logits, hgrn_norm_w, w_out,
           w_ffn_in, w_ffn_out, final_norm_w):
    n_prompt = c_prompt.shape[0]
    past_len = PAST_LEN
    assert cache_k.shape[2] == min(WINDOW_STEPS * max(DILATIONS), PAST_LEN) == PAST_LEN
    mods = _ada(jnp.concatenate([c_prompt, c_sample], axis=0), w_ada, b_ada)
    weights = (w_in.astype(BF16), conv_w, conv_b, hgrn_lb_logits.astype(F32), hgrn_norm_w,
               w_out.astype(BF16), w_ffn_in.astype(BF16), w_ffn_out.astype(BF16), final_norm_w)
    pos_p = jnp.arange(x_prompt.shape[1], dtype=jnp.int32)
    pos_s = past_len + jnp.arange(x_sample.shape[1], dtype=jnp.int32)
    out_p = _trunk(x_prompt, mods[:, :n_prompt], pos_p, None, None, None, None, weights)
    out_s = _trunk(x_sample, mods[:, n_prompt:], pos_s, cache_k, cache_v, state_hgrn, state_conv, weights)
    return (out_p[0], out_s[0]) + out_p[1:] + out_s[1:]
```

```python
import functools
import math

import jax
import jax.numpy as jnp
from jax import lax
from jax.experimental import pallas as pl
from jax.experimental.pallas import tpu as pltpu

F32 = jnp.float32
BF16 = jnp.bfloat16

HEAD_DIM = 64
LANES = 128
DILATIONS = (1, 4, 16)
WINDOW_STEPS = 128
PAST_LEN = 2048
ROPE_THETA = 10000.0
NORM_EPS = 1e-6
REC_CHUNK = 16
UNROLL = 8
LOG2_E = 1.4426950408889634
ROW_TILE = 512
MXU_WIDTH = 256
FFN_COLS = 1536
CACHE_BATCH = 2
VMEM_LIMIT = 48 * 1024 * 1024


def _params(sem):
    return pltpu.CompilerParams(dimension_semantics=sem, vmem_limit_bytes=VMEM_LIMIT)


def _dot(a, b):
    return jnp.dot(a, b, preferred_element_type=F32)


def _dot_nt(a, b):
    return lax.dot_general(a, b, (((1,), (1,)), ((), ())), preferred_element_type=F32)


def _dot_tn(a, b):
    return lax.dot_general(a, b, (((0,), (0,)), ((), ())), preferred_element_type=F32)


def _split3(x):
    hi = x.astype(BF16)
    r = x - hi.astype(F32)
    mid = r.astype(BF16)
    lo = (r - mid.astype(F32)).astype(BF16)
    return hi, mid, lo


def _silu(x):
    return x * jax.nn.sigmoid(x)


def _rms(x):
    return x * lax.rsqrt(jnp.mean(x * x, axis=-1, keepdims=True) + NORM_EPS)


def _token_tile(batch, seq):
    if seq >= ROW_TILE:
        assert seq % ROW_TILE == 0
        return 1, ROW_TILE
    assert ROW_TILE % seq == 0 and seq % 8 == 0
    bb = min(batch, ROW_TILE // seq)
    assert batch % bb == 0
    return bb, seq


def _ada_body(c_ref, w_ref, b_ref, o_ref):
    s = _silu(c_ref[...])
    s_hi, s_mid, _ = _split3(s)
    w_hi, w_mid, _ = _split3(w_ref[0])
    o_ref[0] = _dot(s_hi, w_hi) + _dot(s_hi, w_mid) + _dot(s_mid, w_hi) + b_ref[0]


def _ada(c_all, w_ada, b_ada):
    depth, d, n6 = w_ada.shape
    bc = c_all.shape[0]
    tn = 512
    return pl.pallas_call(
        _ada_body,
        grid=(depth, n6 // tn),
        in_specs=[pl.BlockSpec((bc, d), lambda l, j: (0, 0)),
                  pl.BlockSpec((1, d, tn), lambda l, j: (l, 0, j)),
                  pl.BlockSpec((1, 1, tn), lambda l, j: (l, 0, j))],
        out_specs=pl.BlockSpec((1, bc, tn), lambda l, j: (l, 0, j)),
        out_shape=jax.ShapeDtypeStruct((depth, bc, n6), F32),
        compiler_params=_params(("arbitrary", "arbitrary")),
        name="ada",
    )(c_all, w_ada, b_ada.reshape(depth, 1, n6))


def _rope_tables(pos, n_heads):
    half = HEAD_DIM // 2
    freqs = ROPE_THETA ** (-jnp.arange(half, dtype=F32) / half)
    ang = pos.astype(F32)[:, None] * freqs[None, :]
    cos, sin = jnp.cos(ang), jnp.sin(ang)
    cos_t = jnp.tile(jnp.concatenate([cos, cos], axis=-1), (1, n_heads))
    sin_t = jnp.tile(jnp.concatenate([-sin, sin], axis=-1), (1, n_heads))
    return cos_t, sin_t


def _rope(x, cos, sin_signed):
    outs = []
    for c in range(x.shape[-1] // LANES):
        sl = slice(c * LANES, (c + 1) * LANES)
        xc = x[:, sl]
        lane = lax.broadcasted_iota(jnp.int32, xc.shape, 1)
        ahead = pltpu.roll(xc, LANES - HEAD_DIM // 2, axis=1)
        behind = pltpu.roll(xc, HEAD_DIM // 2, axis=1)
        rot = jnp.where((lane & (HEAD_DIM // 2)) == 0, ahead, behind)
        outs.append(xc * cos[:, sl] + rot * sin_signed[:, sl])
    return jnp.concatenate(outs, axis=-1)


def _in_proj_body(x_ref, sh_ref, sc_ref, w_ref, cos_ref, sin_ref, *out_refs, bb, tt, widths, n_paired, runs):
    d = x_ref.shape[-1]
    x = x_ref[...].reshape(bb, tt, d)
    h = _rms(x) * (1.0 + sc_ref[...]) + sh_ref[...]
    hb = h.reshape(bb * tt, d).astype(BF16)
    for first_out, n_out in runs:
        off, wd = sum(widths[:first_out]), widths[first_out]
        group = _dot(hb, w_ref[:, off:off + wd * n_out])
        for n in range(n_out):
            i = first_out + n
            p = group[:, n * wd:(n + 1) * wd]
            if i < 2:
                p = _rope(p, cos_ref[...], sin_ref[...])
            if i < n_paired:
                for c in range(wd // LANES):
                    out_refs[i][c] = p[:, c * LANES:(c + 1) * LANES]
            else:
                out_refs[i][...] = p


def _equal_runs(widths):
    runs = []
    for i, wd in enumerate(widths):
        if runs and widths[runs[-1][0]] == wd:
            runs[-1][1] += 1
        else:
            runs.append([i, 1])
    return tuple((a, n) for a, n in runs)


def _in_proj(x2, shift, scale, w_in_bf, cos_t, sin_t, batch, seq, widths, n_paired):
    n, d = x2.shape
    bb, tt = _token_tile(batch, seq)
    tm = bb * tt
    n_t = seq // tt
    table_blocks = cos_t.shape[0] // tm
    body = functools.partial(_in_proj_body, bb=bb, tt=tt, widths=widths, n_paired=n_paired, runs=_equal_runs(widths))
    mod_spec = pl.BlockSpec((bb, 1, d), lambda i: (i // n_t, 0, 0))
    tab_spec = pl.BlockSpec((tm, cos_t.shape[1]), lambda i: (i % table_blocks, 0))
    out_specs, out_shape = [], []
    for idx, wd in enumerate(widths):
        if idx < n_paired:
            out_specs.append(pl.BlockSpec((wd // LANES, tm, LANES), lambda i: (0, i, 0)))
            out_shape.append(jax.ShapeDtypeStruct((wd // LANES, n, LANES), F32))
        else:
            out_specs.append(pl.BlockSpec((tm, wd), lambda i: (i, 0)))
            out_shape.append(jax.ShapeDtypeStruct((n, wd), F32))
    return pl.pallas_call(
        body,
        grid=(n // tm,),
        in_specs=[pl.BlockSpec((tm, d), lambda i: (i, 0)), mod_spec, mod_spec,
                  pl.BlockSpec(w_in_bf.shape, lambda i: (0, 0)), tab_spec, tab_spec],
        out_specs=out_specs,
        out_shape=out_shape,
        compiler_params=_params(("arbitrary",)),
        name="in_proj",
    )(x2, shift, scale, w_in_bf, cos_t, sin_t)


def _prompt_attn_body(q_ref, k_ref, v_ref, o_ref, m_ref, l_ref, acc_ref):
    n_pairs, seq, _ = q_ref.shape
    blk = WINDOW_STEPS
    lane = lax.broadcasted_iota(jnp.int32, (blk, LANES), 1)
    first = lane < HEAD_DIM
    scale = HEAD_DIM ** -0.5 * LOG2_E

    def rows(start, dil):
        return pl.ds(start, blk) if dil == 1 else pl.ds(start, blk, stride=dil)

    for b_idx, dil in enumerate(DILATIONS):
        n_blk = seq // dil // blk
        has_prev = n_blk > 1
        n_keys = 2 * blk if has_prev else blk
        q_row = lax.broadcasted_iota(jnp.int32, (2 * blk, n_keys), 0) & (blk - 1)
        k_col = lax.broadcasted_iota(jnp.int32, (2 * blk, n_keys), 1)
        is_first, is_last = b_idx == 0, b_idx == len(DILATIONS) - 1

        def unit(u, carry, dil=dil, has_prev=has_prev, q_row=q_row, k_col=k_col,
                 is_first=is_first, is_last=is_last):
            res = u & (dil - 1)
            i = u >> (dil.bit_length() - 1)
            start = res + dil * blk * i
            if dil == 1:
                start = pl.multiple_of(start, blk)
            if has_prev:
                prev_start = jnp.maximum(start - dil * blk, res)
                lo = jnp.maximum(q_row, jnp.where(i > 0, 0, blk))
                valid = (k_col >= lo) & (k_col <= q_row + blk)
            else:
                valid = k_col <= q_row
            loaded = []
            for p in range(n_pairs):
                q = q_ref[p, rows(start, dil), :] * scale
                kc = k_ref[p, rows(start, dil), :]
                vc = v_ref[p, rows(start, dil), :]
                if has_prev:
                    kc = jnp.concatenate([k_ref[p, rows(prev_start, dil), :], kc], axis=0)
                    vc = jnp.concatenate([v_ref[p, rows(prev_start, dil), :], vc], axis=0)
                old = None
                if not is_first:
                    old = (m_ref[p, rows(start, dil), :], l_ref[p, rows(start, dil), :],
                           acc_ref[p, rows(start, dil), :])
                loaded.append((q, kc, vc, old))
            scores = []
            for q, kc, vc, old in loaded:
                q2 = jnp.concatenate([jnp.where(first, q, 0.0), jnp.where(first, 0.0, q)], axis=0)
                scores.append(_dot_nt(q2.astype(BF16), kc.astype(BF16)))
            results = []
            for (q, kc, vc, old), s in zip(loaded, scores):
                s = jnp.where(valid, s, -jnp.inf)
                m2 = jnp.max(s, axis=-1, keepdims=True)
                e = jnp.exp2(s - m2).astype(BF16)
                v1 = jnp.concatenate([vc.astype(BF16), jnp.ones(vc.shape, BF16)], axis=1)
                pv = _dot(e, v1)
                m_b = jnp.where(first, m2[:blk], m2[blk:])
                l_b = jnp.where(first, pv[:blk, LANES:], pv[blk:, LANES:])
                acc_b = jnp.where(first, pv[:blk, :LANES], pv[blk:, :LANES])
                if not is_first:
                    m_o, l_o, acc_o = old
                    m_n = jnp.maximum(m_o, m_b)
                    w_o, w_b = jnp.exp2(m_o - m_n), jnp.exp2(m_b - m_n)
                    l_b = w_o * l_o + w_b * l_b
                    acc_b = w_o * acc_o + w_b * acc_b
                    m_b = m_n
                results.append((m_b, l_b, acc_b))
            for p, (m_b, l_b, acc_b) in enumerate(results):
                if is_last:
                    o_ref[p, rows(start, dil), :] = acc_b / l_b
                else:
                    m_ref[p, rows(start, dil), :] = m_b
                    l_ref[p, rows(start, dil), :] = l_b
                    acc_ref[p, rows(start, dil), :] = acc_b
            return carry

        lax.fori_loop(0, dil * n_blk, unit, 0, unroll=4)


def _prompt_attention(q3, k3, v3, batch, seq):
    n_pairs = q3.shape[0]
    assert seq % (WINDOW_STEPS * max(DILATIONS)) == 0
    spec = pl.BlockSpec((n_pairs, seq, LANES), lambda b: (0, b, 0))
    return pl.pallas_call(
        _prompt_attn_body,
        grid=(batch,),
        in_specs=[spec, spec, spec],
        out_specs=spec,
        out_shape=jax.ShapeDtypeStruct(q3.shape, F32),
        scratch_shapes=[pltpu.VMEM((n_pairs, seq, LANES), F32)] * 3,
        compiler_params=_params(("arbitrary",)),
        name="prompt_attn",
    )(q3, k3, v3)


def _branch_count(delta):
    cnt = jnp.zeros(delta.shape, F32)
    for dil in DILATIONS:
        hit = (delta >= 0) & (delta <= WINDOW_STEPS * dil) & ((delta & (dil - 1)) == 0)
        cnt = cnt + jnp.where(hit, 1.0, 0.0)
    return cnt


def _cache_attn_body(q_ref, kn_ref, vn_ref, kt_ref, vt_ref, o_ref, *, past, t_new):
    n_pairs = q_ref.shape[0]
    t_c = lax.broadcasted_iota(jnp.int32, (t_new, past), 0)
    j_c = lax.broadcasted_iota(jnp.int32, (t_new, past), 1)
    cnt_c = _branch_count(past + t_c - j_c)
    t_n = lax.broadcasted_iota(jnp.int32, (t_new, t_new), 0)
    j_n = lax.broadcasted_iota(jnp.int32, (t_new, t_new), 1)
    cnt_n = _branch_count(t_n - j_n)
    scale = HEAD_DIM ** -0.5
    for bi in range(kt_ref.shape[1]):
        rs = slice(bi * t_new, (bi + 1) * t_new)
        for p in range(n_pairs):
            q_pair, kn_pair, vn_pair = q_ref[p, rs, :] * scale, kn_ref[p, rs, :], vn_ref[p, rs, :]
            outs = []
            for hh in range(LANES // HEAD_DIM):
                h = p * (LANES // HEAD_DIM) + hh
                sl = slice(hh * HEAD_DIM, (hh + 1) * HEAD_DIM)
                q = q_pair[:, sl]
                s_c = jnp.where(cnt_c > 0, _dot(q.astype(BF16), kt_ref[0, bi, h].astype(BF16)), -jnp.inf)
                s_n = jnp.where(cnt_n > 0, _dot_nt(q, kn_pair[:, sl]), -jnp.inf)
                m = jnp.maximum(jnp.max(s_c, axis=-1, keepdims=True), jnp.max(s_n, axis=-1, keepdims=True))
                p_c = cnt_c * jnp.exp(s_c - m)
                p_n = cnt_n * jnp.exp(s_n - m)
                den = jnp.sum(p_c, axis=-1, keepdims=True) + jnp.sum(p_n, axis=-1, keepdims=True)
                acc = _dot_nt(p_c.astype(BF16), vt_ref[0, bi, h].astype(BF16)) + _dot(p_n, vn_pair[:, sl])
                outs.append(acc / den)
            o_ref[p, rs, :] = jnp.concatenate(outs, axis=-1)


def _cache_attention(q3, kn3, vn3, cache_k, cache_v, layer, batch, seq):
    depth, _, past, n_heads, hd = cache_k.shape
    n_pairs = q3.shape[0]
    bb = CACHE_BATCH if batch % CACHE_BATCH == 0 else 1
    by_head = lambda a: jnp.transpose(a, (0, 1, 3, 4, 2))
    new = pl.BlockSpec((n_pairs, bb * seq, LANES), lambda b: (0, b, 0))
    old = pl.BlockSpec((1, bb, n_heads, hd, past), lambda b: (layer, b, 0, 0, 0))
    return pl.pallas_call(
        functools.partial(_cache_attn_body, past=past, t_new=seq),
        grid=(batch // bb,),
        in_specs=[new, new, new, old, old],
        out_specs=new,
        out_shape=jax.ShapeDtypeStruct(q3.shape, F32),
        compiler_params=_params(("arbitrary",)),
        name="cache_attn",
    )(q3, kn3, vn3, by_head(cache_k), by_head(cache_v))


def _lower_bound(logits, layer):
    e = jnp.exp(logits - jnp.max(logits, axis=0, keepdims=True))
    sm = e / jnp.sum(e, axis=0, keepdims=True)
    lb = jnp.zeros_like(sm[0:1])
    for j in range(1, layer + 1):
        lb = lb + sm[j:j + 1]
    return lb


def _hgrn_body(lbl_ref, nw_ref, q_ref, f_ref, i_ref, g_ref, s0_ref, o_ref, sout_ref,
               st_ref, b2_ref, kk_ref, eb_ref, q_s_ref, v_s_ref, qt_ref, kh_ref, o_s_ref,
               *, layer, n_chunks):
    c = REC_CHUNK
    d_rec = q_ref.shape[-1]
    n_pairs = d_rec // LANES
    rows_in = q_ref.shape[1]
    rows = n_chunks * c
    grp = min(rows, LANES)

    @pl.when(pl.program_id(1) == 0)
    def _():
        st_ref[...] = s0_ref[0]

    def padded(ref):
        x = ref[0]
        if rows_in < rows:
            x = jnp.concatenate([x, jnp.zeros((rows - rows_in, d_rec), F32)], axis=0)
        return x

    lb = _lower_bound(lbl_ref[...], layer)
    log_lb = jnp.log(lb)
    z = padded(f_ref)
    t = jnp.exp(-jnp.abs(z))
    log_sig = jnp.minimum(z, 0.0) - jnp.log(1.0 + t)
    b_term = jnp.log1p(-lb) + log_sig
    log_f = jnp.maximum(log_lb, b_term) + jnp.log(1.0 + jnp.exp(-jnp.abs(log_lb - b_term)))
    inv = 1.0 / (1.0 + t)
    kk = (1.0 - lb) * jnp.where(z > 0.0, t * inv, inv)
    if rows_in < rows:
        live = lax.broadcasted_iota(jnp.int32, (rows, d_rec), 0) < rows_in
        log_f = jnp.where(live, log_f, 0.0)
        kk = jnp.where(live, kk, 0.0)
    r_i = lax.broadcasted_iota(jnp.int32, (grp, grp), 0)
    c_i = lax.broadcasted_iota(jnp.int32, (grp, grp), 1)
    same_chunk = (r_i >> (c.bit_length() - 1)) == (c_i >> (c.bit_length() - 1))
    prefix = jnp.where(same_chunk & (c_i <= r_i), 1.0, 0.0).astype(BF16)
    suffix = jnp.where(same_chunk & (c_i > r_i), 1.0, 0.0).astype(BF16)
    q_all = padded(q_ref)
    for g in range(rows // grp):
        gs = slice(g * grp, (g + 1) * grp)
        parts = _split3(log_f[gs])
        b = sum(_dot(prefix, part) for part in parts)
        r = sum(_dot(suffix, part) for part in parts)
        eb = jnp.exp(b)
        b2_ref[gs, :] = b * LOG2_E
        eb_ref[gs, :] = eb
        qt_ref[gs, :] = (q_all[gs] * eb).astype(BF16)
        kh_ref[gs, :] = (kk[gs] * jnp.exp(r)).astype(BF16)
    kk_ref[...] = kk
    q_s_ref[...] = q_all
    v_s_ref[...] = padded(i_ref)

    l_r = lax.broadcasted_iota(jnp.int32, (LANES, LANES), 0) >> (HEAD_DIM.bit_length() - 1)
    l_c = lax.broadcasted_iota(jnp.int32, (LANES, LANES), 1) >> (HEAD_DIM.bit_length() - 1)
    same_head = l_r == l_c
    head_sum = jnp.where(same_head, 1.0, 0.0).astype(BF16)
    row = lax.broadcasted_iota(jnp.int32, (c, LANES), 0)

    def chunk(ci, carry):
        r0 = pl.multiple_of(ci * c, c)
        rs = pl.ds(r0, c)
        for p in range(n_pairs):
            sl = slice(p * LANES, (p + 1) * LANES)
            b2, q, v, k = b2_ref[rs, sl], q_s_ref[rs, sl], v_s_ref[rs, sl], kk_ref[rs, sl]
            xs = []
            for s_i in range(c):
                cap = jnp.where(row >= s_i, 0.0, -jnp.inf)
                e = jnp.exp2(jnp.minimum(b2 - b2[s_i:s_i + 1], cap))
                xs.append((q * e) * k[s_i:s_i + 1])
            a_rep = _dot(jnp.concatenate(xs, axis=0).astype(BF16), head_sum)
            o = a_rep[0:c] * v[0:1]
            for s_i in range(1, c):
                o = o + a_rep[s_i * c:(s_i + 1) * c] * v[s_i:s_i + 1]
            st = st_ref[p]
            o = o + _dot_nt(qt_ref[rs, sl], st.astype(BF16))
            upd = _dot_tn(v.astype(BF16), kh_ref[rs, sl])
            st_ref[p] = st * eb_ref[rs, sl][c - 1:c] + jnp.where(same_head, upd, 0.0)
            o_s_ref[rs, sl] = o
        return carry

    lax.fori_loop(0, n_chunks, chunk, 0, unroll=UNROLL if n_chunks % UNROLL == 0 else 1)

    g_all = padded(g_ref)
    outs = []
    for p in range(n_pairs):
        sl = slice(p * LANES, (p + 1) * LANES)
        o = o_s_ref[:, sl]
        sq_hi, sq_mid, _ = _split3(o * o)
        ms = (_dot(sq_hi, head_sum) + _dot(sq_mid, head_sum)) * (1.0 / HEAD_DIM)
        outs.append(o * lax.rsqrt(ms + NORM_EPS) * nw_ref[:, sl] * _silu(g_all[:, sl]))
    o_ref[0] = jnp.concatenate(outs, axis=-1)[:rows_in]

    @pl.when(pl.program_id(1) == pl.num_programs(1) - 1)
    def _():
        sout_ref[0] = st_ref[...]


def _pair_state(s):
    b, h, kd, vd = s.shape
    st = jnp.swapaxes(s, 2, 3).reshape(b, h // 2, 2, vd, kd)
    z = jnp.zeros_like(st[:, :, 0])
    top = jnp.concatenate([st[:, :, 0], z], axis=-1)
    bot = jnp.concatenate([z, st[:, :, 1]], axis=-1)
    return jnp.concatenate([top, bot], axis=-2)


def _unpair_state(sp):
    b, hp = sp.shape[:2]
    s0 = sp[:, :, :HEAD_DIM, :HEAD_DIM]
    s1 = sp[:, :, HEAD_DIM:, HEAD_DIM:]
    st = jnp.stack([s0, s1], axis=2).reshape(b, 2 * hp, HEAD_DIM, HEAD_DIM)
    return jnp.swapaxes(st, 2, 3)


def _conv_block(bc_ref, cc_ref, xc_ref, buf_ref, w_ref, b_ref, oc_ref, new_ref, up_ref):
    tb = cc_ref.shape[1]
    width = w_ref.shape[0]
    head = CONV_PAD - (width - 1)

    @pl.when(pl.program_id(1) == 0)
    def _():
        up_ref[pl.ds(head, width - 1), :] = buf_ref[0]

    u = cc_ref[0] * xc_ref[0]
    up_ref[pl.ds(CONV_PAD, tb), :] = u
    y = u * w_ref[width - 1:width, :] + b_ref[...]
    for i in range(width - 1):
        y = y + up_ref[pl.ds(head + i, tb), :] * w_ref[i:i + 1, :]
    oc_ref[0] = bc_ref[0] * y
    tail = up_ref[pl.ds(CONV_PAD + tb - (width - 1), width - 1), :]
    up_ref[pl.ds(head, width - 1), :] = tail
    new_ref[0] = tail


def _kv_layout_block(ins, outs):
    depth = len(ins) // len(outs)
    for which, o_ref in enumerate(outs):
        for l in range(depth):
            src = ins[which * depth + l]
            for p in range(src.shape[0]):
                o_ref[l, 0, p * LANES:(p + 1) * LANES, :] = src[p].T


N_HGRN_IN, N_CONV_IN, N_HGRN_SCRATCH = 7, 6, 9


def _mixers_body(*refs, layer, n_chunks, n_kv):
    hgrn_in, refs = refs[:N_HGRN_IN], refs[N_HGRN_IN:]
    conv_in, refs = refs[:N_CONV_IN], refs[N_CONV_IN:]
    kv_in, refs = refs[:n_kv], refs[n_kv:]
    (o_ref, sout_ref, oc_ref, new_ref), refs = refs[:4], refs[4:]
    kv_out, refs = refs[:min(n_kv, 2)], refs[min(n_kv, 2):]
    hgrn_scratch, up_ref = refs[:N_HGRN_SCRATCH], refs[N_HGRN_SCRATCH]
    _hgrn_body(*hgrn_in, o_ref, sout_ref, *hgrn_scratch, layer=layer, n_chunks=n_chunks)
    _conv_block(*conv_in, oc_ref, new_ref, up_ref)
    if n_kv:
        _kv_layout_block(kv_in, kv_out)


def _mixers(qr, fr, ir, gr, s0, lb_logits, norm_w, bc, cc, xc, conv0, conv_w, conv_b, kv_parts, layer, batch, seq):
    d_rec, d_conv = qr.shape[-1], bc.shape[-1]
    n_pairs = d_rec // LANES
    width = conv_w.shape[0]
    tb = min(seq, ROW_TILE)
    assert seq % tb == 0 and tb % REC_CHUNK == 0 and tb % LANES == 0
    n_t = seq // tb
    n_chunks = tb // REC_CHUNK
    tok = lambda w: pl.BlockSpec((1, tb, w), lambda b, t: (b, t, 0))
    whole = lambda a: pl.BlockSpec(a.shape, lambda b, t: (0,) * a.ndim)
    state = pl.BlockSpec((1, n_pairs, LANES, LANES), lambda b, t: (b, 0, 0, 0))
    tail = pl.BlockSpec((1, width - 1, d_conv), lambda b, t: (b, 0, 0))
    norm_w2, conv_b2 = norm_w.reshape(1, d_rec), conv_b.reshape(1, d_conv)
    view = lambda a: a.reshape(batch, seq, a.shape[-1])
    ins = [lb_logits, norm_w2, view(qr), view(fr), view(ir), view(gr), _pair_state(s0),
           view(bc), view(cc), view(xc), conv0, conv_w, conv_b2]
    in_specs = [whole(lb_logits), whole(norm_w2), tok(d_rec), tok(d_rec), tok(d_rec), tok(d_rec), state,
                tok(d_conv), tok(d_conv), tok(d_conv), tail, whole(conv_w), whole(conv_b2)]
    out_specs = [tok(d_rec), state, tok(d_conv), tail]
    out_shape = [jax.ShapeDtypeStruct((batch, seq, d_rec), F32),
                 jax.ShapeDtypeStruct((batch, n_pairs, LANES, LANES), F32),
                 jax.ShapeDtypeStruct((batch, seq, d_conv), F32),
                 jax.ShapeDtypeStruct((batch, width - 1, d_conv), F32)]
    if kv_parts:
        depth = len(kv_parts) // 2
        d_att = kv_parts[0].shape[0] * LANES
        ins += list(kv_parts)
        in_specs += [pl.BlockSpec((d_att // LANES, tb, LANES), lambda b, t: (0, b * n_t + t, 0))] * len(kv_parts)
        out_specs += [pl.BlockSpec((depth, 1, d_att, tb), lambda b, t: (0, b, 0, t))] * 2
        out_shape += [jax.ShapeDtypeStruct((depth, batch, d_att, seq), F32)] * 2
    outs = pl.pallas_call(
        functools.partial(_mixers_body, layer=layer, n_chunks=n_chunks, n_kv=len(kv_parts)),
        grid=(batch, n_t),
        in_specs=in_specs,
        out_specs=out_specs,
        out_shape=out_shape,
        scratch_shapes=[pltpu.VMEM((n_pairs, LANES, LANES), F32)]
        + [pltpu.VMEM((tb, d_rec), F32)] * 5 + [pltpu.VMEM((tb, d_rec), BF16)] * 2
        + [pltpu.VMEM((tb, d_rec), F32), pltpu.VMEM((CONV_PAD + tb, d_conv), F32)],
        compiler_params=_params(("arbitrary", "arbitrary")),
        name="mixers",
    )(*ins)
    o, s_new, oc, conv_new = outs[:4]
    kv_t = outs[4:]
    if kv_parts:
        heads = lambda a: jnp.transpose(a.reshape(depth, batch, d_att // HEAD_DIM, HEAD_DIM, seq), (0, 1, 4, 2, 3))
        kv_t = [heads(a) for a in kv_t]
    return (o.reshape(batch * seq, d_rec), _unpair_state(s_new), oc.reshape(batch * seq, d_conv), conv_new, kv_t)


def _hgrn_step_body(lblt_ref, nwt_ref, q_ref, f_ref, i_ref, g_ref, s0_ref, o_ref, sout_ref, *, layer):
    n_t, hd, _ = q_ref.shape
    logits = lblt_ref[...]
    e = jnp.exp(logits - jnp.max(logits, axis=1, keepdims=True))
    sm = e / jnp.sum(e, axis=1, keepdims=True)
    lb = jnp.zeros_like(sm[:, 0:1])
    for j in range(1, layer + 1):
        lb = lb + sm[:, j:j + 1]
    for t in range(n_t):
        z = f_ref[t]
        f_t = lb + (1.0 - lb) * jax.nn.sigmoid(z)
        k_t = (1.0 - lb) * jax.nn.sigmoid(-z)
        q_t, v_t = q_ref[t], i_ref[t]
        src = s0_ref if t == 0 else sout_ref
        o_t = jnp.zeros(v_t.shape, F32)
        for k in range(hd):
            s_k = f_t[k:k + 1] * src[0, 0, k] + k_t[k:k + 1] * v_t
            sout_ref[0, 0, k] = s_k
            o_t = o_t + s_k * q_t[k:k + 1]
        ms = jnp.mean(o_t * o_t, axis=0, keepdims=True)
        o_ref[t] = o_t * lax.rsqrt(ms + NORM_EPS) * nwt_ref[...] * _silu(g_ref[t])


def _hgrn_step(qr, fr, ir, gr, state_all, lb_logits, norm_w, layer, batch, seq):
    d_rec = qr.shape[-1]
    n_heads = d_rec // HEAD_DIM
    to_lanes = lambda a: jnp.transpose(a.reshape(batch, seq, d_rec), (1, 2, 0))
    tok = pl.BlockSpec((seq, HEAD_DIM, batch), lambda h: (0, h, 0))
    state_in = pl.BlockSpec((1, 1, HEAD_DIM, HEAD_DIM, batch), lambda h: (layer, h, 0, 0, 0))
    state_out = pl.BlockSpec((1, 1, HEAD_DIM, HEAD_DIM, batch), lambda h: (0, h, 0, 0, 0))
    o_t, s_new = pl.pallas_call(
        functools.partial(_hgrn_step_body, layer=layer),
        grid=(n_heads,),
        in_specs=[pl.BlockSpec((HEAD_DIM, lb_logits.shape[0]), lambda h: (h, 0)),
                  pl.BlockSpec((HEAD_DIM, 1), lambda h: (h, 0)),
                  tok, tok, tok, tok, state_in],
        out_specs=[tok, state_out],
        out_shape=[jax.ShapeDtypeStruct((seq, d_rec, batch), F32),
                   jax.ShapeDtypeStruct((1, n_heads, HEAD_DIM, HEAD_DIM, batch), F32)],
        compiler_params=_params(("arbitrary",)),
        name="hgrn_step",
    )(lb_logits.T, norm_w.reshape(d_rec, 1), to_lanes(qr), to_lanes(fr), to_lanes(ir), to_lanes(gr),
      jnp.transpose(state_all, (0, 2, 3, 4, 1)))
    o = jnp.transpose(o_t, (2, 0, 1)).reshape(batch * seq, d_rec)
    return o, jnp.transpose(s_new[0], (3, 0, 1, 2))


CONV_PAD = 8


def _conv_body(bc_ref, cc_ref, xc_ref, buf_ref, w_ref, b_ref, oc_ref, new_ref, up_ref):
    seq = cc_ref.shape[1]
    width = w_ref.shape[0]
    u = cc_ref[0] * xc_ref[0]
    up_ref[pl.ds(CONV_PAD, seq), :] = u
    up_ref[pl.ds(CONV_PAD - (width - 1), width - 1), :] = buf_ref[0]
    y = u * w_ref[width - 1:width, :] + b_ref[...]
    for i in range(width - 1):
        y = y + up_ref[pl.ds(CONV_PAD - (width - 1) + i, seq), :] * w_ref[i:i + 1, :]
    oc_ref[0] = bc_ref[0] * y
    new_ref[0] = up_ref[pl.ds(CONV_PAD + seq - (width - 1), width - 1), :]


def _short_conv(bc, cc, xc, buf, conv_w, conv_b, batch, seq):
    d_conv = bc.shape[-1]
    width = conv_w.shape[0]
    view = lambda a: a.reshape(batch, seq, d_conv)
    tok = pl.BlockSpec((1, seq, d_conv), lambda b: (b, 0, 0))
    tail = pl.BlockSpec((1, width - 1, d_conv), lambda b: (b, 0, 0))
    oc, new = pl.pallas_call(
        _conv_body,
        grid=(batch,),
        in_specs=[tok, tok, tok, tail,
                  pl.BlockSpec((width, d_conv), lambda b: (0, 0)),
                  pl.BlockSpec((1, d_conv), lambda b: (0, 0))],
        out_specs=[tok, tail],
        out_shape=[jax.ShapeDtypeStruct((batch, seq, d_conv), F32),
                   jax.ShapeDtypeStruct((batch, width - 1, d_conv), F32)],
        scratch_shapes=[pltpu.VMEM((CONV_PAD + seq, d_conv), F32)],
        compiler_params=_params(("arbitrary",)),
        name="short_conv",
    )(view(bc), view(cc), view(xc), buf, conv_w, conv_b.reshape(1, d_conv))
    return oc.reshape(batch * seq, d_conv), new


def _mix_ffn_body(oa_ref, or_ref, oc_ref, x_ref, g1_ref, sh_ref, sc_ref, g2_ref, w_ref, wi_ref, wo_ref,
                  fw_ref, o_ref, *, bb, tt, final, cuts):
    d = x_ref.shape[-1]
    d_ff = wo_ref.shape[0]
    n_pairs = oa_ref.shape[0]
    d_att, d_rec = n_pairs * LANES, or_ref.shape[-1]
    mix = (_dot(or_ref[...].astype(BF16), w_ref[d_att:d_att + d_rec, :])
           + _dot(oc_ref[...].astype(BF16), w_ref[d_att + d_rec:, :]))
    for p in range(n_pairs):
        mix = mix + _dot(oa_ref[p].astype(BF16), w_ref[p * LANES:(p + 1) * LANES, :])
    x = x_ref[...].reshape(bb, tt, d) + g1_ref[...] * mix.reshape(bb, tt, d)
    h = (_rms(x) * (1.0 + sc_ref[...]) + sh_ref[...]).reshape(bb * tt, d).astype(BF16)
    acc = None
    for c0, c1 in zip(cuts[:-1], cuts[1:]):
        a = (_silu(_dot(h, wi_ref[:, c0:c1])) * _dot(h, wi_ref[:, d_ff + c0:d_ff + c1])).astype(BF16)
        part = _dot(a, wo_ref[c0:c1, :])
        acc = part if acc is None else acc + part
    x = x + g2_ref[...] * acc.reshape(bb, tt, d)
    if final:
        x = _rms(x) * fw_ref[...]
    o_ref[...] = x.reshape(bb * tt, d)


def _mix_ffn(oa3, o_r, oc, x2, g1, shift, scale, g2, w_out_bf, w_ffn_in_bf, w_ffn_out_bf, final_w, batch, seq, final):
    n, d = x2.shape
    d_ff = w_ffn_out_bf.shape[0]
    bb, tt = _token_tile(batch, seq)
    tm = bb * tt
    n_t = seq // tt
    assert d_ff % MXU_WIDTH == 0
    n_cut = -(-d_ff // FFN_COLS)
    cuts = tuple(round(i * (d_ff // MXU_WIDTH) / n_cut) * MXU_WIDTH for i in range(n_cut + 1))
    tok = lambda a: pl.BlockSpec((tm, a.shape[-1]), lambda i: (i, 0))
    mod_spec = pl.BlockSpec((bb, 1, d), lambda i: (i // n_t, 0, 0))
    held = lambda a: pl.BlockSpec(a.shape, lambda i: (0,) * a.ndim, pipeline_mode=pl.Buffered(1))
    return pl.pallas_call(
        functools.partial(_mix_ffn_body, bb=bb, tt=tt, final=final, cuts=cuts),
        grid=(n // tm,),
        in_specs=[pl.BlockSpec((oa3.shape[0], tm, LANES), lambda i: (0, i, 0)), tok(o_r), tok(oc), tok(x2)]
        + [mod_spec] * 4
        + [held(w_out_bf), held(w_ffn_in_bf), held(w_ffn_out_bf), pl.BlockSpec((1, d), lambda i: (0, 0))],
        out_specs=pl.BlockSpec((tm, d), lambda i: (i, 0)),
        out_shape=jax.ShapeDtypeStruct((n, d), F32),
        compiler_params=_params(("arbitrary",)),
        name="mix_ffn",
    )(oa3, o_r, oc, x2, g1, shift, scale, g2, w_out_bf, w_ffn_in_bf, w_ffn_out_bf, final_w.reshape(1, d))


def _trunk(x, mods, pos, cache_k, cache_v, state_hgrn, state_conv, weights):
    (w_in, conv_w, conv_b, lb_logits, hgrn_norm_w, w_out, w_ffn_in, w_ffn_out, final_norm_w) = weights
    batch, seq, d = x.shape
    depth = w_in.shape[0]
    d_conv = conv_w.shape[-1]
    d_att = d_rec = (w_in.shape[-1] - 3 * d_conv) // 7
    n_heads = d_att // HEAD_DIM
    widths = (d_att,) * 3 + (d_rec,) * 4 + (d_conv,) * 3
    bb, tt = _token_tile(batch, seq)
    cos_t, sin_t = _rope_tables(pos, n_heads)
    if bb > 1:
        cos_t, sin_t = jnp.tile(cos_t, (bb, 1)), jnp.tile(sin_t, (bb, 1))
    x2 = x.reshape(batch * seq, d)
    ks, vs, hs, cs = [], [], [], []
    for l in range(depth):
        sh1, sc1, g1, sh2, sc2, g2 = [m.reshape(batch, 1, d) for m in jnp.split(mods[l], 6, axis=-1)]
        qa, ka, va, qr, fr, ir, gr, bc, cc, xc = _in_proj(
            x2, sh1, sc1, w_in[l], cos_t, sin_t, batch, seq, widths, 3)
        if cache_k is None:
            oa = _prompt_attention(qa, ka, va, batch, seq)
            s0 = jnp.zeros((batch, n_heads, HEAD_DIM, HEAD_DIM), F32)
            conv0 = jnp.zeros((batch, conv_w.shape[1] - 1, d_conv), F32)
        else:
            oa = _cache_attention(qa, ka, va, cache_k, cache_v, l, batch, seq)
            conv0 = state_conv[l]
        ks.append(ka)
        vs.append(va)
        if cache_k is None:
            kv_parts = tuple(ks) + tuple(vs) if l == depth - 1 and seq % LANES == 0 else ()
            o_r, s_new, oc, conv_new, kv_t = _mixers(qr, fr, ir, gr, s0, lb_logits, hgrn_norm_w[l], bc, cc, xc, conv0,
                                                     conv_w[l], conv_b[l], kv_parts, l, batch, seq)
        else:
            o_r, s_new = _hgrn_step(qr, fr, ir, gr, state_hgrn, lb_logits, hgrn_norm_w[l], l, batch, seq)
            oc, conv_new = _short_conv(bc, cc, xc, conv0, conv_w[l], conv_b[l], batch, seq)
            kv_t = []
        x2 = _mix_ffn(oa, o_r, oc, x2, g1, sh2, sc2, g2, w_out[l], w_ffn_in[l], w_ffn_out[l], final_norm_w,
                      batch, seq, l == depth - 1)
        hs.append(s_new)
        cs.append(conv_new)
    keep = min(WINDOW_STEPS * max(DILATIONS), seq)
    if kv_t:
        k_out, v_out = kv_t
    else:
        unpair = lambda parts: jnp.transpose(
            jnp.stack(parts).reshape(depth, n_heads // 2, batch, seq, 2, HEAD_DIM),
            (0, 2, 3, 1, 4, 5)).reshape(depth, batch, seq, n_heads, HEAD_DIM)
        k_out, v_out = unpair(ks), unpair(vs)
    return (x2.reshape(batch, seq, d), k_out[:, :, seq - keep:], v_out[:, :, seq - keep:],
            jnp.stack(hs), jnp.stack(cs))


def kernel(x_prompt, x_sample, cache_k, cache_v, state_hgrn, state_conv, c_prompt, c_sample,
           w_ada, b_ada, w_in, conv_w, conv_b, hgrn_lb_logits, hgrn_norm_w, w_out,
           w_ffn_in, w_ffn_out, final_norm_w):
    n_prompt = c_prompt.shape[0]
    past_len = PAST_LEN
    assert cache_k.shape[2] == min(WINDOW_STEPS * max(DILATIONS), PAST_LEN) == PAST_LEN
    mods = _ada(jnp.concatenate([c_prompt, c_sample], axis=0), w_ada, b_ada)
    weights = (w_in.astype(BF16), conv_w, conv_b, hgrn_lb_logits.astype(F32), hgrn_norm_w,
               w_out.astype(BF16), w_ffn_in.astype(BF16), w_ffn_out.astype(BF16), final_norm_w)
    pos_p = jnp.arange(x_prompt.shape[1], dtype=jnp.int32)
    pos_s = past_len + jnp.arange(x_sample.shape[1], dtype=jnp.int32)
    out_p = _trunk(x_prompt, mods[:, :n_prompt], pos_p, None, None, None, None, weights)
    out_s = _trunk(x_sample, mods[:, n_prompt:], pos_s, cache_k, cache_v, state_hgrn, state_conv, weights)
    return (out_p[0], out_s[0]) + out_p[1:] + out_s[1:]
```

```python
import functools
import math

import jax
import jax.numpy as jnp
from jax import lax
from jax.experimental import pallas as pl
from jax.experimental.pallas import tpu as pltpu

F32 = jnp.float32
BF16 = jnp.bfloat16

HEAD_DIM = 64
LANES = 128
DILATIONS = (1, 4, 16)
WINDOW_STEPS = 128
PAST_LEN = 2048
ROPE_THETA = 10000.0
NORM_EPS = 1e-6
REC_CHUNK = 16
UNROLL = 8
LOG2_E = 1.4426950408889634
ROW_TILE = 512
MXU_WIDTH = 256
FFN_COLS = 1536
CACHE_BATCH = 2
VMEM_LIMIT = 48 * 1024 * 1024


def _params(sem):
    return pltpu.CompilerParams(dimension_semantics=sem, vmem_limit_bytes=VMEM_LIMIT)


def _dot(a, b):
    return jnp.dot(a, b, preferred_element_type=F32)


def _dot_nt(a, b):
    return lax.dot_general(a, b, (((1,), (1,)), ((), ())), preferred_element_type=F32)


def _dot_tn(a, b):
    return lax.dot_general(a, b, (((0,), (0,)), ((), ())), preferred_element_type=F32)


def _split3(x):
    hi = x.astype(BF16)
    r = x - hi.astype(F32)
    mid = r.astype(BF16)
    lo = (r - mid.astype(F32)).astype(BF16)
    return hi, mid, lo


def _silu(x):
    return x * jax.nn.sigmoid(x)


def _rms(x):
    return x * lax.rsqrt(jnp.mean(x * x, axis=-1, keepdims=True) + NORM_EPS)


def _token_tile(batch, seq):
    if seq >= ROW_TILE:
        assert seq % ROW_TILE == 0
        return 1, ROW_TILE
    assert ROW_TILE % seq == 0 and seq % 8 == 0
    bb = min(batch, ROW_TILE // seq)
    assert batch % bb == 0
    return bb, seq


def _ada_body(*refs):
    n = (len(refs) - 2) // 2
    c_refs, (w_ref, b_ref), o_refs = refs[:n], refs[n:n + 2], refs[n + 2:]
    w_hi, w_mid, _ = _split3(w_ref[0])
    for c_ref, o_ref in zip(c_refs, o_refs):
        s_hi, s_mid, _ = _split3(_silu(c_ref[...]))
        o_ref[0] = _dot(s_hi, w_hi) + _dot(s_hi, w_mid) + _dot(s_mid, w_hi) + b_ref[0]


def _ada(cs, w_ada, b_ada):
    depth, d, n6 = w_ada.shape
    tn = 512
    return pl.pallas_call(
        _ada_body,
        grid=(depth, n6 // tn),
        in_specs=[pl.BlockSpec(c.shape, lambda l, j: (0, 0)) for c in cs]
        + [pl.BlockSpec((1, d, tn), lambda l, j: (l, 0, j)),
           pl.BlockSpec((1, 1, tn), lambda l, j: (l, 0, j))],
        out_specs=[pl.BlockSpec((1, c.shape[0], tn), lambda l, j: (l, 0, j)) for c in cs],
        out_shape=[jax.ShapeDtypeStruct((depth, c.shape[0], n6), F32) for c in cs],
        compiler_params=_params(("arbitrary", "arbitrary")),
        name="ada",
    )(*cs, w_ada, b_ada.reshape(depth, 1, n6))


def _mod_planes(mods, d):
    depth, batch, _ = mods.shape
    return jnp.transpose(mods.reshape(depth, batch, 6, 1, d), (0, 2, 1, 3, 4))


def _mod_spec(layer, which, bb, d, n_t):
    return pl.BlockSpec((None, None, bb, 1, d), lambda i: (layer, which, i // n_t, 0, 0))


SHIFT1, SCALE1, GATE1, SHIFT2, SCALE2, GATE2 = range(6)


def _rope_tables(pos, n_heads):
    half = HEAD_DIM // 2
    freqs = ROPE_THETA ** (-jnp.arange(half, dtype=F32) / half)
    ang = pos.astype(F32)[:, None] * freqs[None, :]
    cos, sin = jnp.cos(ang), jnp.sin(ang)
    cos_t = jnp.tile(jnp.concatenate([cos, cos], axis=-1), (1, n_heads))
    sin_t = jnp.tile(jnp.concatenate([-sin, sin], axis=-1), (1, n_heads))
    return cos_t, sin_t


def _rope(x, cos, sin_signed):
    outs = []
    for c in range(x.shape[-1] // LANES):
        sl = slice(c * LANES, (c + 1) * LANES)
        xc = x[:, sl]
        lane = lax.broadcasted_iota(jnp.int32, xc.shape, 1)
        ahead = pltpu.roll(xc, LANES - HEAD_DIM // 2, axis=1)
        behind = pltpu.roll(xc, HEAD_DIM // 2, axis=1)
        rot = jnp.where((lane & (HEAD_DIM // 2)) == 0, ahead, behind)
        outs.append(xc * cos[:, sl] + rot * sin_signed[:, sl])
    return jnp.concatenate(outs, axis=-1)


def _in_proj_body(x_ref, sh_ref, sc_ref, w_ref, cos_ref, sin_ref, *out_refs, bb, tt, widths, n_paired, runs):
    d = x_ref.shape[-1]
    x = x_ref[...].reshape(bb, tt, d)
    h = _rms(x) * (1.0 + sc_ref[...]) + sh_ref[...]
    hb = h.reshape(bb * tt, d).astype(BF16)
    for first_out, n_out in runs:
        off, wd = sum(widths[:first_out]), widths[first_out]
        group = _dot(hb, w_ref[:, off:off + wd * n_out])
        for n in range(n_out):
            i = first_out + n
            p = group[:, n * wd:(n + 1) * wd]
            if i < 2:
                p = _rope(p, cos_ref[...], sin_ref[...])
            if i < n_paired:
                for c in range(wd // LANES):
                    out_refs[i][c] = p[:, c * LANES:(c + 1) * LANES]
            else:
                out_refs[i][...] = p


def _equal_runs(widths):
    runs = []
    for i, wd in enumerate(widths):
        if runs and widths[runs[-1][0]] == wd:
            runs[-1][1] += 1
        else:
            runs.append([i, 1])
    return tuple((a, n) for a, n in runs)


def _in_proj(x2, planes, w_in_bf, layer, cos_t, sin_t, batch, seq, widths, n_paired):
    n, d = x2.shape
    bb, tt = _token_tile(batch, seq)
    tm = bb * tt
    n_t = seq // tt
    table_blocks = cos_t.shape[0] // tm
    body = functools.partial(_in_proj_body, bb=bb, tt=tt, widths=widths, n_paired=n_paired, runs=_equal_runs(widths))
    tab_spec = pl.BlockSpec((tm, cos_t.shape[1]), lambda i: (i % table_blocks, 0))
    out_specs, out_shape = [], []
    for idx, wd in enumerate(widths):
        if idx < n_paired:
            out_specs.append(pl.BlockSpec((wd // LANES, tm, LANES), lambda i: (0, i, 0)))
            out_shape.append(jax.ShapeDtypeStruct((wd // LANES, n, LANES), F32))
        else:
            out_specs.append(pl.BlockSpec((tm, wd), lambda i: (i, 0)))
            out_shape.append(jax.ShapeDtypeStruct((n, wd), F32))
    return pl.pallas_call(
        body,
        grid=(n // tm,),
        in_specs=[pl.BlockSpec((tm, d), lambda i: (i, 0)),
                  _mod_spec(layer, SHIFT1, bb, d, n_t), _mod_spec(layer, SCALE1, bb, d, n_t),
                  pl.BlockSpec((None,) + w_in_bf.shape[1:], lambda i: (layer, 0, 0)), tab_spec, tab_spec],
        out_specs=out_specs,
        out_shape=out_shape,
        compiler_params=_params(("arbitrary",)),
        name="in_proj",
    )(x2, planes, planes, w_in_bf, cos_t, sin_t)


def _prompt_attn_body(q_ref, k_ref, v_ref, o_ref, m_ref, l_ref, acc_ref):
    n_pairs, seq, _ = q_ref.shape
    blk = WINDOW_STEPS
    lane = lax.broadcasted_iota(jnp.int32, (blk, LANES), 1)
    first = lane < HEAD_DIM
    scale = HEAD_DIM ** -0.5 * LOG2_E

    def rows(start, dil):
        return pl.ds(start, blk) if dil == 1 else pl.ds(start, blk, stride=dil)

    for b_idx, dil in enumerate(DILATIONS):
        n_blk = seq // dil // blk
        has_prev = n_blk > 1
        n_keys = 2 * blk if has_prev else blk
        q_row = lax.broadcasted_iota(jnp.int32, (2 * blk, n_keys), 0) & (blk - 1)
        k_col = lax.broadcasted_iota(jnp.int32, (2 * blk, n_keys), 1)
        is_first, is_last = b_idx == 0, b_idx == len(DILATIONS) - 1

        def unit(u, carry, dil=dil, has_prev=has_prev, q_row=q_row, k_col=k_col,
                 is_first=is_first, is_last=is_last):
            res = u & (dil - 1)
            i = u >> (dil.bit_length() - 1)
            start = res + dil * blk * i
            if dil == 1:
                start = pl.multiple_of(start, blk)
            if has_prev:
                prev_start = jnp.maximum(start - dil * blk, res)
                lo = jnp.maximum(q_row, jnp.where(i > 0, 0, blk))
                valid = (k_col >= lo) & (k_col <= q_row + blk)
            else:
                valid = k_col <= q_row
            loaded = []
            for p in range(n_pairs):
                q = q_ref[p, rows(start, dil), :] * scale
                kc = k_ref[p, rows(start, dil), :]
                vc = v_ref[p, rows(start, dil), :]
                if has_prev:
                    kc = jnp.concatenate([k_ref[p, rows(prev_start, dil), :], kc], axis=0)
                    vc = jnp.concatenate([v_ref[p, rows(prev_start, dil), :], vc], axis=0)
                old = None
                if not is_first:
                    old = (m_ref[p, rows(start, dil), :], l_ref[p, rows(start, dil), :],
                           acc_ref[p, rows(start, dil), :])
                loaded.append((q, kc, vc, old))
            scores = []
            for q, kc, vc, old in loaded:
                q2 = jnp.concatenate([jnp.where(first, q, 0.0), jnp.where(first, 0.0, q)], axis=0)
                scores.append(_dot_nt(q2.astype(BF16), kc.astype(BF16)))
            results = []
            for (q, kc, vc, old), s in zip(loaded, scores):
                s = jnp.where(valid, s, -jnp.inf)
                m2 = jnp.max(s, axis=-1, keepdims=True)
                e = jnp.exp2(s - m2).astype(BF16)
                v1 = jnp.concatenate([vc.astype(BF16), jnp.ones(vc.shape, BF16)], axis=1)
                pv = _dot(e, v1)
                m_b = jnp.where(first, m2[:blk], m2[blk:])
                l_b = jnp.where(first, pv[:blk, LANES:], pv[blk:, LANES:])
                acc_b = jnp.where(first, pv[:blk, :LANES], pv[blk:, :LANES])
                if not is_first:
                    m_o, l_o, acc_o = old
                    m_n = jnp.maximum(m_o, m_b)
                    w_o, w_b = jnp.exp2(m_o - m_n), jnp.exp2(m_b - m_n)
                    l_b = w_o * l_o + w_b * l_b
                    acc_b = w_o * acc_o + w_b * acc_b
                    m_b = m_n
                results.append((m_b, l_b, acc_b))
            for p, (m_b, l_b, acc_b) in enumerate(results):
                if is_last:
                    o_ref[p, rows(start, dil), :] = acc_b / l_b
                else:
                    m_ref[p, rows(start, dil), :] = m_b
                    l_ref[p, rows(start, dil), :] = l_b
                    acc_ref[p, rows(start, dil), :] = acc_b
            return carry

        lax.fori_loop(0, dil * n_blk, unit, 0, unroll=4)


def _prompt_attention(q3, k3, v3, batch, seq):
    n_pairs = q3.shape[0]
    assert seq % (WINDOW_STEPS * max(DILATIONS)) == 0
    spec = pl.BlockSpec((n_pairs, seq, LANES), lambda b: (0, b, 0))
    return pl.pallas_call(
        _prompt_attn_body,
        grid=(batch,),
        in_specs=[spec, spec, spec],
        out_specs=spec,
        out_shape=jax.ShapeDtypeStruct(q3.shape, F32),
        scratch_shapes=[pltpu.VMEM((n_pairs, seq, LANES), F32)] * 3,
        compiler_params=_params(("arbitrary",)),
        name="prompt_attn",
    )(q3, k3, v3)


def _branch_count(delta):
    cnt = jnp.zeros(delta.shape, F32)
    for dil in DILATIONS:
        hit = (delta >= 0) & (delta <= WINDOW_STEPS * dil) & ((delta & (dil - 1)) == 0)
        cnt = cnt + jnp.where(hit, 1.0, 0.0)
    return cnt


def _cache_attn_body(q_ref, kn_ref, vn_ref, kt_ref, vt_ref, o_ref, *, past, t_new):
    n_pairs = q_ref.shape[0]
    t_c = lax.broadcasted_iota(jnp.int32, (t_new, past), 0)
    j_c = lax.broadcasted_iota(jnp.int32, (t_new, past), 1)
    cnt_c = _branch_count(past + t_c - j_c)
    t_n = lax.broadcasted_iota(jnp.int32, (t_new, t_new), 0)
    j_n = lax.broadcasted_iota(jnp.int32, (t_new, t_new), 1)
    cnt_n = _branch_count(t_n - j_n)
    scale = HEAD_DIM ** -0.5
    for bi in range(kt_ref.shape[1]):
        rs = slice(bi * t_new, (bi + 1) * t_new)
        for p in range(n_pairs):
            q_pair, kn_pair, vn_pair = q_ref[p, rs, :] * scale, kn_ref[p, rs, :], vn_ref[p, rs, :]
            outs = []
            for hh in range(LANES // HEAD_DIM):
                h = p * (LANES // HEAD_DIM) + hh
                sl = slice(hh * HEAD_DIM, (hh + 1) * HEAD_DIM)
                q = q_pair[:, sl]
                s_c = jnp.where(cnt_c > 0, _dot(q.astype(BF16), kt_ref[0, bi, h].astype(BF16)), -jnp.inf)
                s_n = jnp.where(cnt_n > 0, _dot_nt(q, kn_pair[:, sl]), -jnp.inf)
                m = jnp.maximum(jnp.max(s_c, axis=-1, keepdims=True), jnp.max(s_n, axis=-1, keepdims=True))
                p_c = cnt_c * jnp.exp(s_c - m)
                p_n = cnt_n * jnp.exp(s_n - m)
                den = jnp.sum(p_c, axis=-1, keepdims=True) + jnp.sum(p_n, axis=-1, keepdims=True)
                acc = _dot_nt(p_c.astype(BF16), vt_ref[0, bi, h].astype(BF16)) + _dot(p_n, vn_pair[:, sl])
                outs.append(acc / den)
            o_ref[p, rs, :] = jnp.concatenate(outs, axis=-1)


def _cache_attention(q3, kn3, vn3, cache_k, cache_v, layer, batch, seq):
    depth, _, past, n_heads, hd = cache_k.shape
    n_pairs = q3.shape[0]
    bb = CACHE_BATCH if batch % CACHE_BATCH == 0 else 1
    by_head = lambda a: jnp.transpose(a, (0, 1, 3, 4, 2))
    new = pl.BlockSpec((n_pairs, bb * seq, LANES), lambda b: (0, b, 0))
    old = pl.BlockSpec((1, bb, n_heads, hd, past), lambda b: (layer, b, 0, 0, 0))
    return pl.pallas_call(
        functools.partial(_cache_attn_body, past=past, t_new=seq),
        grid=(batch // bb,),
        in_specs=[new, new, new, old, old],
        out_specs=new,
        out_shape=jax.ShapeDtypeStruct(q3.shape, F32),
        compiler_params=_params(("arbitrary",)),
        name="cache_attn",
    )(q3, kn3, vn3, by_head(cache_k), by_head(cache_v))


def _lower_bound(logits, layer):
    e = jnp.exp(logits - jnp.max(logits, axis=0, keepdims=True))
    sm = e / jnp.sum(e, axis=0, keepdims=True)
    lb = jnp.zeros_like(sm[0:1])
    for j in range(1, layer + 1):
        lb = lb + sm[j:j + 1]
    return lb


def _hgrn_body(lbl_ref, nw_ref, q_ref, f_ref, i_ref, g_ref, s0_ref, o_ref, sout_ref,
               st_ref, b2_ref, kk_ref, eb_ref, q_s_ref, v_s_ref, qt_ref, kh_ref, o_s_ref,
               *, layer, n_chunks):
    c = REC_CHUNK
    d_rec = q_ref.shape[-1]
    n_pairs = d_rec // LANES
    rows_in = q_ref.shape[1]
    rows = n_chunks * c
    grp = min(rows, LANES)

    @pl.when(pl.program_id(1) == 0)
    def _():
        st_ref[...] = s0_ref[0]

    def padded(ref):
        x = ref[0]
        if rows_in < rows:
            x = jnp.concatenate([x, jnp.zeros((rows - rows_in, d_rec), F32)], axis=0)
        return x

    lb = _lower_bound(lbl_ref[...], layer)
    log_lb = jnp.log(lb)
    z = padded(f_ref)
    t = jnp.exp(-jnp.abs(z))
    log_sig = jnp.minimum(z, 0.0) - jnp.log(1.0 + t)
    b_term = jnp.log1p(-lb) + log_sig
    log_f = jnp.maximum(log_lb, b_term) + jnp.log(1.0 + jnp.exp(-jnp.abs(log_lb - b_term)))
    inv = 1.0 / (1.0 + t)
    kk = (1.0 - lb) * jnp.where(z > 0.0, t * inv, inv)
    if rows_in < rows:
        live = lax.broadcasted_iota(jnp.int32, (rows, d_rec), 0) < rows_in
        log_f = jnp.where(live, log_f, 0.0)
        kk = jnp.where(live, kk, 0.0)
    r_i = lax.broadcasted_iota(jnp.int32, (grp, grp), 0)
    c_i = lax.broadcasted_iota(jnp.int32, (grp, grp), 1)
    same_chunk = (r_i >> (c.bit_length() - 1)) == (c_i >> (c.bit_length() - 1))
    prefix = jnp.where(same_chunk & (c_i <= r_i), 1.0, 0.0).astype(BF16)
    suffix = jnp.where(same_chunk & (c_i > r_i), 1.0, 0.0).astype(BF16)
    q_all = padded(q_ref)
    for g in range(rows // grp):
        gs = slice(g * grp, (g + 1) * grp)
        parts = _split3(log_f[gs])
        b = sum(_dot(prefix, part) for part in parts)
        r = sum(_dot(suffix, part) for part in parts)
        eb = jnp.exp(b)
        b2_ref[gs, :] = b * LOG2_E
        eb_ref[gs, :] = eb
        qt_ref[gs, :] = (q_all[gs] * eb).astype(BF16)
        kh_ref[gs, :] = (kk[gs] * jnp.exp(r)).astype(BF16)
    kk_ref[...] = kk
    q_s_ref[...] = q_all
    v_s_ref[...] = padded(i_ref)

    l_r = lax.broadcasted_iota(jnp.int32, (LANES, LANES), 0) >> (HEAD_DIM.bit_length() - 1)
    l_c = lax.broadcasted_iota(jnp.int32, (LANES, LANES), 1) >> (HEAD_DIM.bit_length() - 1)
    same_head = l_r == l_c
    head_sum = jnp.where(same_head, 1.0, 0.0).astype(BF16)
    row = lax.broadcasted_iota(jnp.int32, (c, LANES), 0)

    def chunk(ci, carry):
        r0 = pl.multiple_of(ci * c, c)
        rs = pl.ds(r0, c)
        for p in range(n_pairs):
            sl = slice(p * LANES, (p + 1) * LANES)
            b2, q, v, k = b2_ref[rs, sl], q_s_ref[rs, sl], v_s_ref[rs, sl], kk_ref[rs, sl]
            xs = []
            for s_i in range(c):
                cap = jnp.where(row >= s_i, 0.0, -jnp.inf)
                e = jnp.exp2(jnp.minimum(b2 - b2[s_i:s_i + 1], cap))
                xs.append((q * e) * k[s_i:s_i + 1])
            a_rep = _dot(jnp.concatenate(xs, axis=0).astype(BF16), head_sum)
            o = a_rep[0:c] * v[0:1]
            for s_i in range(1, c):
                o = o + a_rep[s_i * c:(s_i + 1) * c] * v[s_i:s_i + 1]
            st = st_ref[p]
            o = o + _dot_nt(qt_ref[rs, sl], st.astype(BF16))
            upd = _dot_tn(v.astype(BF16), kh_ref[rs, sl])
            st_ref[p] = st * eb_ref[rs, sl][c - 1:c] + jnp.where(same_head, upd, 0.0)
            o_s_ref[rs, sl] = o
        return carry

    lax.fori_loop(0, n_chunks, chunk, 0, unroll=UNROLL if n_chunks % UNROLL == 0 else 1)

    g_all = padded(g_ref)
    outs = []
    for p in range(n_pairs):
        sl = slice(p * LANES, (p + 1) * LANES)
        o = o_s_ref[:, sl]
        sq_hi, sq_mid, _ = _split3(o * o)
        ms = (_dot(sq_hi, head_sum) + _dot(sq_mid, head_sum)) * (1.0 / HEAD_DIM)
        outs.append(o * lax.rsqrt(ms + NORM_EPS) * nw_ref[:, sl] * _silu(g_all[:, sl]))
    o_ref[0] = jnp.concatenate(outs, axis=-1)[:rows_in]

    @pl.when(pl.program_id(1) == pl.num_programs(1) - 1)
    def _():
        sout_ref[0] = st_ref[...]


def _pair_state(s):
    b, h, kd, vd = s.shape
    st = jnp.swapaxes(s, 2, 3).reshape(b, h // 2, 2, vd, kd)
    z = jnp.zeros_like(st[:, :, 0])
    top = jnp.concatenate([st[:, :, 0], z], axis=-1)
    bot = jnp.concatenate([z, st[:, :, 1]], axis=-1)
    return jnp.concatenate([top, bot], axis=-2)


def _unpair_state(sp):
    b, hp = sp.shape[:2]
    s0 = sp[:, :, :HEAD_DIM, :HEAD_DIM]
    s1 = sp[:, :, HEAD_DIM:, HEAD_DIM:]
    st = jnp.stack([s0, s1], axis=2).reshape(b, 2 * hp, HEAD_DIM, HEAD_DIM)
    return jnp.swapaxes(st, 2, 3)


def _hgrn(qr, fr, ir, gr, s0, lb_logits, norm_w, layer, batch, seq):
    d_rec = qr.shape[-1]
    n_pairs = d_rec // LANES
    tb = min(seq, ROW_TILE)
    assert seq % tb == 0 and (tb % REC_CHUNK == 0 or tb == seq < REC_CHUNK)
    n_chunks = max(tb // REC_CHUNK, 1)
    rows = n_chunks * REC_CHUNK
    view = lambda a: a.reshape(batch, seq, d_rec)
    tok = pl.BlockSpec((1, tb, d_rec), lambda b, t: (b, t, 0))
    state = pl.BlockSpec((1, n_pairs, LANES, LANES), lambda b, t: (b, 0, 0, 0))
    o, s_new = pl.pallas_call(
        functools.partial(_hgrn_body, layer=layer, n_chunks=n_chunks),
        grid=(batch, seq // tb),
        in_specs=[pl.BlockSpec(lb_logits.shape, lambda b, t: (0, 0)),
                  pl.BlockSpec((1, d_rec), lambda b, t: (0, 0)),
                  tok, tok, tok, tok, state],
        out_specs=[tok, state],
        out_shape=[jax.ShapeDtypeStruct((batch, seq, d_rec), F32),
                   jax.ShapeDtypeStruct((batch, n_pairs, LANES, LANES), F32)],
        scratch_shapes=[pltpu.VMEM((n_pairs, LANES, LANES), F32)]
        + [pltpu.VMEM((rows, d_rec), F32)] * 5 + [pltpu.VMEM((rows, d_rec), BF16)] * 2
        + [pltpu.VMEM((rows, d_rec), F32)],
        compiler_params=_params(("arbitrary", "arbitrary")),
        name="hgrn",
    )(lb_logits, norm_w.reshape(1, d_rec), view(qr), view(fr), view(ir), view(gr), _pair_state(s0))
    return o.reshape(batch * seq, d_rec), _unpair_state(s_new)


def _hgrn_step_body(lblt_ref, nwt_ref, q_ref, f_ref, i_ref, g_ref, s0_ref, o_ref, sout_ref, *, layer):
    n_t, hd, _ = q_ref.shape
    logits = lblt_ref[...]
    e = jnp.exp(logits - jnp.max(logits, axis=1, keepdims=True))
    sm = e / jnp.sum(e, axis=1, keepdims=True)
    lb = jnp.zeros_like(sm[:, 0:1])
    for j in range(1, layer + 1):
        lb = lb + sm[:, j:j + 1]
    for t in range(n_t):
        z = f_ref[t]
        f_t = lb + (1.0 - lb) * jax.nn.sigmoid(z)
        k_t = (1.0 - lb) * jax.nn.sigmoid(-z)
        q_t, v_t = q_ref[t], i_ref[t]
        src = s0_ref if t == 0 else sout_ref
        o_t = jnp.zeros(v_t.shape, F32)
        for k in range(hd):
            s_k = f_t[k:k + 1] * src[0, 0, k] + k_t[k:k + 1] * v_t
            sout_ref[0, 0, k] = s_k
            o_t = o_t + s_k * q_t[k:k + 1]
        ms = jnp.mean(o_t * o_t, axis=0, keepdims=True)
        o_ref[t] = o_t * lax.rsqrt(ms + NORM_EPS) * nwt_ref[...] * _silu(g_ref[t])


def _hgrn_step(qr, fr, ir, gr, state_all, lb_logits, norm_w, layer, batch, seq):
    d_rec = qr.shape[-1]
    n_heads = d_rec // HEAD_DIM
    to_lanes = lambda a: jnp.transpose(a.reshape(batch, seq, d_rec), (1, 2, 0))
    tok = pl.BlockSpec((seq, HEAD_DIM, batch), lambda h: (0, h, 0))
    state_in = pl.BlockSpec((1, 1, HEAD_DIM, HEAD_DIM, batch), lambda h: (layer, h, 0, 0, 0))
    state_out = pl.BlockSpec((1, 1, HEAD_DIM, HEAD_DIM, batch), lambda h: (0, h, 0, 0, 0))
    o_t, s_new = pl.pallas_call(
        functools.partial(_hgrn_step_body, layer=layer),
        grid=(n_heads,),
        in_specs=[pl.BlockSpec((HEAD_DIM, lb_logits.shape[0]), lambda h: (h, 0)),
                  pl.BlockSpec((HEAD_DIM, 1), lambda h: (h, 0)),
                  tok, tok, tok, tok, state_in],
        out_specs=[tok, state_out],
        out_shape=[jax.ShapeDtypeStruct((seq, d_rec, batch), F32),
                   jax.ShapeDtypeStruct((1, n_heads, HEAD_DIM, HEAD_DIM, batch), F32)],
        compiler_params=_params(("arbitrary",)),
        name="hgrn_step",
    )(lb_logits.T, norm_w.reshape(d_rec, 1), to_lanes(qr), to_lanes(fr), to_lanes(ir), to_lanes(gr),
      jnp.transpose(state_all, (0, 2, 3, 4, 1)))
    o = jnp.transpose(o_t, (2, 0, 1)).reshape(batch * seq, d_rec)
    return o, jnp.transpose(s_new[0], (3, 0, 1, 2))


CONV_PAD = 8


def _conv_body(bc_ref, cc_ref, xc_ref, buf_ref, w_ref, b_ref, oc_ref, new_ref, up_ref):
    seq = cc_ref.shape[1]
    width = w_ref.shape[0]
    u = cc_ref[0] * xc_ref[0]
    up_ref[pl.ds(CONV_PAD, seq), :] = u
    up_ref[pl.ds(CONV_PAD - (width - 1), width - 1), :] = buf_ref[0]
    y = u * w_ref[width - 1:width, :] + b_ref[...]
    for i in range(width - 1):
        y = y + up_ref[pl.ds(CONV_PAD - (width - 1) + i, seq), :] * w_ref[i:i + 1, :]
    oc_ref[0] = bc_ref[0] * y
    new_ref[0] = up_ref[pl.ds(CONV_PAD + seq - (width - 1), width - 1), :]


def _short_conv(bc, cc, xc, buf, conv_w, conv_b, batch, seq):
    d_conv = bc.shape[-1]
    width = conv_w.shape[0]
    view = lambda a: a.reshape(batch, seq, d_conv)
    tok = pl.BlockSpec((1, seq, d_conv), lambda b: (b, 0, 0))
    tail = pl.BlockSpec((1, width - 1, d_conv), lambda b: (b, 0, 0))
    oc, new = pl.pallas_call(
        _conv_body,
        grid=(batch,),
        in_specs=[tok, tok, tok, tail,
                  pl.BlockSpec((width, d_conv), lambda b: (0, 0)),
                  pl.BlockSpec((1, d_conv), lambda b: (0, 0))],
        out_specs=[tok, tail],
        out_shape=[jax.ShapeDtypeStruct((batch, seq, d_conv), F32),
                   jax.ShapeDtypeStruct((batch, width - 1, d_conv), F32)],
        scratch_shapes=[pltpu.VMEM((CONV_PAD + seq, d_conv), F32)],
        compiler_params=_params(("arbitrary",)),
        name="short_conv",
    )(view(bc), view(cc), view(xc), buf, conv_w, conv_b.reshape(1, d_conv))
    return oc.reshape(batch * seq, d_conv), new


def _mix_ffn_body(oa_ref, or_ref, oc_ref, x_ref, g1_ref, sh_ref, sc_ref, g2_ref, w_ref, wi_ref, wo_ref,
                  fw_ref, o_ref, *, bb, tt, final, cuts):
    d = x_ref.shape[-1]
    d_ff = wo_ref.shape[0]
    n_pairs = oa_ref.shape[0]
    d_att, d_rec = n_pairs * LANES, or_ref.shape[-1]
    mix = (_dot(or_ref[...].astype(BF16), w_ref[d_att:d_att + d_rec, :])
           + _dot(oc_ref[...].astype(BF16), w_ref[d_att + d_rec:, :]))
    for p in range(n_pairs):
        mix = mix + _dot(oa_ref[p].astype(BF16), w_ref[p * LANES:(p + 1) * LANES, :])
    x = x_ref[...].reshape(bb, tt, d) + g1_ref[...] * mix.reshape(bb, tt, d)
    h = (_rms(x) * (1.0 + sc_ref[...]) + sh_ref[...]).reshape(bb * tt, d).astype(BF16)
    acc = None
    for c0, c1 in zip(cuts[:-1], cuts[1:]):
        a = (_silu(_dot(h, wi_ref[:, c0:c1])) * _dot(h, wi_ref[:, d_ff + c0:d_ff + c1])).astype(BF16)
        part = _dot(a, wo_ref[c0:c1, :])
        acc = part if acc is None else acc + part
    x = x + g2_ref[...] * acc.reshape(bb, tt, d)
    if final:
        x = _rms(x) * fw_ref[...]
    o_ref[...] = x.reshape(bb * tt, d)


def _mix_ffn(oa3, o_r, oc, x2, planes, w_out_bf, w_ffn_in_bf, w_ffn_out_bf, layer, final_w, batch, seq, final):
    n, d = x2.shape
    d_ff = w_ffn_out_bf.shape[1]
    bb, tt = _token_tile(batch, seq)
    tm = bb * tt
    n_t = seq // tt
    assert d_ff % MXU_WIDTH == 0
    n_cut = -(-d_ff // FFN_COLS)
    cuts = tuple(round(i * (d_ff // MXU_WIDTH) / n_cut) * MXU_WIDTH for i in range(n_cut + 1))
    tok = lambda a: pl.BlockSpec((tm, a.shape[-1]), lambda i: (i, 0))
    held = lambda a: pl.BlockSpec((None,) + a.shape[1:], lambda i: (layer, 0, 0), pipeline_mode=pl.Buffered(1))
    return pl.pallas_call(
        functools.partial(_mix_ffn_body, bb=bb, tt=tt, final=final, cuts=cuts),
        grid=(n // tm,),
        in_specs=[pl.BlockSpec((oa3.shape[0], tm, LANES), lambda i: (0, i, 0)), tok(o_r), tok(oc), tok(x2)]
        + [_mod_spec(layer, which, bb, d, n_t) for which in (GATE1, SHIFT2, SCALE2, GATE2)]
        + [held(w_out_bf), held(w_ffn_in_bf), held(w_ffn_out_bf), pl.BlockSpec((1, d), lambda i: (0, 0))],
        out_specs=pl.BlockSpec((tm, d), lambda i: (i, 0)),
        out_shape=jax.ShapeDtypeStruct((n, d), F32),
        compiler_params=_params(("arbitrary",)),
        name="mix_ffn",
    )(oa3, o_r, oc, x2, planes, planes, planes, planes, w_out_bf, w_ffn_in_bf, w_ffn_out_bf, final_w.reshape(1, d))


def _kv_layout_body(*refs, depth):
    ins, outs = refs[:2 * depth], refs[2 * depth:]
    for which, o_ref in enumerate(outs):
        for l in range(depth):
            src = ins[which * depth + l]
            for p in range(src.shape[0]):
                o_ref[l, 0, p * LANES:(p + 1) * LANES, :] = src[p].T


def _kv_layout(ks, vs, batch, seq):
    depth = len(ks)
    n_pairs = ks[0].shape[0]
    d_att = n_pairs * LANES
    tt = min(seq, ROW_TILE)
    n_t = seq // tt
    src = pl.BlockSpec((n_pairs, tt, LANES), lambda b, j: (0, b * n_t + j, 0))
    dst = pl.BlockSpec((depth, 1, d_att, tt), lambda b, j: (0, b, 0, j))
    shape = jax.ShapeDtypeStruct((depth, batch, d_att, seq), F32)
    k_t, v_t = pl.pallas_call(
        functools.partial(_kv_layout_body, depth=depth),
        grid=(batch, n_t),
        in_specs=[src] * (2 * depth),
        out_specs=[dst, dst],
        out_shape=[shape, shape],
        compiler_params=_params(("arbitrary", "arbitrary")),
        name="kv_layout",
    )(*ks, *vs)
    heads = lambda a: jnp.transpose(a.reshape(depth, batch, d_att // HEAD_DIM, HEAD_DIM, seq), (0, 1, 4, 2, 3))
    return heads(k_t), heads(v_t)


def _trunk(x, mods, pos, cache_k, cache_v, state_hgrn, state_conv, weights):
    (w_in, conv_w, conv_b, lb_logits, hgrn_norm_w, w_out, w_ffn_in, w_ffn_out, final_norm_w) = weights
    batch, seq, d = x.shape
    depth = w_in.shape[0]
    d_conv = conv_w.shape[-1]
    d_att = d_rec = (w_in.shape[-1] - 3 * d_conv) // 7
    n_heads = d_att // HEAD_DIM
    widths = (d_att,) * 3 + (d_rec,) * 4 + (d_conv,) * 3
    bb, tt = _token_tile(batch, seq)
    cos_t, sin_t = _rope_tables(pos, n_heads)
    if bb > 1:
        cos_t, sin_t = jnp.tile(cos_t, (bb, 1)), jnp.tile(sin_t, (bb, 1))
    x2 = x.reshape(batch * seq, d)
    planes = _mod_planes(mods, d)
    ks, vs, hs, cs = [], [], [], []
    for l in range(depth):
        qa, ka, va, qr, fr, ir, gr, bc, cc, xc = _in_proj(
            x2, planes, w_in, l, cos_t, sin_t, batch, seq, widths, 3)
        if cache_k is None:
            oa = _prompt_attention(qa, ka, va, batch, seq)
            s0 = jnp.zeros((batch, n_heads, HEAD_DIM, HEAD_DIM), F32)
            conv0 = jnp.zeros((batch, conv_w.shape[1] - 1, d_conv), F32)
        else:
            oa = _cache_attention(qa, ka, va, cache_k, cache_v, l, batch, seq)
            conv0 = state_conv[l]
        if cache_k is None:
            o_r, s_new = _hgrn(qr, fr, ir, gr, s0, lb_logits, hgrn_norm_w[l], l, batch, seq)
        else:
            o_r, s_new = _hgrn_step(qr, fr, ir, gr, state_hgrn, lb_logits, hgrn_norm_w[l], l, batch, seq)
        oc, conv_new = _short_conv(bc, cc, xc, conv0, conv_w[l], conv_b[l], batch, seq)
        x2 = _mix_ffn(oa, o_r, oc, x2, planes, w_out, w_ffn_in, w_ffn_out, l, final_norm_w, batch, seq, l == depth - 1)
        ks.append(ka)
        vs.append(va)
        hs.append(s_new)
        cs.append(conv_new)
    keep = min(WINDOW_STEPS * max(DILATIONS), seq)
    if seq % LANES == 0:
        k_out, v_out = _kv_layout(ks, vs, batch, seq)
    else:
        unpair = lambda parts: jnp.transpose(
            jnp.stack(parts).reshape(depth, n_heads // 2, batch, seq, 2, HEAD_DIM),
            (0, 2, 3, 1, 4, 5)).reshape(depth, batch, seq, n_heads, HEAD_DIM)
        k_out, v_out = unpair(ks), unpair(vs)
    return (x2.reshape(batch, seq, d), k_out[:, :, seq - keep:], v_out[:, :, seq - keep:],
            jnp.stack(hs), jnp.stack(cs))


def kernel(x_prompt, x_sample, cache_k, cache_v, state_hgrn, state_conv, c_prompt, c_sample,
           w_ada, b_ada, w_in, conv_w, conv_b, hgrn_lb_logits, hgrn_norm_w, w_out,
           w_ffn_in, w_ffn_out, final_norm_w):
    past_len = PAST_LEN
    assert cache_k.shape[2] == min(WINDOW_STEPS * max(DILATIONS), PAST_LEN) == PAST_LEN
    mods_p, mods_s = _ada((c_prompt, c_sample), w_ada, b_ada)
    weights = (w_in.astype(BF16), conv_w, conv_b, hgrn_lb_logits.astype(F32), hgrn_norm_w,
               w_out.astype(BF16), w_ffn_in.astype(BF16), w_ffn_out.astype(BF16), final_norm_w)
    pos_p = jnp.arange(x_prompt.shape[1], dtype=jnp.int32)
    pos_s = past_len + jnp.arange(x_sample.shape[1], dtype=jnp.int32)
    out_p = _trunk(x_prompt, mods_p, pos_p, None, None, None, None, weights)
    out_s = _trunk(x_sample, mods_s, pos_s, cache_k, cache_v, state_hgrn, state_conv, weights)
    return (out_p[0], out_s[0]) + out_p[1:] + out_s[1:]
```

```python
import functools

import jax
import jax.numpy as jnp
from jax import lax
from jax.experimental import pallas as pl
from jax.experimental.pallas import tpu as pltpu

F32 = jnp.float32
BF16 = jnp.bfloat16

HEAD_DIM = 64
LANES = 128
DILATIONS = (1, 4, 16)
WINDOW_STEPS = 128
PAST_LEN = 2048
ROPE_THETA = 10000.0
NORM_EPS = 1e-6
REC_CHUNK = 16
UNROLL = 8
LOG2_E = 1.4426950408889634
ROW_TILE = 512
MXU_WIDTH = 256
FFN_COLS = 1536
CACHE_BATCH = 2
VMEM_LIMIT = 48 * 1024 * 1024


def _params(sem):
    return pltpu.CompilerParams(dimension_semantics=sem, vmem_limit_bytes=VMEM_LIMIT)


def _dot(a, b):
    return jnp.dot(a, b, preferred_element_type=F32)


def _dot_nt(a, b):
    return lax.dot_general(a, b, (((1,), (1,)), ((), ())), preferred_element_type=F32)


def _dot_tn(a, b):
    return lax.dot_general(a, b, (((0,), (0,)), ((), ())), preferred_element_type=F32)


def _split3(x):
    hi = x.astype(BF16)
    r = x - hi.astype(F32)
    mid = r.astype(BF16)
    lo = (r - mid.astype(F32)).astype(BF16)
    return hi, mid, lo


def _silu(x):
    return x * jax.nn.sigmoid(x)


def _rms(x):
    return x * lax.rsqrt(jnp.mean(x * x, axis=-1, keepdims=True) + NORM_EPS)


def _token_tile(batch, seq):
    if seq >= ROW_TILE:
        assert seq % ROW_TILE == 0
        return 1, ROW_TILE
    assert ROW_TILE % seq == 0 and seq % 8 == 0
    bb = min(batch, ROW_TILE // seq)
    assert batch % bb == 0
    return bb, seq


def _ada_body(*refs):
    n = (len(refs) - 2) // 2
    c_refs, (w_ref, b_ref), o_refs = refs[:n], refs[n:n + 2], refs[n + 2:]
    w_hi, w_mid, _ = _split3(w_ref[0])
    for c_ref, o_ref in zip(c_refs, o_refs):
        s_hi, s_mid, _ = _split3(_silu(c_ref[...]))
        o_ref[0] = _dot(s_hi, w_hi) + _dot(s_hi, w_mid) + _dot(s_mid, w_hi) + b_ref[0]


def _ada(cs, w_ada, b_ada):
    depth, d, n6 = w_ada.shape
    tn = 512
    return pl.pallas_call(
        _ada_body,
        grid=(depth, n6 // tn),
        in_specs=[pl.BlockSpec(c.shape, lambda l, j: (0, 0)) for c in cs]
        + [pl.BlockSpec((1, d, tn), lambda l, j: (l, 0, j)),
           pl.BlockSpec((1, 1, tn), lambda l, j: (l, 0, j))],
        out_specs=[pl.BlockSpec((1, c.shape[0], tn), lambda l, j: (l, 0, j)) for c in cs],
        out_shape=[jax.ShapeDtypeStruct((depth, c.shape[0], n6), F32) for c in cs],
        compiler_params=_params(("arbitrary", "arbitrary")),
        name="ada",
    )(*cs, w_ada, b_ada.reshape(depth, 1, n6))


def _mod_planes(mods, d):
    depth, batch, _ = mods.shape
    return jnp.transpose(mods.reshape(depth, batch, 6, 1, d), (0, 2, 1, 3, 4))


def _mod_spec(layer, which, bb, d, n_t):
    return pl.BlockSpec((None, None, bb, 1, d), lambda i: (layer, which, i // n_t, 0, 0))


SHIFT1, SCALE1, GATE1, SHIFT2, SCALE2, GATE2 = range(6)


def _rope_tables(pos, n_heads):
    half = HEAD_DIM // 2
    freqs = ROPE_THETA ** (-jnp.arange(half, dtype=F32) / half)
    ang = pos.astype(F32)[:, None] * freqs[None, :]
    cos, sin = jnp.cos(ang), jnp.sin(ang)
    cos_t = jnp.tile(jnp.concatenate([cos, cos], axis=-1), (1, n_heads))
    sin_t = jnp.tile(jnp.concatenate([-sin, sin], axis=-1), (1, n_heads))
    return cos_t, sin_t


def _rope(x, cos, sin_signed):
    outs = []
    for c in range(x.shape[-1] // LANES):
        sl = slice(c * LANES, (c + 1) * LANES)
        xc = x[:, sl]
        lane = lax.broadcasted_iota(jnp.int32, xc.shape, 1)
        ahead = pltpu.roll(xc, LANES - HEAD_DIM // 2, axis=1)
        behind = pltpu.roll(xc, HEAD_DIM // 2, axis=1)
        rot = jnp.where((lane & (HEAD_DIM // 2)) == 0, ahead, behind)
        outs.append(xc * cos[:, sl] + rot * sin_signed[:, sl])
    return jnp.concatenate(outs, axis=-1)


def _in_proj_body(x_ref, sh_ref, sc_ref, w_ref, cos_ref, sin_ref, *out_refs, bb, tt, widths, n_paired, runs):
    d = x_ref.shape[-1]
    x = x_ref[...].reshape(bb, tt, d)
    h = _rms(x) * (1.0 + sc_ref[...]) + sh_ref[...]
    hb = h.reshape(bb * tt, d).astype(BF16)
    for first_out, n_out in runs:
        off, wd = sum(widths[:first_out]), widths[first_out]
        group = _dot(hb, w_ref[:, off:off + wd * n_out])
        for n in range(n_out):
            i = first_out + n
            p = group[:, n * wd:(n + 1) * wd]
            if i < 2:
                p = _rope(p, cos_ref[...], sin_ref[...])
            if i < n_paired:
                for c in range(wd // LANES):
                    out_refs[i][c] = p[:, c * LANES:(c + 1) * LANES]
            else:
                out_refs[i][...] = p


def _equal_runs(widths):
    runs = []
    for i, wd in enumerate(widths):
        if runs and widths[runs[-1][0]] == wd:
            runs[-1][1] += 1
        else:
            runs.append([i, 1])
    return tuple((a, n) for a, n in runs)


def _in_proj(x2, planes, w_in_bf, layer, cos_t, sin_t, batch, seq, widths, n_paired):
    n, d = x2.shape
    bb, tt = _token_tile(batch, seq)
    tm = bb * tt
    n_t = seq // tt
    table_blocks = cos_t.shape[0] // tm
    body = functools.partial(_in_proj_body, bb=bb, tt=tt, widths=widths, n_paired=n_paired, runs=_equal_runs(widths))
    tab_spec = pl.BlockSpec((tm, cos_t.shape[1]), lambda i: (i % table_blocks, 0))
    out_specs, out_shape = [], []
    for idx, wd in enumerate(widths):
        if idx < n_paired:
            out_specs.append(pl.BlockSpec((wd // LANES, tm, LANES), lambda i: (0, i, 0)))
            out_shape.append(jax.ShapeDtypeStruct((wd // LANES, n, LANES), F32))
        else:
            out_specs.append(pl.BlockSpec((tm, wd), lambda i: (i, 0)))
            out_shape.append(jax.ShapeDtypeStruct((n, wd), F32))
    return pl.pallas_call(
        body,
        grid=(n // tm,),
        in_specs=[pl.BlockSpec((tm, d), lambda i: (i, 0)),
                  _mod_spec(layer, SHIFT1, bb, d, n_t), _mod_spec(layer, SCALE1, bb, d, n_t),
                  pl.BlockSpec((None,) + w_in_bf.shape[1:], lambda i: (layer, 0, 0)), tab_spec, tab_spec],
        out_specs=out_specs,
        out_shape=out_shape,
        compiler_params=_params(("arbitrary",)),
        name="in_proj",
    )(x2, planes, planes, w_in_bf, cos_t, sin_t)


def _prompt_attn_body(q_ref, k_ref, v_ref, o_ref, m_ref, l_ref, acc_ref):
    n_pairs, seq, _ = q_ref.shape
    blk = WINDOW_STEPS
    lane = lax.broadcasted_iota(jnp.int32, (blk, LANES), 1)
    first = lane < HEAD_DIM
    scale = HEAD_DIM ** -0.5 * LOG2_E

    def rows(start, dil):
        return pl.ds(start, blk) if dil == 1 else pl.ds(start, blk, stride=dil)

    order = sorted(DILATIONS, reverse=True)
    for b_idx, dil in enumerate(order):
        n_blk = seq // dil // blk
        has_prev = n_blk > 1
        n_keys = 2 * blk if has_prev else blk
        q_row = lax.broadcasted_iota(jnp.int32, (2 * blk, n_keys), 0) & (blk - 1)
        k_col = lax.broadcasted_iota(jnp.int32, (2 * blk, n_keys), 1)
        is_first, is_last = b_idx == 0, b_idx == len(DILATIONS) - 1

        def unit(u, carry, dil=dil, has_prev=has_prev, q_row=q_row, k_col=k_col,
                 is_first=is_first, is_last=is_last):
            res = u & (dil - 1)
            i = u >> (dil.bit_length() - 1)
            start = res + dil * blk * i
            if dil == 1:
                start = pl.multiple_of(start, blk)
            if has_prev:
                prev_start = jnp.maximum(start - dil * blk, res)
                lo = jnp.maximum(q_row, jnp.where(i > 0, 0, blk))
                valid = (k_col >= lo) & (k_col <= q_row + blk)
            else:
                valid = k_col <= q_row
            loaded = []
            for p in range(n_pairs):
                q = q_ref[p, rows(start, dil), :] * scale
                kc = k_ref[p, rows(start, dil), :]
                vc = v_ref[p, rows(start, dil), :]
                if has_prev:
                    kc = jnp.concatenate([k_ref[p, rows(prev_start, dil), :], kc], axis=0)
                    vc = jnp.concatenate([v_ref[p, rows(prev_start, dil), :], vc], axis=0)
                old = None
                if not is_first:
                    old = (m_ref[p, rows(start, dil), :], l_ref[p, rows(start, dil), :],
                           acc_ref[p, rows(start, dil), :])
                loaded.append((q, kc, vc, old))
            scores = []
            for q, kc, vc, old in loaded:
                q2 = jnp.concatenate([jnp.where(first, q, 0.0), jnp.where(first, 0.0, q)], axis=0)
                scores.append(_dot_nt(q2.astype(BF16), kc.astype(BF16)))
            results = []
            for (q, kc, vc, old), s in zip(loaded, scores):
                s = jnp.where(valid, s, -jnp.inf)
                m2 = jnp.max(s, axis=-1, keepdims=True)
                e = jnp.exp2(s - m2).astype(BF16)
                v1 = jnp.concatenate([vc.astype(BF16), jnp.ones(vc.shape, BF16)], axis=1)
                pv = _dot(e, v1)
                m_b = jnp.where(first, m2[:blk], m2[blk:])
                l_b = jnp.where(first, pv[:blk, LANES:], pv[blk:, LANES:])
                acc_b = jnp.where(first, pv[:blk, :LANES], pv[blk:, :LANES])
                if not is_first:
                    m_o, l_o, acc_o = old
                    m_n = jnp.maximum(m_o, m_b)
                    w_o, w_b = jnp.exp2(m_o - m_n), jnp.exp2(m_b - m_n)
                    l_b = w_o * l_o + w_b * l_b
                    acc_b = w_o * acc_o + w_b * acc_b
                    m_b = m_n
                results.append((m_b, l_b, acc_b))
            for p, (m_b, l_b, acc_b) in enumerate(results):
                if is_last:
                    o_ref[p, rows(start, dil), :] = acc_b / l_b
                else:
                    m_ref[p, rows(start, dil), :] = m_b
                    l_ref[p, rows(start, dil), :] = l_b
                    acc_ref[p, rows(start, dil), :] = acc_b
            return carry

        lax.fori_loop(0, dil * n_blk, unit, 0, unroll=4)


def _prompt_attention(q3, k3, v3, batch, seq):
    n_pairs = q3.shape[0]
    assert seq % (WINDOW_STEPS * max(DILATIONS)) == 0
    spec = pl.BlockSpec((n_pairs, seq, LANES), lambda b: (0, b, 0))
    return pl.pallas_call(
        _prompt_attn_body,
        grid=(batch,),
        in_specs=[spec, spec, spec],
        out_specs=spec,
        out_shape=jax.ShapeDtypeStruct(q3.shape, F32),
        scratch_shapes=[pltpu.VMEM((n_pairs, seq, LANES), F32)] * 3,
        compiler_params=_params(("arbitrary",)),
        name="prompt_attn",
    )(q3, k3, v3)


def _branch_count(delta):
    cnt = jnp.zeros(delta.shape, F32)
    for dil in DILATIONS:
        hit = (delta >= 0) & (delta <= WINDOW_STEPS * dil) & ((delta & (dil - 1)) == 0)
        cnt = cnt + jnp.where(hit, 1.0, 0.0)
    return cnt


def _cache_attn_body(q_ref, kn_ref, vn_ref, kt_ref, vt_ref, o_ref, *, past, t_new):
    n_pairs = q_ref.shape[0]
    t_c = lax.broadcasted_iota(jnp.int32, (t_new, past), 0)
    j_c = lax.broadcasted_iota(jnp.int32, (t_new, past), 1)
    cnt_c = _branch_count(past + t_c - j_c)
    t_n = lax.broadcasted_iota(jnp.int32, (t_new, t_new), 0)
    j_n = lax.broadcasted_iota(jnp.int32, (t_new, t_new), 1)
    cnt_n = _branch_count(t_n - j_n)
    scale = HEAD_DIM ** -0.5
    for bi in range(kt_ref.shape[1]):
        rs = slice(bi * t_new, (bi + 1) * t_new)
        for p in range(n_pairs):
            q_pair, kn_pair, vn_pair = q_ref[p, rs, :] * scale, kn_ref[p, rs, :], vn_ref[p, rs, :]
            outs = []
            for hh in range(LANES // HEAD_DIM):
                h = p * (LANES // HEAD_DIM) + hh
                sl = slice(hh * HEAD_DIM, (hh + 1) * HEAD_DIM)
                q = q_pair[:, sl]
                s_c = jnp.where(cnt_c > 0, _dot(q.astype(BF16), kt_ref[0, bi, h].astype(BF16)), -jnp.inf)
                s_n = jnp.where(cnt_n > 0, _dot_nt(q, kn_pair[:, sl]), -jnp.inf)
                m = jnp.maximum(jnp.max(s_c, axis=-1, keepdims=True), jnp.max(s_n, axis=-1, keepdims=True))
                p_c = cnt_c * jnp.exp(s_c - m)
                p_n = cnt_n * jnp.exp(s_n - m)
                den = jnp.sum(p_c, axis=-1, keepdims=True) + jnp.sum(p_n, axis=-1, keepdims=True)
                acc = _dot_nt(p_c.astype(BF16), vt_ref[0, bi, h].astype(BF16)) + _dot(p_n, vn_pair[:, sl])
                outs.append(acc / den)
            o_ref[p, rs, :] = jnp.concatenate(outs, axis=-1)


def _cache_attention(q3, kn3, vn3, cache_k, cache_v, layer, batch, seq):
    depth, _, past, n_heads, hd = cache_k.shape
    n_pairs = q3.shape[0]
    bb = CACHE_BATCH if batch % CACHE_BATCH == 0 else 1
    by_head = lambda a: jnp.transpose(a, (0, 1, 3, 4, 2))
    new = pl.BlockSpec((n_pairs, bb * seq, LANES), lambda b: (0, b, 0))
    old = pl.BlockSpec((1, bb, n_heads, hd, past), lambda b: (layer, b, 0, 0, 0))
    return pl.pallas_call(
        functools.partial(_cache_attn_body, past=past, t_new=seq),
        grid=(batch // bb,),
        in_specs=[new, new, new, old, old],
        out_specs=new,
        out_shape=jax.ShapeDtypeStruct(q3.shape, F32),
        compiler_params=_params(("arbitrary",)),
        name="cache_attn",
    )(q3, kn3, vn3, by_head(cache_k), by_head(cache_v))


def _lower_bound(logits, layer):
    e = jnp.exp(logits - jnp.max(logits, axis=0, keepdims=True))
    sm = e / jnp.sum(e, axis=0, keepdims=True)
    lb = jnp.zeros_like(sm[0:1])
    for j in range(1, layer + 1):
        lb = lb + sm[j:j + 1]
    return lb


def _hgrn_body(lbl_ref, nw_ref, q_ref, f_ref, i_ref, g_ref, s0_ref, o_ref, sout_ref,
               st_ref, b2_ref, kk_ref, eb_ref, qt_ref, kh_ref, o_s_ref, *, layer):
    c = REC_CHUNK
    rows, d_rec = q_ref.shape[1], q_ref.shape[2]
    n_pairs = d_rec // LANES
    n_chunks = rows // c
    grp = min(rows, LANES)

    @pl.when(pl.program_id(1) == 0)
    def _():
        st_ref[...] = s0_ref[0]

    lb = _lower_bound(lbl_ref[...], layer)
    log_lb = jnp.log(lb)
    z = f_ref[0]
    t = jnp.exp(-jnp.abs(z))
    log_sig = jnp.minimum(z, 0.0) - jnp.log(1.0 + t)
    b_term = jnp.log1p(-lb) + log_sig
    log_f = jnp.maximum(log_lb, b_term) + jnp.log(1.0 + jnp.exp(-jnp.abs(log_lb - b_term)))
    inv = 1.0 / (1.0 + t)
    kk = (1.0 - lb) * jnp.where(z > 0.0, t * inv, inv)
    r_i = lax.broadcasted_iota(jnp.int32, (grp, grp), 0)
    c_i = lax.broadcasted_iota(jnp.int32, (grp, grp), 1)
    same_chunk = (r_i >> (c.bit_length() - 1)) == (c_i >> (c.bit_length() - 1))
    prefix = jnp.where(same_chunk & (c_i <= r_i), 1.0, 0.0).astype(BF16)
    suffix = jnp.where(same_chunk & (c_i > r_i), 1.0, 0.0).astype(BF16)
    for g in range(rows // grp):
        gs = slice(g * grp, (g + 1) * grp)
        parts = _split3(log_f[gs])
        b = sum(_dot(prefix, part) for part in parts)
        r = sum(_dot(suffix, part) for part in parts)
        eb = jnp.exp(b)
        b2_ref[gs, :] = b * LOG2_E
        eb_ref[gs, :] = eb
        qt_ref[gs, :] = (q_ref[0, gs, :] * eb).astype(BF16)
        kh_ref[gs, :] = (kk[gs] * jnp.exp(r)).astype(BF16)
    kk_ref[...] = kk

    l_r = lax.broadcasted_iota(jnp.int32, (LANES, LANES), 0) >> (HEAD_DIM.bit_length() - 1)
    l_c = lax.broadcasted_iota(jnp.int32, (LANES, LANES), 1) >> (HEAD_DIM.bit_length() - 1)
    same_head = l_r == l_c
    head_sum = jnp.where(same_head, 1.0, 0.0).astype(BF16)
    row = lax.broadcasted_iota(jnp.int32, (c, LANES), 0)

    def chunk(ci, carry):
        r0 = pl.multiple_of(ci * c, c)
        rs = pl.ds(r0, c)
        for p in range(n_pairs):
            sl = slice(p * LANES, (p + 1) * LANES)
            b2, q, v, k = b2_ref[rs, sl], q_ref[0, rs, sl], i_ref[0, rs, sl], kk_ref[rs, sl]
            xs = []
            for s_i in range(c):
                cap = jnp.where(row >= s_i, 0.0, -jnp.inf)
                e = jnp.exp2(jnp.minimum(b2 - b2[s_i:s_i + 1], cap))
                xs.append((q * e) * k[s_i:s_i + 1])
            a_rep = _dot(jnp.concatenate(xs, axis=0).astype(BF16), head_sum)
            o = a_rep[0:c] * v[0:1]
            for s_i in range(1, c):
                o = o + a_rep[s_i * c:(s_i + 1) * c] * v[s_i:s_i + 1]
            st = st_ref[p]
            o = o + _dot_nt(qt_ref[rs, sl], st.astype(BF16))
            upd = _dot_tn(v.astype(BF16), kh_ref[rs, sl])
            st_ref[p] = st * eb_ref[rs, sl][c - 1:c] + jnp.where(same_head, upd, 0.0)
            o_s_ref[rs, sl] = o
        return carry

    lax.fori_loop(0, n_chunks, chunk, 0, unroll=UNROLL if n_chunks % UNROLL == 0 else 1)

    g_all = g_ref[0]
    outs = []
    for p in range(n_pairs):
        sl = slice(p * LANES, (p + 1) * LANES)
        o = o_s_ref[:, sl]
        sq_hi, sq_mid, _ = _split3(o * o)
        ms = (_dot(sq_hi, head_sum) + _dot(sq_mid, head_sum)) * (1.0 / HEAD_DIM)
        outs.append(o * lax.rsqrt(ms + NORM_EPS) * nw_ref[:, sl] * _silu(g_all[:, sl]))
    o_ref[0] = jnp.concatenate(outs, axis=-1)

    @pl.when(pl.program_id(1) == pl.num_programs(1) - 1)
    def _():
        sout_ref[0] = st_ref[...]


def _pair_state(s):
    b, h, kd, vd = s.shape
    st = jnp.swapaxes(s, 2, 3).reshape(b, h // 2, 2, vd, kd)
    z = jnp.zeros_like(st[:, :, 0])
    top = jnp.concatenate([st[:, :, 0], z], axis=-1)
    bot = jnp.concatenate([z, st[:, :, 1]], axis=-1)
    return jnp.concatenate([top, bot], axis=-2)


def _unpair_state(sp):
    b, hp = sp.shape[:2]
    s0 = sp[:, :, :HEAD_DIM, :HEAD_DIM]
    s1 = sp[:, :, HEAD_DIM:, HEAD_DIM:]
    st = jnp.stack([s0, s1], axis=2).reshape(b, 2 * hp, HEAD_DIM, HEAD_DIM)
    return jnp.swapaxes(st, 2, 3)


def _hgrn(qr, fr, ir, gr, s0, lb_logits, norm_w, layer, batch, seq):
    d_rec = qr.shape[-1]
    n_pairs = d_rec // LANES
    tb = min(seq, ROW_TILE)
    assert seq % tb == 0 and tb % LANES == 0 and LANES % REC_CHUNK == 0
    view = lambda a: a.reshape(batch, seq, d_rec)
    tok = pl.BlockSpec((1, tb, d_rec), lambda b, t: (b, t, 0))
    state = pl.BlockSpec((1, n_pairs, LANES, LANES), lambda b, t: (b, 0, 0, 0))
    o, s_new = pl.pallas_call(
        functools.partial(_hgrn_body, layer=layer),
        grid=(batch, seq // tb),
        in_specs=[pl.BlockSpec(lb_logits.shape, lambda b, t: (0, 0)),
                  pl.BlockSpec((1, d_rec), lambda b, t: (0, 0)),
                  tok, tok, tok, tok, state],
        out_specs=[tok, state],
        out_shape=[jax.ShapeDtypeStruct((batch, seq, d_rec), F32),
                   jax.ShapeDtypeStruct((batch, n_pairs, LANES, LANES), F32)],
        scratch_shapes=[pltpu.VMEM((n_pairs, LANES, LANES), F32)]
        + [pltpu.VMEM((tb, d_rec), F32)] * 3 + [pltpu.VMEM((tb, d_rec), BF16)] * 2
        + [pltpu.VMEM((tb, d_rec), F32)],
        compiler_params=_params(("arbitrary", "arbitrary")),
        name="hgrn",
    )(lb_logits, norm_w.reshape(1, d_rec), view(qr), view(fr), view(ir), view(gr), _pair_state(s0))
    return o.reshape(batch * seq, d_rec), _unpair_state(s_new)


def _hgrn_step_body(lblt_ref, nwt_ref, q_ref, f_ref, i_ref, g_ref, s0_ref, o_ref, sout_ref, *, layer):
    n_t, hd, _ = q_ref.shape
    logits = lblt_ref[...]
    e = jnp.exp(logits - jnp.max(logits, axis=1, keepdims=True))
    sm = e / jnp.sum(e, axis=1, keepdims=True)
    lb = jnp.zeros_like(sm[:, 0:1])
    for j in range(1, layer + 1):
        lb = lb + sm[:, j:j + 1]
    for t in range(n_t):
        z = f_ref[t]
        f_t = lb + (1.0 - lb) * jax.nn.sigmoid(z)
        k_t = (1.0 - lb) * jax.nn.sigmoid(-z)
        q_t, v_t = q_ref[t], i_ref[t]
        src = s0_ref if t == 0 else sout_ref
        o_t = jnp.zeros(v_t.shape, F32)
        for k in range(hd):
            s_k = f_t[k:k + 1] * src[0, 0, k] + k_t[k:k + 1] * v_t
            sout_ref[0, 0, k] = s_k
            o_t = o_t + s_k * q_t[k:k + 1]
        ms = jnp.mean(o_t * o_t, axis=0, keepdims=True)
        o_ref[t] = o_t * lax.rsqrt(ms + NORM_EPS) * nwt_ref[...] * _silu(g_ref[t])


def _hgrn_step(qr, fr, ir, gr, state_all, lb_logits, norm_w, layer, batch, seq):
    d_rec = qr.shape[-1]
    n_heads = d_rec // HEAD_DIM
    to_lanes = lambda a: jnp.transpose(a.reshape(batch, seq, d_rec), (1, 2, 0))
    tok = pl.BlockSpec((seq, HEAD_DIM, batch), lambda h: (0, h, 0))
    state_in = pl.BlockSpec((1, 1, HEAD_DIM, HEAD_DIM, batch), lambda h: (layer, h, 0, 0, 0))
    state_out = pl.BlockSpec((1, 1, HEAD_DIM, HEAD_DIM, batch), lambda h: (0, h, 0, 0, 0))
    o_t, s_new = pl.pallas_call(
        functools.partial(_hgrn_step_body, layer=layer),
        grid=(n_heads,),
        in_specs=[pl.BlockSpec((HEAD_DIM, lb_logits.shape[0]), lambda h: (h, 0)),
                  pl.BlockSpec((HEAD_DIM, 1), lambda h: (h, 0)),
                  tok, tok, tok, tok, state_in],
        out_specs=[tok, state_out],
        out_shape=[jax.ShapeDtypeStruct((seq, d_rec, batch), F32),
                   jax.ShapeDtypeStruct((1, n_heads, HEAD_DIM, HEAD_DIM, batch), F32)],
        compiler_params=_params(("arbitrary",)),
        name="hgrn_step",
    )(lb_logits.T, norm_w.reshape(d_rec, 1), to_lanes(qr), to_lanes(fr), to_lanes(ir), to_lanes(gr),
      jnp.transpose(state_all, (0, 2, 3, 4, 1)))
    o = jnp.transpose(o_t, (2, 0, 1)).reshape(batch * seq, d_rec)
    return o, jnp.transpose(s_new[0], (3, 0, 1, 2))


CONV_PAD = 8


def _conv_body(bc_ref, cc_ref, xc_ref, buf_ref, w_ref, b_ref, oc_ref, new_ref, up_ref):
    seq = cc_ref.shape[1]
    width = w_ref.shape[0]
    u = cc_ref[0] * xc_ref[0]
    up_ref[pl.ds(CONV_PAD, seq), :] = u
    up_ref[pl.ds(CONV_PAD - (width - 1), width - 1), :] = buf_ref[0]
    y = u * w_ref[width - 1:width, :] + b_ref[...]
    for i in range(width - 1):
        y = y + up_ref[pl.ds(CONV_PAD - (width - 1) + i, seq), :] * w_ref[i:i + 1, :]
    oc_ref[0] = bc_ref[0] * y
    new_ref[0] = up_ref[pl.ds(CONV_PAD + seq - (width - 1), width - 1), :]


def _short_conv(bc, cc, xc, buf, conv_w, conv_b, batch, seq):
    d_conv = bc.shape[-1]
    width = conv_w.shape[0]
    view = lambda a: a.reshape(batch, seq, d_conv)
    tok = pl.BlockSpec((1, seq, d_conv), lambda b: (b, 0, 0))
    tail = pl.BlockSpec((1, width - 1, d_conv), lambda b: (b, 0, 0))
    oc, new = pl.pallas_call(
        _conv_body,
        grid=(batch,),
        in_specs=[tok, tok, tok, tail,
                  pl.BlockSpec((width, d_conv), lambda b: (0, 0)),
                  pl.BlockSpec((1, d_conv), lambda b: (0, 0))],
        out_specs=[tok, tail],
        out_shape=[jax.ShapeDtypeStruct((batch, seq, d_conv), F32),
                   jax.ShapeDtypeStruct((batch, width - 1, d_conv), F32)],
        scratch_shapes=[pltpu.VMEM((CONV_PAD + seq, d_conv), F32)],
        compiler_params=_params(("arbitrary",)),
        name="short_conv",
    )(view(bc), view(cc), view(xc), buf, conv_w, conv_b.reshape(1, d_conv))
    return oc.reshape(batch * seq, d_conv), new


def _mix_ffn_body(oa_ref, or_ref, oc_ref, x_ref, g1_ref, sh_ref, sc_ref, g2_ref, w_ref, wi_ref, wo_ref,
                  fw_ref, o_ref, *, bb, tt, final, cuts):
    d = x_ref.shape[-1]
    d_ff = wo_ref.shape[0]
    n_pairs = oa_ref.shape[0]
    d_att, d_rec = n_pairs * LANES, or_ref.shape[-1]
    mix = (_dot(or_ref[...].astype(BF16), w_ref[d_att:d_att + d_rec, :])
           + _dot(oc_ref[...].astype(BF16), w_ref[d_att + d_rec:, :]))
    for p in range(n_pairs):
        mix = mix + _dot(oa_ref[p].astype(BF16), w_ref[p * LANES:(p + 1) * LANES, :])
    x = x_ref[...].reshape(bb, tt, d) + g1_ref[...] * mix.reshape(bb, tt, d)
    h = (_rms(x) * (1.0 + sc_ref[...]) + sh_ref[...]).reshape(bb * tt, d).astype(BF16)
    acc = None
    for c0, c1 in zip(cuts[:-1], cuts[1:]):
        a = (_silu(_dot(h, wi_ref[:, c0:c1])) * _dot(h, wi_ref[:, d_ff + c0:d_ff + c1])).astype(BF16)
        part = _dot(a, wo_ref[c0:c1, :])
        acc = part if acc is None else acc + part
    x = x + g2_ref[...] * acc.reshape(bb, tt, d)
    if final:
        x = _rms(x) * fw_ref[...]
    o_ref[...] = x.reshape(bb * tt, d)


def _mix_ffn(oa3, o_r, oc, x2, planes, w_out_bf, w_ffn_in_bf, w_ffn_out_bf, layer, final_w, batch, seq, final):
    n, d = x2.shape
    d_ff = w_ffn_out_bf.shape[1]
    bb, tt = _token_tile(batch, seq)
    tm = bb * tt
    n_t = seq // tt
    assert d_ff % MXU_WIDTH == 0
    n_cut = -(-d_ff // FFN_COLS)
    cuts = tuple(round(i * (d_ff // MXU_WIDTH) / n_cut) * MXU_WIDTH for i in range(n_cut + 1))
    tok = lambda a: pl.BlockSpec((tm, a.shape[-1]), lambda i: (i, 0))
    held = lambda a: pl.BlockSpec((None,) + a.shape[1:], lambda i: (layer, 0, 0), pipeline_mode=pl.Buffered(1))
    return pl.pallas_call(
        functools.partial(_mix_ffn_body, bb=bb, tt=tt, final=final, cuts=cuts),
        grid=(n // tm,),
        in_specs=[pl.BlockSpec((oa3.shape[0], tm, LANES), lambda i: (0, i, 0)), tok(o_r), tok(oc), tok(x2)]
        + [_mod_spec(layer, which, bb, d, n_t) for which in (GATE1, SHIFT2, SCALE2, GATE2)]
        + [held(w_out_bf), held(w_ffn_in_bf), held(w_ffn_out_bf), pl.BlockSpec((1, d), lambda i: (0, 0))],
        out_specs=pl.BlockSpec((tm, d), lambda i: (i, 0)),
        out_shape=jax.ShapeDtypeStruct((n, d), F32),
        compiler_params=_params(("arbitrary",)),
        name="mix_ffn",
    )(oa3, o_r, oc, x2, planes, planes, planes, planes, w_out_bf, w_ffn_in_bf, w_ffn_out_bf, final_w.reshape(1, d))


def _kv_layout_body(*refs, depth):
    ins, outs = refs[:2 * depth], refs[2 * depth:]
    for which, o_ref in enumerate(outs):
        for l in range(depth):
            src = ins[which * depth + l]
            for p in range(src.shape[0]):
                o_ref[l, 0, p * LANES:(p + 1) * LANES, :] = src[p].T


def _kv_layout(ks, vs, batch, seq):
    depth = len(ks)
    n_pairs = ks[0].shape[0]
    d_att = n_pairs * LANES
    tt = min(seq, ROW_TILE)
    n_t = seq // tt
    src = pl.BlockSpec((n_pairs, tt, LANES), lambda b, j: (0, b * n_t + j, 0))
    dst = pl.BlockSpec((depth, 1, d_att, tt), lambda b, j: (0, b, 0, j))
    shape = jax.ShapeDtypeStruct((depth, batch, d_att, seq), F32)
    k_t, v_t = pl.pallas_call(
        functools.partial(_kv_layout_body, depth=depth),
        grid=(batch, n_t),
        in_specs=[src] * (2 * depth),
        out_specs=[dst, dst],
        out_shape=[shape, shape],
        compiler_params=_params(("arbitrary", "arbitrary")),
        name="kv_layout",
    )(*ks, *vs)
    heads = lambda a: jnp.transpose(a.reshape(depth, batch, d_att // HEAD_DIM, HEAD_DIM, seq), (0, 1, 4, 2, 3))
    return heads(k_t), heads(v_t)


def _trunk(x, mods, pos, cache_k, cache_v, state_hgrn, state_conv, weights):
    (w_in, conv_w, conv_b, lb_logits, hgrn_norm_w, w_out, w_ffn_in, w_ffn_out, final_norm_w) = weights
    batch, seq, d = x.shape
    depth = w_in.shape[0]
    d_conv = conv_w.shape[-1]
    d_att = d_rec = (w_in.shape[-1] - 3 * d_conv) // 7
    n_heads = d_att // HEAD_DIM
    widths = (d_att,) * 3 + (d_rec,) * 4 + (d_conv,) * 3
    bb, tt = _token_tile(batch, seq)
    cos_t, sin_t = _rope_tables(pos, n_heads)
    if bb > 1:
        cos_t, sin_t = jnp.tile(cos_t, (bb, 1)), jnp.tile(sin_t, (bb, 1))
    x2 = x.reshape(batch * seq, d)
    planes = _mod_planes(mods, d)
    ks, vs, hs, cs = [], [], [], []
    for l in range(depth):
        qa, ka, va, qr, fr, ir, gr, bc, cc, xc = _in_proj(
            x2, planes, w_in, l, cos_t, sin_t, batch, seq, widths, 3)
        if cache_k is None:
            oa = _prompt_attention(qa, ka, va, batch, seq)
            s0 = jnp.zeros((batch, n_heads, HEAD_DIM, HEAD_DIM), F32)
            conv0 = jnp.zeros((batch, conv_w.shape[1] - 1, d_conv), F32)
        else:
            oa = _cache_attention(qa, ka, va, cache_k, cache_v, l, batch, seq)
            conv0 = state_conv[l]
        if cache_k is None:
            o_r, s_new = _hgrn(qr, fr, ir, gr, s0, lb_logits, hgrn_norm_w[l], l, batch, seq)
        else:
            o_r, s_new = _hgrn_step(qr, fr, ir, gr, state_hgrn, lb_logits, hgrn_norm_w[l], l, batch, seq)
        oc, conv_new = _short_conv(bc, cc, xc, conv0, conv_w[l], conv_b[l], batch, seq)
        x2 = _mix_ffn(oa, o_r, oc, x2, planes, w_out, w_ffn_in, w_ffn_out, l, final_norm_w, batch, seq, l == depth - 1)
        ks.append(ka)
        vs.append(va)
        hs.append(s_new)
        cs.append(conv_new)
    keep = min(WINDOW_STEPS * max(DILATIONS), seq)
    if seq % LANES == 0:
        k_out, v_out = _kv_layout(ks, vs, batch, seq)
    else:
        unpair = lambda parts: jnp.transpose(
            jnp.stack(parts).reshape(depth, n_heads // 2, batch, seq, 2, HEAD_DIM),
            (0, 2, 3, 1, 4, 5)).reshape(depth, batch, seq, n_heads, HEAD_DIM)
        k_out, v_out = unpair(ks), unpair(vs)
    return (x2.reshape(batch, seq, d), k_out[:, :, seq - keep:], v_out[:, :, seq - keep:],
            jnp.stack(hs), jnp.stack(cs))


def kernel(x_prompt, x_sample, cache_k, cache_v, state_hgrn, state_conv, c_prompt, c_sample,
           w_ada, b_ada, w_in, conv_w, conv_b, hgrn_lb_logits, hgrn_norm_w, w_out,
           w_ffn_in, w_ffn_out, final_norm_w):
    past_len = PAST_LEN
    assert cache_k.shape[2] == min(WINDOW_STEPS * max(DILATIONS), PAST_LEN) == PAST_LEN
    mods_p, mods_s = _ada((c_prompt, c_sample), w_ada, b_ada)
    weights = (w_in.astype(BF16), conv_w, conv_b, hgrn_lb_logits.astype(F32), hgrn_norm_w,
               w_out.astype(BF16), w_ffn_in.astype(BF16), w_ffn_out.astype(BF16), final_norm_w)
    pos_p = jnp.arange(x_prompt.shape[1], dtype=jnp.int32)
    pos_s = past_len + jnp.arange(x_sample.shape[1], dtype=jnp.int32)
    out_p = _trunk(x_prompt, mods_p, pos_p, None, None, None, None, weights)
    out_s = _trunk(x_sample, mods_s, pos_s, cache_k, cache_v, state_hgrn, state_conv, weights)
    return (out_p[0], out_s[0]) + out_p[1:] + out_s[1:]
```

```python
import functools

import jax
import jax.numpy as jnp
from jax import lax
from jax.experimental import pallas as pl
from jax.experimental.pallas import tpu as pltpu

F32 = jnp.float32
BF16 = jnp.bfloat16

HEAD_DIM = 64
LANES = 128
DILATIONS = (1, 4, 16)
WINDOW_STEPS = 128
PAST_LEN = 2048
ROPE_THETA = 10000.0
NORM_EPS = 1e-6
REC_CHUNK = 16
UNROLL = 16
LOG2_E = 1.4426950408889634
ROW_TILE = 512
MXU_WIDTH = 256
FFN_COLS = 1536
CACHE_BATCH = 2
VMEM_LIMIT = 48 * 1024 * 1024


def _params(sem):
    return pltpu.CompilerParams(dimension_semantics=sem, vmem_limit_bytes=VMEM_LIMIT)


def _dot(a, b):
    return jnp.dot(a, b, preferred_element_type=F32)


def _dot_nt(a, b):
    return lax.dot_general(a, b, (((1,), (1,)), ((), ())), preferred_element_type=F32)


def _dot_tn(a, b):
    return lax.dot_general(a, b, (((0,), (0,)), ((), ())), preferred_element_type=F32)


def _split3(x):
    hi = x.astype(BF16)
    r = x - hi.astype(F32)
    mid = r.astype(BF16)
    lo = (r - mid.astype(F32)).astype(BF16)
    return hi, mid, lo


def _silu(x):
    return x * jax.nn.sigmoid(x)


def _rms(x):
    return x * lax.rsqrt(jnp.mean(x * x, axis=-1, keepdims=True) + NORM_EPS)


def _token_tile(batch, seq):
    if seq >= ROW_TILE:
        assert seq % ROW_TILE == 0
        return 1, ROW_TILE
    assert ROW_TILE % seq == 0 and seq % 8 == 0
    bb = min(batch, ROW_TILE // seq)
    assert batch % bb == 0
    return bb, seq


def _ada_body(*refs):
    n = (len(refs) - 2) // 2
    c_refs, (w_ref, b_ref), o_refs = refs[:n], refs[n:n + 2], refs[n + 2:]
    w_hi, w_mid, _ = _split3(w_ref[0])
    for c_ref, o_ref in zip(c_refs, o_refs):
        s_hi, s_mid, _ = _split3(_silu(c_ref[...]))
        o_ref[0] = _dot(s_hi, w_hi) + _dot(s_hi, w_mid) + _dot(s_mid, w_hi) + b_ref[0]


def _ada(cs, w_ada, b_ada):
    depth, d, n6 = w_ada.shape
    tn = 512
    return pl.pallas_call(
        _ada_body,
        grid=(depth, n6 // tn),
        in_specs=[pl.BlockSpec(c.shape, lambda l, j: (0, 0)) for c in cs]
        + [pl.BlockSpec((1, d, tn), lambda l, j: (l, 0, j)),
           pl.BlockSpec((1, 1, tn), lambda l, j: (l, 0, j))],
        out_specs=[pl.BlockSpec((1, c.shape[0], tn), lambda l, j: (l, 0, j)) for c in cs],
        out_shape=[jax.ShapeDtypeStruct((depth, c.shape[0], n6), F32) for c in cs],
        compiler_params=_params(("arbitrary", "arbitrary")),
        name="ada",
    )(*cs, w_ada, b_ada.reshape(depth, 1, n6))


def _mod_planes(mods, d):
    depth, batch, _ = mods.shape
    return jnp.transpose(mods.reshape(depth, batch, 6, 1, d), (0, 2, 1, 3, 4))


def _mod_spec(layer, which, bb, d, n_t):
    return pl.BlockSpec((None, None, bb, 1, d), lambda i: (layer, which, i // n_t, 0, 0))


SHIFT1, SCALE1, GATE1, SHIFT2, SCALE2, GATE2 = range(6)


def _rope_tables(pos, n_heads):
    half = HEAD_DIM // 2
    freqs = ROPE_THETA ** (-jnp.arange(half, dtype=F32) / half)
    ang = pos.astype(F32)[:, None] * freqs[None, :]
    cos, sin = jnp.cos(ang), jnp.sin(ang)
    cos_t = jnp.tile(jnp.concatenate([cos, cos], axis=-1), (1, n_heads))
    sin_t = jnp.tile(jnp.concatenate([-sin, sin], axis=-1), (1, n_heads))
    return cos_t, sin_t


def _rope(x, cos, sin_signed):
    outs = []
    for c in range(x.shape[-1] // LANES):
        sl = slice(c * LANES, (c + 1) * LANES)
        xc = x[:, sl]
        lane = lax.broadcasted_iota(jnp.int32, xc.shape, 1)
        ahead = pltpu.roll(xc, LANES - HEAD_DIM // 2, axis=1)
        behind = pltpu.roll(xc, HEAD_DIM // 2, axis=1)
        rot = jnp.where((lane & (HEAD_DIM // 2)) == 0, ahead, behind)
        outs.append(xc * cos[:, sl] + rot * sin_signed[:, sl])
    return jnp.concatenate(outs, axis=-1)


def _in_proj_body(x_ref, sh_ref, sc_ref, w_ref, cos_ref, sin_ref, *out_refs, bb, tt, widths, n_paired, runs):
    d = x_ref.shape[-1]
    x = x_ref[...].reshape(bb, tt, d)
    h = _rms(x) * (1.0 + sc_ref[...]) + sh_ref[...]
    hb = h.reshape(bb * tt, d).astype(BF16)
    for first_out, n_out in runs:
        off, wd = sum(widths[:first_out]), widths[first_out]
        group = _dot(hb, w_ref[:, off:off + wd * n_out])
        for n in range(n_out):
            i = first_out + n
            p = group[:, n * wd:(n + 1) * wd]
            if i < 2:
                p = _rope(p, cos_ref[...], sin_ref[...])
            if i < n_paired:
                for c in range(wd // LANES):
                    out_refs[i][c] = p[:, c * LANES:(c + 1) * LANES]
            else:
                out_refs[i][...] = p


def _equal_runs(widths):
    runs = []
    for i, wd in enumerate(widths):
        if runs and widths[runs[-1][0]] == wd:
            runs[-1][1] += 1
        else:
            runs.append([i, 1])
    return tuple((a, n) for a, n in runs)


def _in_proj(x2, planes, w_in_bf, layer, cos_t, sin_t, batch, seq, widths, n_paired):
    n, d = x2.shape
    bb, tt = _token_tile(batch, seq)
    tm = bb * tt
    n_t = seq // tt
    table_blocks = cos_t.shape[0] // tm
    body = functools.partial(_in_proj_body, bb=bb, tt=tt, widths=widths, n_paired=n_paired, runs=_equal_runs(widths))
    tab_spec = pl.BlockSpec((tm, cos_t.shape[1]), lambda i: (i % table_blocks, 0))
    out_specs, out_shape = [], []
    for idx, wd in enumerate(widths):
        if idx < n_paired:
            out_specs.append(pl.BlockSpec((wd // LANES, tm, LANES), lambda i: (0, i, 0)))
            out_shape.append(jax.ShapeDtypeStruct((wd // LANES, n, LANES), F32))
        else:
            out_specs.append(pl.BlockSpec((tm, wd), lambda i: (i, 0)))
            out_shape.append(jax.ShapeDtypeStruct((n, wd), F32))
    return pl.pallas_call(
        body,
        grid=(n // tm,),
        in_specs=[pl.BlockSpec((tm, d), lambda i: (i, 0)),
                  _mod_spec(layer, SHIFT1, bb, d, n_t), _mod_spec(layer, SCALE1, bb, d, n_t),
                  pl.BlockSpec((None,) + w_in_bf.shape[1:], lambda i: (layer, 0, 0)), tab_spec, tab_spec],
        out_specs=out_specs,
        out_shape=out_shape,
        compiler_params=_params(("arbitrary",)),
        name="in_proj",
    )(x2, planes, planes, w_in_bf, cos_t, sin_t)


def _prompt_attn_body(q_ref, k_ref, v_ref, o_ref, m_ref, l_ref, acc_ref):
    n_pairs, seq, _ = q_ref.shape
    blk = WINDOW_STEPS
    lane = lax.broadcasted_iota(jnp.int32, (blk, LANES), 1)
    first = lane < HEAD_DIM
    scale = HEAD_DIM ** -0.5 * LOG2_E

    def rows(start, dil):
        return pl.ds(start, blk) if dil == 1 else pl.ds(start, blk, stride=dil)

    order = sorted(DILATIONS, reverse=True)
    for b_idx, dil in enumerate(order):
        n_blk = seq // dil // blk
        has_prev = n_blk > 1
        n_keys = 2 * blk if has_prev else blk
        q_row = lax.broadcasted_iota(jnp.int32, (2 * blk, n_keys), 0) & (blk - 1)
        k_col = lax.broadcasted_iota(jnp.int32, (2 * blk, n_keys), 1)
        is_first, is_last = b_idx == 0, b_idx == len(DILATIONS) - 1

        def unit(u, carry, dil=dil, has_prev=has_prev, q_row=q_row, k_col=k_col,
                 is_first=is_first, is_last=is_last):
            res = u & (dil - 1)
            i = u >> (dil.bit_length() - 1)
            start = res + dil * blk * i
            if dil == 1:
                start = pl.multiple_of(start, blk)
            if has_prev:
                prev_start = jnp.maximum(start - dil * blk, res)
                lo = jnp.maximum(q_row, jnp.where(i > 0, 0, blk))
                valid = (k_col >= lo) & (k_col <= q_row + blk)
            else:
                valid = k_col <= q_row
            loaded = []
            for p in range(n_pairs):
                q = q_ref[p, rows(start, dil), :] * scale
                kc = k_ref[p, rows(start, dil), :]
                vc = v_ref[p, rows(start, dil), :]
                if has_prev:
                    kc = jnp.concatenate([k_ref[p, rows(prev_start, dil), :], kc], axis=0)
                    vc = jnp.concatenate([v_ref[p, rows(prev_start, dil), :], vc], axis=0)
                old = None
                if not is_first:
                    old = (m_ref[p, rows(start, dil), :], l_ref[p, rows(start, dil), :],
                           acc_ref[p, rows(start, dil), :])
                loaded.append((q, kc, vc, old))
            scores = []
            for q, kc, vc, old in loaded:
                q2 = jnp.concatenate([jnp.where(first, q, 0.0), jnp.where(first, 0.0, q)], axis=0)
                scores.append(_dot_nt(q2.astype(BF16), kc.astype(BF16)))
            results = []
            for (q, kc, vc, old), s in zip(loaded, scores):
                s = jnp.where(valid, s, -jnp.inf)
                m2 = jnp.max(s, axis=-1, keepdims=True)
                e = jnp.exp2(s - m2).astype(BF16)
                v1 = jnp.concatenate([vc.astype(BF16), jnp.ones(vc.shape, BF16)], axis=1)
                pv = _dot(e, v1)
                m_b = jnp.where(first, m2[:blk], m2[blk:])
                l_b = jnp.where(first, pv[:blk, LANES:], pv[blk:, LANES:])
                acc_b = jnp.where(first, pv[:blk, :LANES], pv[blk:, :LANES])
                if not is_first:
                    m_o, l_o, acc_o = old
                    m_n = jnp.maximum(m_o, m_b)
                    w_o, w_b = jnp.exp2(m_o - m_n), jnp.exp2(m_b - m_n)
                    l_b = w_o * l_o + w_b * l_b
                    acc_b = w_o * acc_o + w_b * acc_b
                    m_b = m_n
                results.append((m_b, l_b, acc_b))
            for p, (m_b, l_b, acc_b) in enumerate(results):
                if is_last:
                    o_ref[p, rows(start, dil), :] = acc_b / l_b
                else:
                    m_ref[p, rows(start, dil), :] = m_b
                    l_ref[p, rows(start, dil), :] = l_b
                    acc_ref[p, rows(start, dil), :] = acc_b
            return carry

        lax.fori_loop(0, dil * n_blk, unit, 0, unroll=4)


def _prompt_attention(q3, k3, v3, batch, seq):
    n_pairs = q3.shape[0]
    assert seq % (WINDOW_STEPS * max(DILATIONS)) == 0
    spec = pl.BlockSpec((n_pairs, seq, LANES), lambda b: (0, b, 0))
    return pl.pallas_call(
        _prompt_attn_body,
        grid=(batch,),
        in_specs=[spec, spec, spec],
        out_specs=spec,
        out_shape=jax.ShapeDtypeStruct(q3.shape, F32),
        scratch_shapes=[pltpu.VMEM((n_pairs, seq, LANES), F32)] * 3,
        compiler_params=_params(("arbitrary",)),
        name="prompt_attn",
    )(q3, k3, v3)


def _branch_count(delta):
    cnt = jnp.zeros(delta.shape, F32)
    for dil in DILATIONS:
        hit = (delta >= 0) & (delta <= WINDOW_STEPS * dil) & ((delta & (dil - 1)) == 0)
        cnt = cnt + jnp.where(hit, 1.0, 0.0)
    return cnt


def _cache_attn_body(q_ref, kn_ref, vn_ref, kt_ref, vt_ref, o_ref, *, past, t_new):
    n_pairs = q_ref.shape[0]
    t_c = lax.broadcasted_iota(jnp.int32, (t_new, past), 0)
    j_c = lax.broadcasted_iota(jnp.int32, (t_new, past), 1)
    cnt_c = _branch_count(past + t_c - j_c)
    t_n = lax.broadcasted_iota(jnp.int32, (t_new, t_new), 0)
    j_n = lax.broadcasted_iota(jnp.int32, (t_new, t_new), 1)
    cnt_n = _branch_count(t_n - j_n)
    scale = HEAD_DIM ** -0.5
    for bi in range(kt_ref.shape[1]):
        rs = slice(bi * t_new, (bi + 1) * t_new)
        for p in range(n_pairs):
            q_pair, kn_pair, vn_pair = q_ref[p, rs, :] * scale, kn_ref[p, rs, :], vn_ref[p, rs, :]
            outs = []
            for hh in range(LANES // HEAD_DIM):
                h = p * (LANES // HEAD_DIM) + hh
                sl = slice(hh * HEAD_DIM, (hh + 1) * HEAD_DIM)
                q = q_pair[:, sl]
                s_c = jnp.where(cnt_c > 0, _dot(q.astype(BF16), kt_ref[0, bi, h].astype(BF16)), -jnp.inf)
                s_n = jnp.where(cnt_n > 0, _dot_nt(q, kn_pair[:, sl]), -jnp.inf)
                m = jnp.maximum(jnp.max(s_c, axis=-1, keepdims=True), jnp.max(s_n, axis=-1, keepdims=True))
                p_c = cnt_c * jnp.exp(s_c - m)
                p_n = cnt_n * jnp.exp(s_n - m)
                den = jnp.sum(p_c, axis=-1, keepdims=True) + jnp.sum(p_n, axis=-1, keepdims=True)
                acc = _dot_nt(p_c.astype(BF16), vt_ref[0, bi, h].astype(BF16)) + _dot(p_n, vn_pair[:, sl])
                outs.append(acc / den)
            o_ref[p, rs, :] = jnp.concatenate(outs, axis=-1)


def _cache_attention(q3, kn3, vn3, cache_k, cache_v, layer, batch, seq):
    depth, _, past, n_heads, hd = cache_k.shape
    n_pairs = q3.shape[0]
    bb = CACHE_BATCH if batch % CACHE_BATCH == 0 else 1
    by_head = lambda a: jnp.transpose(a, (0, 1, 3, 4, 2))
    new = pl.BlockSpec((n_pairs, bb * seq, LANES), lambda b: (0, b, 0))
    old = pl.BlockSpec((1, bb, n_heads, hd, past), lambda b: (layer, b, 0, 0, 0))
    return pl.pallas_call(
        functools.partial(_cache_attn_body, past=past, t_new=seq),
        grid=(batch // bb,),
        in_specs=[new, new, new, old, old],
        out_specs=new,
        out_shape=jax.ShapeDtypeStruct(q3.shape, F32),
        compiler_params=_params(("arbitrary",)),
        name="cache_attn",
    )(q3, kn3, vn3, by_head(cache_k), by_head(cache_v))


def _lower_bound(logits, layer):
    e = jnp.exp(logits - jnp.max(logits, axis=0, keepdims=True))
    sm = e / jnp.sum(e, axis=0, keepdims=True)
    lb = jnp.zeros_like(sm[0:1])
    for j in range(1, layer + 1):
        lb = lb + sm[j:j + 1]
    return lb


def _hgrn_body(lbl_ref, nw_ref, q_ref, f_ref, i_ref, g_ref, s0_ref, o_ref, sout_ref,
               st_ref, b2_ref, kk_ref, eb_ref, qt_ref, kh_ref, o_s_ref, *, layer):
    c = REC_CHUNK
    rows, d_rec = q_ref.shape[1], q_ref.shape[2]
    n_pairs = d_rec // LANES
    n_chunks = rows // c
    grp = min(rows, LANES)

    @pl.when(pl.program_id(1) == 0)
    def _():
        st_ref[...] = s0_ref[0]

    lb = _lower_bound(lbl_ref[...], layer)
    log_lb = jnp.log(lb)
    z = f_ref[0]
    t = jnp.exp(-jnp.abs(z))
    log_sig = jnp.minimum(z, 0.0) - jnp.log(1.0 + t)
    b_term = jnp.log1p(-lb) + log_sig
    log_f = jnp.maximum(log_lb, b_term) + jnp.log(1.0 + jnp.exp(-jnp.abs(log_lb - b_term)))
    inv = 1.0 / (1.0 + t)
    kk = (1.0 - lb) * jnp.where(z > 0.0, t * inv, inv)
    r_i = lax.broadcasted_iota(jnp.int32, (grp, grp), 0)
    c_i = lax.broadcasted_iota(jnp.int32, (grp, grp), 1)
    same_chunk = (r_i >> (c.bit_length() - 1)) == (c_i >> (c.bit_length() - 1))
    prefix = jnp.where(same_chunk & (c_i <= r_i), 1.0, 0.0).astype(BF16)
    suffix = jnp.where(same_chunk & (c_i > r_i), 1.0, 0.0).astype(BF16)
    for g in range(rows // grp):
        gs = slice(g * grp, (g + 1) * grp)
        parts = _split3(log_f[gs])
        b = sum(_dot(prefix, part) for part in parts)
        r = sum(_dot(suffix, part) for part in parts)
        eb = jnp.exp(b)
        b2_ref[gs, :] = b * LOG2_E
        eb_ref[gs, :] = eb
        qt_ref[gs, :] = (q_ref[0, gs, :] * eb).astype(BF16)
        kh_ref[gs, :] = (kk[gs] * jnp.exp(r)).astype(BF16)
    kk_ref[...] = kk

    l_r = lax.broadcasted_iota(jnp.int32, (LANES, LANES), 0) >> (HEAD_DIM.bit_length() - 1)
    l_c = lax.broadcasted_iota(jnp.int32, (LANES, LANES), 1) >> (HEAD_DIM.bit_length() - 1)
    same_head = l_r == l_c
    head_sum = jnp.where(same_head, 1.0, 0.0).astype(BF16)
    row = lax.broadcasted_iota(jnp.int32, (c, LANES), 0)

    def chunk(ci, carry):
        r0 = pl.multiple_of(ci * c, c)
        rs = pl.ds(r0, c)
        xs = []
        for p in range(n_pairs):
            sl = slice(p * LANES, (p + 1) * LANES)
            b2, q, k = b2_ref[rs, sl], q_ref[0, rs, sl], kk_ref[rs, sl]
            for s_i in range(c):
                cap = jnp.where(row >= s_i, 0.0, -jnp.inf)
                e = jnp.exp2(jnp.minimum(b2 - b2[s_i:s_i + 1], cap))
                xs.append((q * e) * k[s_i:s_i + 1])
        a_rep = _dot(jnp.concatenate(xs, axis=0).astype(BF16), head_sum)
        for p in range(n_pairs):
            sl = slice(p * LANES, (p + 1) * LANES)
            v = i_ref[0, rs, sl]
            base = p * c * c
            o = a_rep[base:base + c] * v[0:1]
            for s_i in range(1, c):
                o = o + a_rep[base + s_i * c:base + (s_i + 1) * c] * v[s_i:s_i + 1]
            st = st_ref[p]
            o = o + _dot_nt(qt_ref[rs, sl], st.astype(BF16))
            upd = _dot_tn(v.astype(BF16), kh_ref[rs, sl])
            st_ref[p] = st * eb_ref[rs, sl][c - 1:c] + jnp.where(same_head, upd, 0.0)
            o_s_ref[rs, sl] = o
        return carry

    lax.fori_loop(0, n_chunks, chunk, 0, unroll=UNROLL if n_chunks % UNROLL == 0 else 1)

    g_all = g_ref[0]
    outs = []
    for p in range(n_pairs):
        sl = slice(p * LANES, (p + 1) * LANES)
        o = o_s_ref[:, sl]
        sq_hi, sq_mid, _ = _split3(o * o)
        ms = (_dot(sq_hi, head_sum) + _dot(sq_mid, head_sum)) * (1.0 / HEAD_DIM)
        outs.append(o * lax.rsqrt(ms + NORM_EPS) * nw_ref[:, sl] * _silu(g_all[:, sl]))
    o_ref[0] = jnp.concatenate(outs, axis=-1)

    @pl.when(pl.program_id(1) == pl.num_programs(1) - 1)
    def _():
        sout_ref[0] = st_ref[...]


def _pair_state(s):
    b, h, kd, vd = s.shape
    st = jnp.swapaxes(s, 2, 3).reshape(b, h // 2, 2, vd, kd)
    z = jnp.zeros_like(st[:, :, 0])
    top = jnp.concatenate([st[:, :, 0], z], axis=-1)
    bot = jnp.concatenate([z, st[:, :, 1]], axis=-1)
    return jnp.concatenate([top, bot], axis=-2)


def _unpair_state(sp):
    b, hp = sp.shape[:2]
    s0 = sp[:, :, :HEAD_DIM, :HEAD_DIM]
    s1 = sp[:, :, HEAD_DIM:, HEAD_DIM:]
    st = jnp.stack([s0, s1], axis=2).reshape(b, 2 * hp, HEAD_DIM, HEAD_DIM)
    return jnp.swapaxes(st, 2, 3)


def _hgrn(qr, fr, ir, gr, s0, lb_logits, norm_w, layer, batch, seq):
    d_rec = qr.shape[-1]
    n_pairs = d_rec // LANES
    tb = min(seq, ROW_TILE)
    assert seq % tb == 0 and tb % LANES == 0 and LANES % REC_CHUNK == 0
    view = lambda a: a.reshape(batch, seq, d_rec)
    tok = pl.BlockSpec((1, tb, d_rec), lambda b, t: (b, t, 0))
    state = pl.BlockSpec((1, n_pairs, LANES, LANES), lambda b, t: (b, 0, 0, 0))
    o, s_new = pl.pallas_call(
        functools.partial(_hgrn_body, layer=layer),
        grid=(batch, seq // tb),
        in_specs=[pl.BlockSpec(lb_logits.shape, lambda b, t: (0, 0)),
                  pl.BlockSpec((1, d_rec), lambda b, t: (0, 0)),
                  tok, tok, tok, tok, state],
        out_specs=[tok, state],
        out_shape=[jax.ShapeDtypeStruct((batch, seq, d_rec), F32),
                   jax.ShapeDtypeStruct((batch, n_pairs, LANES, LANES), F32)],
        scratch_shapes=[pltpu.VMEM((n_pairs, LANES, LANES), F32)]
        + [pltpu.VMEM((tb, d_rec), F32)] * 3 + [pltpu.VMEM((tb, d_rec), BF16)] * 2
        + [pltpu.VMEM((tb, d_rec), F32)],
        compiler_params=_params(("arbitrary", "arbitrary")),
        name="hgrn",
    )(lb_logits, norm_w.reshape(1, d_rec), view(qr), view(fr), view(ir), view(gr), _pair_state(s0))
    return o.reshape(batch * seq, d_rec), _unpair_state(s_new)


def _hgrn_step_body(lblt_ref, nwt_ref, q_ref, f_ref, i_ref, g_ref, s0_ref, o_ref, sout_ref, *, layer):
    n_t, hd, _ = q_ref.shape
    logits = lblt_ref[...]
    e = jnp.exp(logits - jnp.max(logits, axis=1, keepdims=True))
    sm = e / jnp.sum(e, axis=1, keepdims=True)
    lb = jnp.zeros_like(sm[:, 0:1])
    for j in range(1, layer + 1):
        lb = lb + sm[:, j:j + 1]
    for t in range(n_t):
        z = f_ref[t]
        f_t = lb + (1.0 - lb) * jax.nn.sigmoid(z)
        k_t = (1.0 - lb) * jax.nn.sigmoid(-z)
        q_t, v_t = q_ref[t], i_ref[t]
        src = s0_ref if t == 0 else sout_ref
        o_t = jnp.zeros(v_t.shape, F32)
        for k in range(hd):
            s_k = f_t[k:k + 1] * src[0, 0, k] + k_t[k:k + 1] * v_t
            sout_ref[0, 0, k] = s_k
            o_t = o_t + s_k * q_t[k:k + 1]
        ms = jnp.mean(o_t * o_t, axis=0, keepdims=True)
        o_ref[t] = o_t * lax.rsqrt(ms + NORM_EPS) * nwt_ref[...] * _silu(g_ref[t])


def _hgrn_step(qr, fr, ir, gr, state_all, lb_logits, norm_w, layer, batch, seq):
    d_rec = qr.shape[-1]
    n_heads = d_rec // HEAD_DIM
    to_lanes = lambda a: jnp.transpose(a.reshape(batch, seq, d_rec), (1, 2, 0))
    tok = pl.BlockSpec((seq, HEAD_DIM, batch), lambda h: (0, h, 0))
    state_in = pl.BlockSpec((1, 1, HEAD_DIM, HEAD_DIM, batch), lambda h: (layer, h, 0, 0, 0))
    state_out = pl.BlockSpec((1, 1, HEAD_DIM, HEAD_DIM, batch), lambda h: (0, h, 0, 0, 0))
    o_t, s_new = pl.pallas_call(
        functools.partial(_hgrn_step_body, layer=layer),
        grid=(n_heads,),
        in_specs=[pl.BlockSpec((HEAD_DIM, lb_logits.shape[0]), lambda h: (h, 0)),
                  pl.BlockSpec((HEAD_DIM, 1), lambda h: (h, 0)),
                  tok, tok, tok, tok, state_in],
        out_specs=[tok, state_out],
        out_shape=[jax.ShapeDtypeStruct((seq, d_rec, batch), F32),
                   jax.ShapeDtypeStruct((1, n_heads, HEAD_DIM, HEAD_DIM, batch), F32)],
        compiler_params=_params(("arbitrary",)),
        name="hgrn_step",
    )(lb_logits.T, norm_w.reshape(d_rec, 1), to_lanes(qr), to_lanes(fr), to_lanes(ir), to_lanes(gr),
      jnp.transpose(state_all, (0, 2, 3, 4, 1)))
    o = jnp.transpose(o_t, (2, 0, 1)).reshape(batch * seq, d_rec)
    return o, jnp.transpose(s_new[0], (3, 0, 1, 2))


CONV_PAD = 8


def _conv_body(bc_ref, cc_ref, xc_ref, buf_ref, w_ref, b_ref, oc_ref, new_ref, up_ref):
    seq = cc_ref.shape[1]
    width = w_ref.shape[0]
    u = cc_ref[0] * xc_ref[0]
    up_ref[pl.ds(CONV_PAD, seq), :] = u
    up_ref[pl.ds(CONV_PAD - (width - 1), width - 1), :] = buf_ref[0]
    y = u * w_ref[width - 1:width, :] + b_ref[...]
    for i in range(width - 1):
        y = y + up_ref[pl.ds(CONV_PAD - (width - 1) + i, seq), :] * w_ref[i:i + 1, :]
    oc_ref[0] = bc_ref[0] * y
    new_ref[0] = up_ref[pl.ds(CONV_PAD + seq - (width - 1), width - 1), :]


def _short_conv(bc, cc, xc, buf, conv_w, conv_b, batch, seq):
    d_conv = bc.shape[-1]
    width = conv_w.shape[0]
    view = lambda a: a.reshape(batch, seq, d_conv)
    tok = pl.BlockSpec((1, seq, d_conv), lambda b: (b, 0, 0))
    tail = pl.BlockSpec((1, width - 1, d_conv), lambda b: (b, 0, 0))
    oc, new = pl.pallas_call(
        _conv_body,
        grid=(batch,),
        in_specs=[tok, tok, tok, tail,
                  pl.BlockSpec((width, d_conv), lambda b: (0, 0)),
                  pl.BlockSpec((1, d_conv), lambda b: (0, 0))],
        out_specs=[tok, tail],
        out_shape=[jax.ShapeDtypeStruct((batch, seq, d_conv), F32),
                   jax.ShapeDtypeStruct((batch, width - 1, d_conv), F32)],
        scratch_shapes=[pltpu.VMEM((CONV_PAD + seq, d_conv), F32)],
        compiler_params=_params(("arbitrary",)),
        name="short_conv",
    )(view(bc), view(cc), view(xc), buf, conv_w, conv_b.reshape(1, d_conv))
    return oc.reshape(batch * seq, d_conv), new


def _mix_ffn_body(oa_ref, or_ref, oc_ref, x_ref, g1_ref, sh_ref, sc_ref, g2_ref, w_ref, wi_ref, wo_ref,
                  fw_ref, o_ref, *, bb, tt, final, cuts):
    d = x_ref.shape[-1]
    d_ff = wo_ref.shape[0]
    n_pairs = oa_ref.shape[0]
    d_att, d_rec = n_pairs * LANES, or_ref.shape[-1]
    mix = (_dot(or_ref[...].astype(BF16), w_ref[d_att:d_att + d_rec, :])
           + _dot(oc_ref[...].astype(BF16), w_ref[d_att + d_rec:, :]))
    for p in range(n_pairs):
        mix = mix + _dot(oa_ref[p].astype(BF16), w_ref[p * LANES:(p + 1) * LANES, :])
    x = x_ref[...].reshape(bb, tt, d) + g1_ref[...] * mix.reshape(bb, tt, d)
    h = (_rms(x) * (1.0 + sc_ref[...]) + sh_ref[...]).reshape(bb * tt, d).astype(BF16)
    acc = None
    for c0, c1 in zip(cuts[:-1], cuts[1:]):
        a = (_silu(_dot(h, wi_ref[:, c0:c1])) * _dot(h, wi_ref[:, d_ff + c0:d_ff + c1])).astype(BF16)
        part = _dot(a, wo_ref[c0:c1, :])
        acc = part if acc is None else acc + part
    x = x + g2_ref[...] * acc.reshape(bb, tt, d)
    if final:
        x = _rms(x) * fw_ref[...]
    o_ref[...] = x.reshape(bb * tt, d)


def _mix_ffn(oa3, o_r, oc, x2, planes, w_out_bf, w_ffn_in_bf, w_ffn_out_bf, layer, final_w, batch, seq, final):
    n, d = x2.shape
    d_ff = w_ffn_out_bf.shape[1]
    bb, tt = _token_tile(batch, seq)
    tm = bb * tt
    n_t = seq // tt
    assert d_ff % MXU_WIDTH == 0
    n_cut = -(-d_ff // FFN_COLS)
    cuts = tuple(round(i * (d_ff // MXU_WIDTH) / n_cut) * MXU_WIDTH for i in range(n_cut + 1))
    tok = lambda a: pl.BlockSpec((tm, a.shape[-1]), lambda i: (i, 0))
    held = lambda a: pl.BlockSpec((None,) + a.shape[1:], lambda i: (layer, 0, 0), pipeline_mode=pl.Buffered(1))
    return pl.pallas_call(
        functools.partial(_mix_ffn_body, bb=bb, tt=tt, final=final, cuts=cuts),
        grid=(n // tm,),
        in_specs=[pl.BlockSpec((oa3.shape[0], tm, LANES), lambda i: (0, i, 0)), tok(o_r), tok(oc), tok(x2)]
        + [_mod_spec(layer, which, bb, d, n_t) for which in (GATE1, SHIFT2, SCALE2, GATE2)]
        + [held(w_out_bf), held(w_ffn_in_bf), held(w_ffn_out_bf), pl.BlockSpec((1, d), lambda i: (0, 0))],
        out_specs=pl.BlockSpec((tm, d), lambda i: (i, 0)),
        out_shape=jax.ShapeDtypeStruct((n, d), F32),
        compiler_params=_params(("arbitrary",)),
        name="mix_ffn",
    )(oa3, o_r, oc, x2, planes, planes, planes, planes, w_out_bf, w_ffn_in_bf, w_ffn_out_bf, final_w.reshape(1, d))


def _kv_layout_body(*refs, depth):
    ins, outs = refs[:2 * depth], refs[2 * depth:]
    for which, o_ref in enumerate(outs):
        for l in range(depth):
            src = ins[which * depth + l]
            for p in range(src.shape[0]):
                o_ref[l, 0, p * LANES:(p + 1) * LANES, :] = src[p].T


def _kv_layout(ks, vs, batch, seq):
    depth = len(ks)
    n_pairs = ks[0].shape[0]
    d_att = n_pairs * LANES
    tt = min(seq, ROW_TILE)
    n_t = seq // tt
    src = pl.BlockSpec((n_pairs, tt, LANES), lambda b, j: (0, b * n_t + j, 0))
    dst = pl.BlockSpec((depth, 1, d_att, tt), lambda b, j: (0, b, 0, j))
    shape = jax.ShapeDtypeStruct((depth, batch, d_att, seq), F32)
    k_t, v_t = pl.pallas_call(
        functools.partial(_kv_layout_body, depth=depth),
        grid=(batch, n_t),
        in_specs=[src] * (2 * depth),
        out_specs=[dst, dst],
        out_shape=[shape, shape],
        compiler_params=_params(("arbitrary", "arbitrary")),
        name="kv_layout",
    )(*ks, *vs)
    heads = lambda a: jnp.transpose(a.reshape(depth, batch, d_att // HEAD_DIM, HEAD_DIM, seq), (0, 1, 4, 2, 3))
    return heads(k_t), heads(v_t)


def _trunk(x, mods, pos, cache_k, cache_v, state_hgrn, state_conv, weights):
    (w_in, conv_w, conv_b, lb_logits, hgrn_norm_w, w_out, w_ffn_in, w_ffn_out, final_norm_w) = weights
    batch, seq, d = x.shape
    depth = w_in.shape[0]
    d_conv = conv_w.shape[-1]
    d_att = d_rec = (w_in.shape[-1] - 3 * d_conv) // 7
    n_heads = d_att // HEAD_DIM
    widths = (d_att,) * 3 + (d_rec,) * 4 + (d_conv,) * 3
    bb, tt = _token_tile(batch, seq)
    cos_t, sin_t = _rope_tables(pos, n_heads)
    if bb > 1:
        cos_t, sin_t = jnp.tile(cos_t, (bb, 1)), jnp.tile(sin_t, (bb, 1))
    x2 = x.reshape(batch * seq, d)
    planes = _mod_planes(mods, d)
    ks, vs, hs, cs = [], [], [], []
    for l in range(depth):
        qa, ka, va, qr, fr, ir, gr, bc, cc, xc = _in_proj(
            x2, planes, w_in, l, cos_t, sin_t, batch, seq, widths, 3)
        if cache_k is None:
            oa = _prompt_attention(qa, ka, va, batch, seq)
            s0 = jnp.zeros((batch, n_heads, HEAD_DIM, HEAD_DIM), F32)
            conv0 = jnp.zeros((batch, conv_w.shape[1] - 1, d_conv), F32)
        else:
            oa = _cache_attention(qa, ka, va, cache_k, cache_v, l, batch, seq)
            conv0 = state_conv[l]
        if cache_k is None:
            o_r, s_new = _hgrn(qr, fr, ir, gr, s0, lb_logits, hgrn_norm_w[l], l, batch, seq)
        else:
            o_r, s_new = _hgrn_step(qr, fr, ir, gr, state_hgrn, lb_logits, hgrn_norm_w[l], l, batch, seq)
        oc, conv_new = _short_conv(bc, cc, xc, conv0, conv_w[l], conv_b[l], batch, seq)
        x2 = _mix_ffn(oa, o_r, oc, x2, planes, w_out, w_ffn_in, w_ffn_out, l, final_norm_w, batch, seq, l == depth - 1)
        ks.append(ka)
        vs.append(va)
        hs.append(s_new)
        cs.append(conv_new)
    keep = min(WINDOW_STEPS * max(DILATIONS), seq)
    if seq % LANES == 0:
        k_out, v_out = _kv_layout(ks, vs, batch, seq)
    else:
        unpair = lambda parts: jnp.transpose(
            jnp.stack(parts).reshape(depth, n_heads // 2, batch, seq, 2, HEAD_DIM),
            (0, 2, 3, 1, 4, 5)).reshape(depth, batch, seq, n_heads, HEAD_DIM)
        k_out, v_out = unpair(ks), unpair(vs)
    return (x2.reshape(batch, seq, d), k_out[:, :, seq - keep:], v_out[:, :, seq - keep:],
            jnp.stack(hs), jnp.stack(cs))


def kernel(x_prompt, x_sample, cache_k, cache_v, state_hgrn, state_conv, c_prompt, c_sample,
           w_ada, b_ada, w_in, conv_w, conv_b, hgrn_lb_logits, hgrn_norm_w, w_out,
           w_ffn_in, w_ffn_out, final_norm_w):
    past_len = PAST_LEN
    assert cache_k.shape[2] == min(WINDOW_STEPS * max(DILATIONS), PAST_LEN) == PAST_LEN
    mods_p, mods_s = _ada((c_prompt, c_sample), w_ada, b_ada)
    weights = (w_in.astype(BF16), conv_w, conv_b, hgrn_lb_logits.astype(F32), hgrn_norm_w,
               w_out.astype(BF16), w_ffn_in.astype(BF16), w_ffn_out.astype(BF16), final_norm_w)
    pos_p = jnp.arange(x_prompt.shape[1], dtype=jnp.int32)
    pos_s = past_len + jnp.arange(x_sample.shape[1], dtype=jnp.int32)
    out_p = _trunk(x_prompt, mods_p, pos_p, None, None, None, None, weights)
    out_s = _trunk(x_sample, mods_s, pos_s, cache_k, cache_v, state_hgrn, state_conv, weights)
    return (out_p[0], out_s[0]) + out_p[1:] + out_s[1:]
```

```python
import functools

import jax
import jax.numpy as jnp
from jax import lax
from jax.experimental import pallas as pl
from jax.experimental.pallas import tpu as pltpu

F32 = jnp.float32
BF16 = jnp.bfloat16

HEAD_DIM = 64
LANES = 128
DILATIONS = (1, 4, 16)
WINDOW_STEPS = 128
PAST_LEN = 2048
ROPE_THETA = 10000.0
NORM_EPS = 1e-6
REC_CHUNK = 16
UNROLL = 32
LOG2_E = 1.4426950408889634
ROW_TILE = 512
MXU_WIDTH = 256
FFN_COLS = 1536
CACHE_BATCH = 2
VMEM_LIMIT = 48 * 1024 * 1024


def _params(sem):
    return pltpu.CompilerParams(dimension_semantics=sem, vmem_limit_bytes=VMEM_LIMIT)


def _dot(a, b):
    return jnp.dot(a, b, preferred_element_type=F32)


def _dot_nt(a, b):
    return lax.dot_general(a, b, (((1,), (1,)), ((), ())), preferred_element_type=F32)


def _dot_tn(a, b):
    return lax.dot_general(a, b, (((0,), (0,)), ((), ())), preferred_element_type=F32)


def _split3(x):
    hi = x.astype(BF16)
    r = x - hi.astype(F32)
    mid = r.astype(BF16)
    lo = (r - mid.astype(F32)).astype(BF16)
    return hi, mid, lo


def _silu(x):
    return x * jax.nn.sigmoid(x)


def _rms(x):
    return x * lax.rsqrt(jnp.mean(x * x, axis=-1, keepdims=True) + NORM_EPS)


def _token_tile(batch, seq):
    if seq >= ROW_TILE:
        assert seq % ROW_TILE == 0
        return 1, ROW_TILE
    assert ROW_TILE % seq == 0 and seq % 8 == 0
    bb = min(batch, ROW_TILE // seq)
    assert batch % bb == 0
    return bb, seq


def _ada_body(*refs):
    n = (len(refs) - 2) // 2
    c_refs, (w_ref, b_ref), o_refs = refs[:n], refs[n:n + 2], refs[n + 2:]
    w_hi, w_mid, _ = _split3(w_ref[0])
    for c_ref, o_ref in zip(c_refs, o_refs):
        s_hi, s_mid, _ = _split3(_silu(c_ref[...]))
        o_ref[0] = _dot(s_hi, w_hi) + _dot(s_hi, w_mid) + _dot(s_mid, w_hi) + b_ref[0]


def _ada(cs, w_ada, b_ada):
    depth, d, n6 = w_ada.shape
    tn = 512
    return pl.pallas_call(
        _ada_body,
        grid=(depth, n6 // tn),
        in_specs=[pl.BlockSpec(c.shape, lambda l, j: (0, 0)) for c in cs]
        + [pl.BlockSpec((1, d, tn), lambda l, j: (l, 0, j)),
           pl.BlockSpec((1, 1, tn), lambda l, j: (l, 0, j))],
        out_specs=[pl.BlockSpec((1, c.shape[0], tn), lambda l, j: (l, 0, j)) for c in cs],
        out_shape=[jax.ShapeDtypeStruct((depth, c.shape[0], n6), F32) for c in cs],
        compiler_params=_params(("arbitrary", "arbitrary")),
        name="ada",
    )(*cs, w_ada, b_ada.reshape(depth, 1, n6))


def _mod_planes(mods, d):
    depth, batch, _ = mods.shape
    return jnp.transpose(mods.reshape(depth, batch, 6, 1, d), (0, 2, 1, 3, 4))


def _mod_spec(layer, which, bb, d, n_t):
    return pl.BlockSpec((None, None, bb, 1, d), lambda i: (layer, which, i // n_t, 0, 0))


SHIFT1, SCALE1, GATE1, SHIFT2, SCALE2, GATE2 = range(6)


def _rope_tables(pos, n_heads):
    half = HEAD_DIM // 2
    freqs = ROPE_THETA ** (-jnp.arange(half, dtype=F32) / half)
    ang = pos.astype(F32)[:, None] * freqs[None, :]
    cos, sin = jnp.cos(ang), jnp.sin(ang)
    cos_t = jnp.tile(jnp.concatenate([cos, cos], axis=-1), (1, n_heads))
    sin_t = jnp.tile(jnp.concatenate([-sin, sin], axis=-1), (1, n_heads))
    return cos_t, sin_t


def _rope(x, cos, sin_signed):
    outs = []
    for c in range(x.shape[-1] // LANES):
        sl = slice(c * LANES, (c + 1) * LANES)
        xc = x[:, sl]
        lane = lax.broadcasted_iota(jnp.int32, xc.shape, 1)
        ahead = pltpu.roll(xc, LANES - HEAD_DIM // 2, axis=1)
        behind = pltpu.roll(xc, HEAD_DIM // 2, axis=1)
        rot = jnp.where((lane & (HEAD_DIM // 2)) == 0, ahead, behind)
        outs.append(xc * cos[:, sl] + rot * sin_signed[:, sl])
    return jnp.concatenate(outs, axis=-1)


def _in_proj_body(x_ref, sh_ref, sc_ref, w_ref, cos_ref, sin_ref, *out_refs, bb, tt, widths, n_paired, runs):
    d = x_ref.shape[-1]
    x = x_ref[...].reshape(bb, tt, d)
    h = _rms(x) * (1.0 + sc_ref[...]) + sh_ref[...]
    hb = h.reshape(bb * tt, d).astype(BF16)
    for first_out, n_out in runs:
        off, wd = sum(widths[:first_out]), widths[first_out]
        group = _dot(hb, w_ref[:, off:off + wd * n_out])
        for n in range(n_out):
            i = first_out + n
            p = group[:, n * wd:(n + 1) * wd]
            if i < 2:
                p = _rope(p, cos_ref[...], sin_ref[...])
            if i < n_paired:
                for c in range(wd // LANES):
                    out_refs[i][c] = p[:, c * LANES:(c + 1) * LANES]
            else:
                out_refs[i][...] = p


def _equal_runs(widths):
    runs = []
    for i, wd in enumerate(widths):
        if runs and widths[runs[-1][0]] == wd:
            runs[-1][1] += 1
        else:
            runs.append([i, 1])
    return tuple((a, n) for a, n in runs)


def _in_proj(x2, planes, w_in_bf, layer, cos_t, sin_t, batch, seq, widths, n_paired):
    n, d = x2.shape
    bb, tt = _token_tile(batch, seq)
    tm = bb * tt
    n_t = seq // tt
    table_blocks = cos_t.shape[0] // tm
    body = functools.partial(_in_proj_body, bb=bb, tt=tt, widths=widths, n_paired=n_paired, runs=_equal_runs(widths))
    tab_spec = pl.BlockSpec((tm, cos_t.shape[1]), lambda i: (i % table_blocks, 0))
    out_specs, out_shape = [], []
    for idx, wd in enumerate(widths):
        if idx < n_paired:
            out_specs.append(pl.BlockSpec((wd // LANES, tm, LANES), lambda i: (0, i, 0)))
            out_shape.append(jax.ShapeDtypeStruct((wd // LANES, n, LANES), F32))
        else:
            out_specs.append(pl.BlockSpec((tm, wd), lambda i: (i, 0)))
            out_shape.append(jax.ShapeDtypeStruct((n, wd), F32))
    return pl.pallas_call(
        body,
        grid=(n // tm,),
        in_specs=[pl.BlockSpec((tm, d), lambda i: (i, 0)),
                  _mod_spec(layer, SHIFT1, bb, d, n_t), _mod_spec(layer, SCALE1, bb, d, n_t),
                  pl.BlockSpec((None,) + w_in_bf.shape[1:], lambda i: (layer, 0, 0)), tab_spec, tab_spec],
        out_specs=out_specs,
        out_shape=out_shape,
        compiler_params=_params(("arbitrary",)),
        name="in_proj",
    )(x2, planes, planes, w_in_bf, cos_t, sin_t)


def _prompt_attn_body(q_ref, k_ref, v_ref, o_ref, m_ref, l_ref, acc_ref):
    n_pairs, seq, _ = q_ref.shape
    blk = WINDOW_STEPS
    lane = lax.broadcasted_iota(jnp.int32, (blk, LANES), 1)
    first = lane < HEAD_DIM
    scale = HEAD_DIM ** -0.5 * LOG2_E

    def rows(start, dil):
        return pl.ds(start, blk) if dil == 1 else pl.ds(start, blk, stride=dil)

    order = sorted(DILATIONS, reverse=True)
    for b_idx, dil in enumerate(order):
        n_blk = seq // dil // blk
        has_prev = n_blk > 1
        n_keys = 2 * blk if has_prev else blk
        q_row = lax.broadcasted_iota(jnp.int32, (2 * blk, n_keys), 0) & (blk - 1)
        k_col = lax.broadcasted_iota(jnp.int32, (2 * blk, n_keys), 1)
        is_first, is_last = b_idx == 0, b_idx == len(DILATIONS) - 1

        def unit(u, carry, dil=dil, has_prev=has_prev, q_row=q_row, k_col=k_col,
                 is_first=is_first, is_last=is_last):
            res = u & (dil - 1)
            i = u >> (dil.bit_length() - 1)
            start = res + dil * blk * i
            if dil == 1:
                start = pl.multiple_of(start, blk)
            if has_prev:
                prev_start = jnp.maximum(start - dil * blk, res)
                lo = jnp.maximum(q_row, jnp.where(i > 0, 0, blk))
                valid = (k_col >= lo) & (k_col <= q_row + blk)
            else:
                valid = k_col <= q_row
            loaded = []
            for p in range(n_pairs):
                q = q_ref[p, rows(start, dil), :] * scale
                kc = k_ref[p, rows(start, dil), :]
                vc = v_ref[p, rows(start, dil), :]
                if has_prev:
                    kc = jnp.concatenate([k_ref[p, rows(prev_start, dil), :], kc], axis=0)
                    vc = jnp.concatenate([v_ref[p, rows(prev_start, dil), :], vc], axis=0)
                old = None
                if not is_first:
                    old = (m_ref[p, rows(start, dil), :], l_ref[p, rows(start, dil), :],
                           acc_ref[p, rows(start, dil), :])
                loaded.append((q, kc, vc, old))
            scores = []
            for q, kc, vc, old in loaded:
                q2 = jnp.concatenate([jnp.where(first, q, 0.0), jnp.where(first, 0.0, q)], axis=0)
                scores.append(_dot_nt(q2.astype(BF16), kc.astype(BF16)))
            results = []
            for (q, kc, vc, old), s in zip(loaded, scores):
                s = jnp.where(valid, s, -jnp.inf)
                m2 = jnp.max(s, axis=-1, keepdims=True)
                e = jnp.exp2(s - m2).astype(BF16)
                v1 = jnp.concatenate([vc.astype(BF16), jnp.ones(vc.shape, BF16)], axis=1)
                pv = _dot(e, v1)
                m_b = jnp.where(first, m2[:blk], m2[blk:])
                l_b = jnp.where(first, pv[:blk, LANES:], pv[blk:, LANES:])
                acc_b = jnp.where(first, pv[:blk, :LANES], pv[blk:, :LANES])
                if not is_first:
                    m_o, l_o, acc_o = old
                    m_n = jnp.maximum(m_o, m_b)
                    w_o, w_b = jnp.exp2(m_o - m_n), jnp.exp2(m_b - m_n)
                    l_b = w_o * l_o + w_b * l_b
                    acc_b = w_o * acc_o + w_b * acc_b
                    m_b = m_n
                results.append((m_b, l_b, acc_b))
            for p, (m_b, l_b, acc_b) in enumerate(results):
                if is_last:
                    o_ref[p, rows(start, dil), :] = acc_b / l_b
                else:
                    m_ref[p, rows(start, dil), :] = m_b
                    l_ref[p, rows(start, dil), :] = l_b
                    acc_ref[p, rows(start, dil), :] = acc_b
            return carry

        lax.fori_loop(0, dil * n_blk, unit, 0, unroll=8)


def _prompt_attention(q3, k3, v3, batch, seq):
    n_pairs = q3.shape[0]
    assert seq % (WINDOW_STEPS * max(DILATIONS)) == 0
    spec = pl.BlockSpec((n_pairs, seq, LANES), lambda b: (0, b, 0))
    return pl.pallas_call(
        _prompt_attn_body,
        grid=(batch,),
        in_specs=[spec, spec, spec],
        out_specs=spec,
        out_shape=jax.ShapeDtypeStruct(q3.shape, F32),
        scratch_shapes=[pltpu.VMEM((n_pairs, seq, LANES), F32)] * 3,
        compiler_params=_params(("arbitrary",)),
        name="prompt_attn",
    )(q3, k3, v3)


def _branch_count(delta):
    cnt = jnp.zeros(delta.shape, F32)
    for dil in DILATIONS:
        hit = (delta >= 0) & (delta <= WINDOW_STEPS * dil) & ((delta & (dil - 1)) == 0)
        cnt = cnt + jnp.where(hit, 1.0, 0.0)
    return cnt


def _cache_attn_body(q_ref, kn_ref, vn_ref, kt_ref, vt_ref, o_ref, *, past, t_new):
    n_pairs = q_ref.shape[0]
    t_c = lax.broadcasted_iota(jnp.int32, (t_new, past), 0)
    j_c = lax.broadcasted_iota(jnp.int32, (t_new, past), 1)
    cnt_c = _branch_count(past + t_c - j_c)
    t_n = lax.broadcasted_iota(jnp.int32, (t_new, t_new), 0)
    j_n = lax.broadcasted_iota(jnp.int32, (t_new, t_new), 1)
    cnt_n = _branch_count(t_n - j_n)
    scale = HEAD_DIM ** -0.5
    for bi in range(kt_ref.shape[1]):
        rs = slice(bi * t_new, (bi + 1) * t_new)
        for p in range(n_pairs):
            q_pair, kn_pair, vn_pair = q_ref[p, rs, :] * scale, kn_ref[p, rs, :], vn_ref[p, rs, :]
            outs = []
            for hh in range(LANES // HEAD_DIM):
                h = p * (LANES // HEAD_DIM) + hh
                sl = slice(hh * HEAD_DIM, (hh + 1) * HEAD_DIM)
                q = q_pair[:, sl]
                s_c = jnp.where(cnt_c > 0, _dot(q.astype(BF16), kt_ref[0, bi, h].astype(BF16)), -jnp.inf)
                s_n = jnp.where(cnt_n > 0, _dot_nt(q, kn_pair[:, sl]), -jnp.inf)
                m = jnp.maximum(jnp.max(s_c, axis=-1, keepdims=True), jnp.max(s_n, axis=-1, keepdims=True))
                p_c = cnt_c * jnp.exp(s_c - m)
                p_n = cnt_n * jnp.exp(s_n - m)
                den = jnp.sum(p_c, axis=-1, keepdims=True) + jnp.sum(p_n, axis=-1, keepdims=True)
                acc = _dot_nt(p_c.astype(BF16), vt_ref[0, bi, h].astype(BF16)) + _dot(p_n, vn_pair[:, sl])
                outs.append(acc / den)
            o_ref[p, rs, :] = jnp.concatenate(outs, axis=-1)


def _cache_attention(q3, kn3, vn3, cache_k, cache_v, layer, batch, seq):
    depth, _, past, n_heads, hd = cache_k.shape
    n_pairs = q3.shape[0]
    bb = CACHE_BATCH if batch % CACHE_BATCH == 0 else 1
    by_head = lambda a: jnp.transpose(a, (0, 1, 3, 4, 2))
    new = pl.BlockSpec((n_pairs, bb * seq, LANES), lambda b: (0, b, 0))
    old = pl.BlockSpec((1, bb, n_heads, hd, past), lambda b: (layer, b, 0, 0, 0))
    return pl.pallas_call(
        functools.partial(_cache_attn_body, past=past, t_new=seq),
        grid=(batch // bb,),
        in_specs=[new, new, new, old, old],
        out_specs=new,
        out_shape=jax.ShapeDtypeStruct(q3.shape, F32),
        compiler_params=_params(("arbitrary",)),
        name="cache_attn",
    )(q3, kn3, vn3, by_head(cache_k), by_head(cache_v))


def _lower_bound(logits, layer):
    e = jnp.exp(logits - jnp.max(logits, axis=0, keepdims=True))
    sm = e / jnp.sum(e, axis=0, keepdims=True)
    lb = jnp.zeros_like(sm[0:1])
    for j in range(1, layer + 1):
        lb = lb + sm[j:j + 1]
    return lb


def _hgrn_body(lbl_ref, nw_ref, q_ref, f_ref, i_ref, g_ref, s0_ref, o_ref, sout_ref,
               st_ref, b2_ref, kk_ref, eb_ref, qt_ref, kh_ref, o_s_ref, *, layer):
    c = REC_CHUNK
    rows, d_rec = q_ref.shape[1], q_ref.shape[2]
    n_pairs = d_rec // LANES
    n_chunks = rows // c
    grp = min(rows, LANES)

    @pl.when(pl.program_id(1) == 0)
    def _():
        st_ref[...] = s0_ref[0]

    lb = _lower_bound(lbl_ref[...], layer)
    log_lb = jnp.log(lb)
    z = f_ref[0]
    t = jnp.exp(-jnp.abs(z))
    log_sig = jnp.minimum(z, 0.0) - jnp.log(1.0 + t)
    b_term = jnp.log1p(-lb) + log_sig
    log_f = jnp.maximum(log_lb, b_term) + jnp.log(1.0 + jnp.exp(-jnp.abs(log_lb - b_term)))
    inv = 1.0 / (1.0 + t)
    kk = (1.0 - lb) * jnp.where(z > 0.0, t * inv, inv)
    r_i = lax.broadcasted_iota(jnp.int32, (grp, grp), 0)
    c_i = lax.broadcasted_iota(jnp.int32, (grp, grp), 1)
    same_chunk = (r_i >> (c.bit_length() - 1)) == (c_i >> (c.bit_length() - 1))
    prefix = jnp.where(same_chunk & (c_i <= r_i), 1.0, 0.0).astype(BF16)
    suffix = jnp.where(same_chunk & (c_i > r_i), 1.0, 0.0).astype(BF16)
    for g in range(rows // grp):
        gs = slice(g * grp, (g + 1) * grp)
        parts = _split3(log_f[gs])
        b = sum(_dot(prefix, part) for part in parts)
        r = sum(_dot(suffix, part) for part in parts)
        eb = jnp.exp(b)
        b2_ref[gs, :] = b * LOG2_E
        eb_ref[gs, :] = eb
        qt_ref[gs, :] = (q_ref[0, gs, :] * eb).astype(BF16)
        kh_ref[gs, :] = (kk[gs] * jnp.exp(r)).astype(BF16)
    kk_ref[...] = kk

    l_r = lax.broadcasted_iota(jnp.int32, (LANES, LANES), 0) >> (HEAD_DIM.bit_length() - 1)
    l_c = lax.broadcasted_iota(jnp.int32, (LANES, LANES), 1) >> (HEAD_DIM.bit_length() - 1)
    same_head = l_r == l_c
    head_sum = jnp.where(same_head, 1.0, 0.0).astype(BF16)
    row = lax.broadcasted_iota(jnp.int32, (c, LANES), 0)

    def chunk(ci, carry):
        r0 = pl.multiple_of(ci * c, c)
        rs = pl.ds(r0, c)
        xs = []
        for p in range(n_pairs):
            sl = slice(p * LANES, (p + 1) * LANES)
            b2, q, k = b2_ref[rs, sl], q_ref[0, rs, sl], kk_ref[rs, sl]
            for s_i in range(c):
                cap = jnp.where(row >= s_i, 0.0, -jnp.inf)
                e = jnp.exp2(jnp.minimum(b2 - b2[s_i:s_i + 1], cap))
                xs.append((q * e) * k[s_i:s_i + 1])
        a_rep = _dot(jnp.concatenate(xs, axis=0).astype(BF16), head_sum)
        for p in range(n_pairs):
            sl = slice(p * LANES, (p + 1) * LANES)
            v = i_ref[0, rs, sl]
            base = p * c * c
            o = a_rep[base:base + c] * v[0:1]
            for s_i in range(1, c):
                o = o + a_rep[base + s_i * c:base + (s_i + 1) * c] * v[s_i:s_i + 1]
            st = st_ref[p]
            o = o + _dot_nt(qt_ref[rs, sl], st.astype(BF16))
            upd = _dot_tn(v.astype(BF16), kh_ref[rs, sl])
            st_ref[p] = st * eb_ref[rs, sl][c - 1:c] + jnp.where(same_head, upd, 0.0)
            o_s_ref[rs, sl] = o
        return carry

    lax.fori_loop(0, n_chunks, chunk, 0, unroll=UNROLL if n_chunks % UNROLL == 0 else 1)

    g_all = g_ref[0]
    outs = []
    for p in range(n_pairs):
        sl = slice(p * LANES, (p + 1) * LANES)
        o = o_s_ref[:, sl]
        sq_hi, sq_mid, _ = _split3(o * o)
        ms = (_dot(sq_hi, head_sum) + _dot(sq_mid, head_sum)) * (1.0 / HEAD_DIM)
        outs.append(o * lax.rsqrt(ms + NORM_EPS) * nw_ref[:, sl] * _silu(g_all[:, sl]))
    o_ref[0] = jnp.concatenate(outs, axis=-1)

    @pl.when(pl.program_id(1) == pl.num_programs(1) - 1)
    def _():
        sout_ref[0] = st_ref[...]


def _pair_state(s):
    b, h, kd, vd = s.shape
    st = jnp.swapaxes(s, 2, 3).reshape(b, h // 2, 2, vd, kd)
    z = jnp.zeros_like(st[:, :, 0])
    top = jnp.concatenate([st[:, :, 0], z], axis=-1)
    bot = jnp.concatenate([z, st[:, :, 1]], axis=-1)
    return jnp.concatenate([top, bot], axis=-2)


def _unpair_state(sp):
    b, hp = sp.shape[:2]
    s0 = sp[:, :, :HEAD_DIM, :HEAD_DIM]
    s1 = sp[:, :, HEAD_DIM:, HEAD_DIM:]
    st = jnp.stack([s0, s1], axis=2).reshape(b, 2 * hp, HEAD_DIM, HEAD_DIM)
    return jnp.swapaxes(st, 2, 3)


def _hgrn(qr, fr, ir, gr, s0, lb_logits, norm_w, layer, batch, seq):
    d_rec = qr.shape[-1]
    n_pairs = d_rec // LANES
    tb = min(seq, ROW_TILE)
    assert seq % tb == 0 and tb % LANES == 0 and LANES % REC_CHUNK == 0
    view = lambda a: a.reshape(batch, seq, d_rec)
    tok = pl.BlockSpec((1, tb, d_rec), lambda b, t: (b, t, 0))
    state = pl.BlockSpec((1, n_pairs, LANES, LANES), lambda b, t: (b, 0, 0, 0))
    o, s_new = pl.pallas_call(
        functools.partial(_hgrn_body, layer=layer),
        grid=(batch, seq // tb),
        in_specs=[pl.BlockSpec(lb_logits.shape, lambda b, t: (0, 0)),
                  pl.BlockSpec((1, d_rec), lambda b, t: (0, 0)),
                  tok, tok, tok, tok, state],
        out_specs=[tok, state],
        out_shape=[jax.ShapeDtypeStruct((batch, seq, d_rec), F32),
                   jax.ShapeDtypeStruct((batch, n_pairs, LANES, LANES), F32)],
        scratch_shapes=[pltpu.VMEM((n_pairs, LANES, LANES), F32)]
        + [pltpu.VMEM((tb, d_rec), F32)] * 3 + [pltpu.VMEM((tb, d_rec), BF16)] * 2
        + [pltpu.VMEM((tb, d_rec), F32)],
        compiler_params=_params(("arbitrary", "arbitrary")),
        name="hgrn",
    )(lb_logits, norm_w.reshape(1, d_rec), view(qr), view(fr), view(ir), view(gr), _pair_state(s0))
    return o.reshape(batch * seq, d_rec), _unpair_state(s_new)


def _hgrn_step_body(lblt_ref, nwt_ref, q_ref, f_ref, i_ref, g_ref, s0_ref, o_ref, sout_ref, *, layer):
    n_t, hd, _ = q_ref.shape
    logits = lblt_ref[...]
    e = jnp.exp(logits - jnp.max(logits, axis=1, keepdims=True))
    sm = e / jnp.sum(e, axis=1, keepdims=True)
    lb = jnp.zeros_like(sm[:, 0:1])
    for j in range(1, layer + 1):
        lb = lb + sm[:, j:j + 1]
    for t in range(n_t):
        z = f_ref[t]
        f_t = lb + (1.0 - lb) * jax.nn.sigmoid(z)
        k_t = (1.0 - lb) * jax.nn.sigmoid(-z)
        q_t, v_t = q_ref[t], i_ref[t]
        src = s0_ref if t == 0 else sout_ref
        o_t = jnp.zeros(v_t.shape, F32)
        for k in range(hd):
            s_k = f_t[k:k + 1] * src[0, 0, k] + k_t[k:k + 1] * v_t
            sout_ref[0, 0, k] = s_k
            o_t = o_t + s_k * q_t[k:k + 1]
        ms = jnp.mean(o_t * o_t, axis=0, keepdims=True)
        o_ref[t] = o_t * lax.rsqrt(ms + NORM_EPS) * nwt_ref[...] * _silu(g_ref[t])


def _hgrn_step(qr, fr, ir, gr, state_all, lb_logits, norm_w, layer, batch, seq):
    d_rec = qr.shape[-1]
    n_heads = d_rec // HEAD_DIM
    to_lanes = lambda a: jnp.transpose(a.reshape(batch, seq, d_rec), (1, 2, 0))
    tok = pl.BlockSpec((seq, HEAD_DIM, batch), lambda h: (0, h, 0))
    state_in = pl.BlockSpec((1, 1, HEAD_DIM, HEAD_DIM, batch), lambda h: (layer, h, 0, 0, 0))
    state_out = pl.BlockSpec((1, 1, HEAD_DIM, HEAD_DIM, batch), lambda h: (0, h, 0, 0, 0))
    o_t, s_new = pl.pallas_call(
        functools.partial(_hgrn_step_body, layer=layer),
        grid=(n_heads,),
        in_specs=[pl.BlockSpec((HEAD_DIM, lb_logits.shape[0]), lambda h: (h, 0)),
                  pl.BlockSpec((HEAD_DIM, 1), lambda h: (h, 0)),
                  tok, tok, tok, tok, state_in],
        out_specs=[tok, state_out],
        out_shape=[jax.ShapeDtypeStruct((seq, d_rec, batch), F32),
                   jax.ShapeDtypeStruct((1, n_heads, HEAD_DIM, HEAD_DIM, batch), F32)],
        compiler_params=_params(("arbitrary",)),
        name="hgrn_step",
    )(lb_logits.T, norm_w.reshape(d_rec, 1), to_lanes(qr), to_lanes(fr), to_lanes(ir), to_lanes(gr),
      jnp.transpose(state_all, (0, 2, 3, 4, 1)))
    o = jnp.transpose(o_t, (2, 0, 1)).reshape(batch * seq, d_rec)
    return o, jnp.transpose(s_new[0], (3, 0, 1, 2))


CONV_PAD = 8


def _conv_body(bc_ref, cc_ref, xc_ref, buf_ref, w_ref, b_ref, oc_ref, new_ref, up_ref):
    seq = cc_ref.shape[1]
    width = w_ref.shape[0]
    u = cc_ref[0] * xc_ref[0]
    up_ref[pl.ds(CONV_PAD, seq), :] = u
    up_ref[pl.ds(CONV_PAD - (width - 1), width - 1), :] = buf_ref[0]
    y = u * w_ref[width - 1:width, :] + b_ref[...]
    for i in range(width - 1):
        y = y + up_ref[pl.ds(CONV_PAD - (width - 1) + i, seq), :] * w_ref[i:i + 1, :]
    oc_ref[0] = bc_ref[0] * y
    new_ref[0] = up_ref[pl.ds(CONV_PAD + seq - (width - 1), width - 1), :]


def _short_conv(bc, cc, xc, buf, conv_w, conv_b, batch, seq):
    d_conv = bc.shape[-1]
    width = conv_w.shape[0]
    view = lambda a: a.reshape(batch, seq, d_conv)
    tok = pl.BlockSpec((1, seq, d_conv), lambda b: (b, 0, 0))
    tail = pl.BlockSpec((1, width - 1, d_conv), lambda b: (b, 0, 0))
    oc, new = pl.pallas_call(
        _conv_body,
        grid=(batch,),
        in_specs=[tok, tok, tok, tail,
                  pl.BlockSpec((width, d_conv), lambda b: (0, 0)),
                  pl.BlockSpec((1, d_conv), lambda b: (0, 0))],
        out_specs=[tok, tail],
        out_shape=[jax.ShapeDtypeStruct((batch, seq, d_conv), F32),
                   jax.ShapeDtypeStruct((batch, width - 1, d_conv), F32)],
        scratch_shapes=[pltpu.VMEM((CONV_PAD + seq, d_conv), F32)],
        compiler_params=_params(("arbitrary",)),
        name="short_conv",
    )(view(bc), view(cc), view(xc), buf, conv_w, conv_b.reshape(1, d_conv))
    return oc.reshape(batch * seq, d_conv), new


def _mix_ffn_body(oa_ref, or_ref, oc_ref, x_ref, g1_ref, sh_ref, sc_ref, g2_ref, w_ref, wi_ref, wo_ref,
                  fw_ref, o_ref, *, bb, tt, final, cuts):
    d = x_ref.shape[-1]
    d_ff = wo_ref.shape[0]
    n_pairs = oa_ref.shape[0]
    d_att, d_rec = n_pairs * LANES, or_ref.shape[-1]
    mix = (_dot(or_ref[...].astype(BF16), w_ref[d_att:d_att + d_rec, :])
           + _dot(oc_ref[...].astype(BF16), w_ref[d_att + d_rec:, :]))
    for p in range(n_pairs):
        mix = mix + _dot(oa_ref[p].astype(BF16), w_ref[p * LANES:(p + 1) * LANES, :])
    x = x_ref[...].reshape(bb, tt, d) + g1_ref[...] * mix.reshape(bb, tt, d)
    h = (_rms(x) * (1.0 + sc_ref[...]) + sh_ref[...]).reshape(bb * tt, d).astype(BF16)
    acc = None
    for c0, c1 in zip(cuts[:-1], cuts[1:]):
        a = (_silu(_dot(h, wi_ref[:, c0:c1])) * _dot(h, wi_ref[:, d_ff + c0:d_ff + c1])).astype(BF16)
        part = _dot(a, wo_ref[c0:c1, :])
        acc = part if acc is None else acc + part
    x = x + g2_ref[...] * acc.reshape(bb, tt, d)
    if final:
        x = _rms(x) * fw_ref[...]
    o_ref[...] = x.reshape(bb * tt, d)


def _mix_ffn(oa3, o_r, oc, x2, planes, w_out_bf, w_ffn_in_bf, w_ffn_out_bf, layer, final_w, batch, seq, final):
    n, d = x2.shape
    d_ff = w_ffn_out_bf.shape[1]
    bb, tt = _token_tile(batch, seq)
    tm = bb * tt
    n_t = seq // tt
    assert d_ff % MXU_WIDTH == 0
    n_cut = -(-d_ff // FFN_COLS)
    cuts = tuple(round(i * (d_ff // MXU_WIDTH) / n_cut) * MXU_WIDTH for i in range(n_cut + 1))
    tok = lambda a: pl.BlockSpec((tm, a.shape[-1]), lambda i: (i, 0))
    held = lambda a: pl.BlockSpec((None,) + a.shape[1:], lambda i: (layer, 0, 0), pipeline_mode=pl.Buffered(1))
    return pl.pallas_call(
        functools.partial(_mix_ffn_body, bb=bb, tt=tt, final=final, cuts=cuts),
        grid=(n // tm,),
        in_specs=[pl.BlockSpec((oa3.shape[0], tm, LANES), lambda i: (0, i, 0)), tok(o_r), tok(oc), tok(x2)]
        + [_mod_spec(layer, which, bb, d, n_t) for which in (GATE1, SHIFT2, SCALE2, GATE2)]
        + [held(w_out_bf), held(w_ffn_in_bf), held(w_ffn_out_bf), pl.BlockSpec((1, d), lambda i: (0, 0))],
        out_specs=pl.BlockSpec((tm, d), lambda i: (i, 0)),
        out_shape=jax.ShapeDtypeStruct((n, d), F32),
        compiler_params=_params(("arbitrary",)),
        name="mix_ffn",
    )(oa3, o_r, oc, x2, planes, planes, planes, planes, w_out_bf, w_ffn_in_bf, w_ffn_out_bf, final_w.reshape(1, d))


def _kv_layout_body(*refs, depth):
    ins, outs = refs[:2 * depth], refs[2 * depth:]
    for which, o_ref in enumerate(outs):
        for l in range(depth):
            src = ins[which * depth + l]
            for p in range(src.shape[0]):
                o_ref[l, 0, p * LANES:(p + 1) * LANES, :] = src[p].T


def _kv_layout(ks, vs, batch, seq):
    depth = len(ks)
    n_pairs = ks[0].shape[0]
    d_att = n_pairs * LANES
    tt = min(seq, ROW_TILE)
    n_t = seq // tt
    src = pl.BlockSpec((n_pairs, tt, LANES), lambda b, j: (0, b * n_t + j, 0))
    dst = pl.BlockSpec((depth, 1, d_att, tt), lambda b, j: (0, b, 0, j))
    shape = jax.ShapeDtypeStruct((depth, batch, d_att, seq), F32)
    k_t, v_t = pl.pallas_call(
        functools.partial(_kv_layout_body, depth=depth),
        grid=(batch, n_t),
        in_specs=[src] * (2 * depth),
        out_specs=[dst, dst],
        out_shape=[shape, shape],
        compiler_params=_params(("arbitrary", "arbitrary")),
        name="kv_layout",
    )(*ks, *vs)
    heads = lambda a: jnp.transpose(a.reshape(depth, batch, d_att // HEAD_DIM, HEAD_DIM, seq), (0, 1, 4, 2, 3))
    return heads(k_t), heads(v_t)


def _trunk(x, mods, pos, cache_k, cache_v, state_hgrn, state_conv, weights):
    (w_in, conv_w, conv_b, lb_logits, hgrn_norm_w, w_out, w_ffn_in, w_ffn_out, final_norm_w) = weights
    batch, seq, d = x.shape
    depth = w_in.shape[0]
    d_conv = conv_w.shape[-1]
    d_att = d_rec = (w_in.shape[-1] - 3 * d_conv) // 7
    n_heads = d_att // HEAD_DIM
    widths = (d_att,) * 3 + (d_rec,) * 4 + (d_conv,) * 3
    bb, tt = _token_tile(batch, seq)
    cos_t, sin_t = _rope_tables(pos, n_heads)
    if bb > 1:
        cos_t, sin_t = jnp.tile(cos_t, (bb, 1)), jnp.tile(sin_t, (bb, 1))
    x2 = x.reshape(batch * seq, d)
    planes = _mod_planes(mods, d)
    ks, vs, hs, cs = [], [], [], []
    for l in range(depth):
        qa, ka, va, qr, fr, ir, gr, bc, cc, xc = _in_proj(
            x2, planes, w_in, l, cos_t, sin_t, batch, seq, widths, 3)
        if cache_k is None:
            oa = _prompt_attention(qa, ka, va, batch, seq)
            s0 = jnp.zeros((batch, n_heads, HEAD_DIM, HEAD_DIM), F32)
            conv0 = jnp.zeros((batch, conv_w.shape[1] - 1, d_conv), F32)
        else:
            oa = _cache_attention(qa, ka, va, cache_k, cache_v, l, batch, seq)
            conv0 = state_conv[l]
        if cache_k is None:
            o_r, s_new = _hgrn(qr, fr, ir, gr, s0, lb_logits, hgrn_norm_w[l], l, batch, seq)
        else:
            o_r, s_new = _hgrn_step(qr, fr, ir, gr, state_hgrn, lb_logits, hgrn_norm_w[l], l, batch, seq)
        oc, conv_new = _short_conv(bc, cc, xc, conv0, conv_w[l], conv_b[l], batch, seq)
        x2 = _mix_ffn(oa, o_r, oc, x2, planes, w_out, w_ffn_in, w_ffn_out, l, final_norm_w, batch, seq, l == depth - 1)
        ks.append(ka)
        vs.append(va)
        hs.append(s_new)
        cs.append(conv_new)
    keep = min(WINDOW_STEPS * max(DILATIONS), seq)
    if seq % LANES == 0:
        k_out, v_out = _kv_layout(ks, vs, batch, seq)
    else:
        unpair = lambda parts: jnp.transpose(
            jnp.stack(parts).reshape(depth, n_heads // 2, batch, seq, 2, HEAD_DIM),
            (0, 2, 3, 1, 4, 5)).reshape(depth, batch, seq, n_heads, HEAD_DIM)
        k_out, v_out = unpair(ks), unpair(vs)
    return (x2.reshape(batch, seq, d), k_out[:, :, seq - keep:], v_out[:, :, seq - keep:],
            jnp.stack(hs), jnp.stack(cs))


def kernel(x_prompt, x_sample, cache_k, cache_v, state_hgrn, state_conv, c_prompt, c_sample,
           w_ada, b_ada, w_in, conv_w, conv_b, hgrn_lb_logits, hgrn_norm_w, w_out,
           w_ffn_in, w_ffn_out, final_norm_w):
    past_len = PAST_LEN
    assert cache_k.shape[2] == min(WINDOW_STEPS * max(DILATIONS), PAST_LEN) == PAST_LEN
    mods_p, mods_s = _ada((c_prompt, c_sample), w_ada, b_ada)
    weights = (w_in.astype(BF16), conv_w, conv_b, hgrn_lb_logits.astype(F32), hgrn_norm_w,
               w_out.astype(BF16), w_ffn_in.astype(BF16), w_ffn_out.astype(BF16), final_norm_w)
    pos_p = jnp.arange(x_prompt.shape[1], dtype=jnp.int32)
    pos_s = past_len + jnp.arange(x_sample.shape[1], dtype=jnp.int32)
    out_p = _trunk(x_prompt, mods_p, pos_p, None, None, None, None, weights)
    out_s = _trunk(x_sample, mods_s, pos_s, cache_k, cache_v, state_hgrn, state_conv, weights)
    return (out_p[0], out_s[0]) + out_p[1:] + out_s[1:]
```
